```python
import jax, jax.numpy as jnp
from jax import lax
import numpy as np

D_MODEL = 2048
BATCH = 1
SEQ = 16384
DEPTH = 1
DEC_BATCH = 32
DEC_SEQ = 16
PAST_LEN = 2048

CHUNK = 64
MIX_WIDTH = D_MODEL
GLA_WIDTH = MIX_WIDTH // 2
MLSTM_WIDTH = MIX_WIDTH - GLA_WIDTH
GLA_HEADS = 4
GLA_DV = GLA_WIDTH // GLA_HEADS
GLA_DK = GLA_DV // 2
GLA_GATE_RANK = 16
GLA_GATE_NORM = 16.0
MLSTM_HEADS = 4
MLSTM_DV = MLSTM_WIDTH // MLSTM_HEADS
MLSTM_DQK = MLSTM_DV // 2
M_QK_CH = 2 * MLSTM_HEADS * MLSTM_DQK
CONV_W = 4
N_GROUPS = 4
EXPERTS_PER_GROUP = 8
N_EXPERTS = N_GROUPS * EXPERTS_PER_GROUP
TOP_K_IN_GROUP = 2
D_EXPERT = D_MODEL // 8
EPS = 1e-6

PROJ_SIZES = (GLA_HEADS * GLA_DK, GLA_HEADS * GLA_DK, GLA_WIDTH, GLA_WIDTH, GLA_GATE_RANK,
              M_QK_CH, MLSTM_WIDTH, MLSTM_WIDTH, MLSTM_HEADS, MLSTM_HEADS)
PROJ_WIDTH = int(sum(PROJ_SIZES))
PROJ_SPLITS = tuple(int(s) for s in np.cumsum(PROJ_SIZES)[:-1])

kernel_name = 'gla_mlstm_hmoe_streaming_step'


def _rmsnorm(x, g):
    x32 = x.astype(jnp.float32)
    y = x32 * lax.rsqrt(jnp.mean(x32 * x32, axis=-1, keepdims=True) + EPS)
    return (y * g.astype(jnp.float32)).astype(x.dtype)


def _head_rmsnorm(o, g, dtype):
    B, T, H, d = o.shape
    y = o * lax.rsqrt(jnp.mean(o * o, axis=-1, keepdims=True) + EPS)
    return (y.reshape(B, T, H * d) * g.astype(jnp.float32)).astype(dtype)


def _to_chunks(t, L):
    B, T, H = t.shape[:3]
    t = t.reshape((B, T // L, L, H) + t.shape[3:])
    return jnp.moveaxis(jnp.swapaxes(t, 2, 3), 1, 0)


def _from_chunks(o):
    nc, B, H, L, d = o.shape
    return jnp.transpose(o, (1, 0, 3, 2, 4)).reshape(B, nc * L, H, d)


def _gla_chunked(q, k, v, log_a, S0):
    T = q.shape[1]
    L = min(CHUNK, T)
    causal = jnp.tril(jnp.ones((L, L), dtype=bool))

    def step(S, blk):
        qc, kc, vc, ac = blk
        b = jnp.cumsum(ac, axis=2)
        diff = b[:, :, :, None, :] - b[:, :, None, :, :]
        decay = jnp.exp(jnp.where(causal[:, :, None], diff, -jnp.inf))
        A = jnp.einsum('bhtd,bhtsd,bhsd->bhts', qc, decay, kc)
        o = (jnp.einsum('bhts,bhsv->bhtv', A, vc)
             + jnp.einsum('bhtd,bhdv->bhtv', qc * jnp.exp(b), S))
        b_end = b[:, :, -1:, :]
        S = (jnp.exp(b_end[:, :, 0, :])[..., None] * S
             + jnp.einsum('bhsd,bhsv->bhdv', kc * jnp.exp(b_end - b), vc))
        return S, o

    S, o = lax.scan(step, S0, (_to_chunks(q, L), _to_chunks(k, L), _to_chunks(v, L), _to_chunks(log_a, L)))
    return _from_chunks(o), S


def _mlstm_chunked(q, k, v, i_pre, log_f, C0, n0, m0):
    T = q.shape[1]
    L = min(CHUNK, T)
    causal = jnp.tril(jnp.ones((L, L), dtype=bool))

    def step(carry, blk):
        C, n, m = carry
        qc, kc, vc, ic, fc = blk
        F = jnp.cumsum(fc, axis=-1)
        Dm = jnp.where(causal, F[..., :, None] - F[..., None, :] + ic[..., None, :], -jnp.inf)
        inter = F + m[..., None]
        m_t = jnp.maximum(inter, jnp.max(Dm, axis=-1))
        W = jnp.exp(Dm - m_t[..., None])
        w_inter = jnp.exp(inter - m_t)
        S = jnp.einsum('bhtd,bhsd->bhts', qc, kc) * W
        num = (jnp.einsum('bhts,bhsv->bhtv', S, vc)
               + w_inter[..., None] * jnp.einsum('bhtd,bhdv->bhtv', qc, C))
        den = jnp.sum(S, axis=-1) + w_inter * jnp.einsum('bhtd,bhd->bht', qc, n)
        h = num / jnp.maximum(jnp.abs(den), jnp.exp(-m_t))[..., None]
        m_end = m_t[..., -1]
        w_s = jnp.exp(F[..., -1:] - F + ic - m_end[..., None])
        dec = w_inter[..., -1]
        C = dec[..., None, None] * C + jnp.einsum('bhsd,bhsv->bhdv', kc * w_s[..., None], vc)
        n = dec[..., None] * n + jnp.einsum('bhs,bhsd->bhd', w_s, kc)
        return (C, n, m_end), h

    (C, n, m), h = lax.scan(step, (C0, n0, m0),
                            (_to_chunks(q, L), _to_chunks(k, L), _to_chunks(v, L),
                             _to_chunks(i_pre, L), _to_chunks(log_f, L)))
    return _from_chunks(h), C, n, m


def _causal_conv(xc, buf, w, b):
    T = xc.shape[1]
    xp = jnp.concatenate([buf.astype(xc.dtype), xc], axis=1)
    y = b
    for j in range(CONV_W):
        y = y + xp[:, j:j + T] * w[j]
    return y, xp[:, T:]


def _token_mixers(h, gla_S, m_C, m_n, m_m, m_conv, w_in, w_gla_gate_up, b_gla_gate_up, g_gla_out,
                  w_mlstm_conv, b_mlstm_conv, b_mlstm_i, b_mlstm_f, g_mlstm_out, w_out):
    B, T, _ = h.shape
    f32 = jnp.float32
    dt = h.dtype
    p = h @ w_in
    gq, gk, gv, gg, gz, mqk, mv, mo, mi, mf = jnp.split(p, PROJ_SPLITS, axis=-1)

    q = gq.reshape(B, T, GLA_HEADS, GLA_DK).astype(f32) * (GLA_DK ** -0.5)
    k = gk.reshape(B, T, GLA_HEADS, GLA_DK).astype(f32)
    v = gv.reshape(B, T, GLA_HEADS, GLA_DV).astype(f32)
    log_a = jax.nn.log_sigmoid((gz @ w_gla_gate_up + b_gla_gate_up).astype(f32)) / GLA_GATE_NORM
    log_a = log_a.reshape(B, T, GLA_HEADS, GLA_DK)
    o_gla, S_new = _gla_chunked(q, k, v, log_a, gla_S.astype(f32))
    o_gla = _head_rmsnorm(o_gla, g_gla_out, dt) * jax.nn.silu(gg)

    qk, conv_new = _causal_conv(mqk, m_conv, w_mlstm_conv, b_mlstm_conv)
    qk = jax.nn.silu(qk)
    mq, mk = jnp.split(qk, 2, axis=-1)
    mq = mq.reshape(B, T, MLSTM_HEADS, MLSTM_DQK).astype(f32)
    mk = mk.reshape(B, T, MLSTM_HEADS, MLSTM_DQK).astype(f32) * (MLSTM_DQK ** -0.5)
    mvv = mv.reshape(B, T, MLSTM_HEADS, MLSTM_DV).astype(f32)
    i_pre = (mi + b_mlstm_i).astype(f32)
    log_f = jax.nn.log_sigmoid((mf + b_mlstm_f).astype(f32))
    h_m, C_new, n_new, m_new = _mlstm_chunked(mq, mk, mvv, i_pre, log_f,
                                              m_C.astype(f32), m_n.astype(f32), m_m.astype(f32))
    o_m = _head_rmsnorm(h_m, g_mlstm_out, dt) * jax.nn.sigmoid(mo)

    out = jnp.concatenate([o_gla, o_m], axis=-1) @ w_out
    return (out, S_new.astype(dt), C_new.astype(dt), n_new.astype(dt), m_new.astype(dt),
            conv_new.astype(dt))


def _hmoe(x, w_router_group, w_router_expert, w_exp_gate, w_exp_up, w_exp_down):
    B, T, D = x.shape
    f32 = jnp.float32
    xt = x.reshape(B * T, D)
    N = xt.shape[0]
    p_group = jax.nn.softmax((xt @ w_router_group).astype(f32), axis=-1)
    g_val, g_idx = lax.top_k(p_group, 1)
    le = (xt @ w_router_expert).astype(f32).reshape(N, N_GROUPS, EXPERTS_PER_GROUP)
    le_sel = jnp.take_along_axis(le, g_idx[:, :, None], axis=1)[:, 0]
    e_val, e_idx = lax.top_k(le_sel, TOP_K_IN_GROUP)
    e_w = jax.nn.softmax(e_val, axis=-1) * g_val
    e_id = g_idx * EXPERTS_PER_GROUP + e_idx
    combine = jnp.sum(jax.nn.one_hot(e_id, N_EXPERTS, dtype=f32) * e_w[..., None], axis=1)
    hg = jnp.einsum('nd,edf->nef', xt, w_exp_gate)
    hu = jnp.einsum('nd,edf->nef', xt, w_exp_up)
    act = jax.nn.silu(hg) * hu * combine[:, :, None].astype(x.dtype)
    y = jnp.einsum('nef,efd->nd', act, w_exp_down)
    return y.reshape(B, T, D)


def _trunk(x, gla_S, m_C, m_n, m_m, m_conv, g_mix_norm, w_in, w_gla_gate_up, b_gla_gate_up,
           g_gla_out, w_mlstm_conv, b_mlstm_conv, b_mlstm_i, b_mlstm_f, g_mlstm_out, w_out,
           g_ffn_norm, w_router_group, w_router_expert, w_exp_gate, w_exp_up, w_exp_down, g_final):
    s_S, s_C, s_n, s_m, s_cv = [], [], [], [], []
    for l in range(DEPTH):
        h = _rmsnorm(x, g_mix_norm[l])
        mix, S1, C1, n1, m1, cv1 = _token_mixers(
            h, gla_S[l], m_C[l], m_n[l], m_m[l], m_conv[l], w_in[l], w_gla_gate_up[l],
            b_gla_gate_up[l], g_gla_out[l], w_mlstm_conv[l], b_mlstm_conv[l], b_mlstm_i[l],
            b_mlstm_f[l], g_mlstm_out[l], w_out[l])
        x = x + mix
        x = x + _hmoe(_rmsnorm(x, g_ffn_norm[l]), w_router_group[l], w_router_expert[l],
                      w_exp_gate[l], w_exp_up[l], w_exp_down[l])
        s_S.append(S1); s_C.append(C1); s_n.append(n1); s_m.append(m1); s_cv.append(cv1)
    y = _rmsnorm(x, g_final)
    return (y, jnp.stack(s_S), jnp.stack(s_C), jnp.stack(s_n), jnp.stack(s_m), jnp.stack(s_cv))


def setup_inputs(seed: int = 0) -> dict:
    key = jax.random.key(seed)
    ks = jax.random.split(key, 32)

    def nrm(k, shape, scale):
        return jax.random.normal(k, shape, jnp.float32) * scale

    return {
        'x_prompt': nrm(ks[0], (BATCH, SEQ, D_MODEL), 1.0),
        'x_sample': nrm(ks[1], (DEC_BATCH, DEC_SEQ, D_MODEL), 1.0),
        'state_gla_S': nrm(ks[2], (DEPTH, DEC_BATCH, GLA_HEADS, GLA_DK, GLA_DV), 1.0),
        'state_mlstm_C': nrm(ks[3], (DEPTH, DEC_BATCH, MLSTM_HEADS, MLSTM_DQK, MLSTM_DV), 0.5),
        'state_mlstm_n': nrm(ks[4], (DEPTH, DEC_BATCH, MLSTM_HEADS, MLSTM_DQK), 0.5),
        'state_mlstm_m': nrm(ks[5], (DEPTH, DEC_BATCH, MLSTM_HEADS), 1.0),
        'cache_mlstm_conv': nrm(ks[6], (DEPTH, DEC_BATCH, CONV_W - 1, M_QK_CH), 1.0),
        'g_mix_norm': 1.0 + nrm(ks[7], (DEPTH, D_MODEL), 0.02),
        'w_in': nrm(ks[8], (DEPTH, D_MODEL, PROJ_WIDTH), D_MODEL ** -0.5),
        'w_gla_gate_up': nrm(ks[9], (DEPTH, GLA_GATE_RANK, GLA_HEADS * GLA_DK), GLA_GATE_RANK ** -0.5),
        'b_gla_gate_up': nrm(ks[10], (DEPTH, GLA_HEADS * GLA_DK), 0.1),
        'g_gla_out': 1.0 + nrm(ks[11], (DEPTH, GLA_WIDTH), 0.02),
        'w_mlstm_conv': nrm(ks[12], (DEPTH, CONV_W, M_QK_CH), CONV_W ** -0.5),
        'b_mlstm_conv': nrm(ks[13], (DEPTH, M_QK_CH), 0.02),
        'b_mlstm_i': nrm(ks[14], (DEPTH, MLSTM_HEADS), 0.1),
        'b_mlstm_f': jnp.linspace(3.0, 6.0, MLSTM_HEADS, dtype=jnp.float32)[None, :]
                     + nrm(ks[15], (DEPTH, MLSTM_HEADS), 0.1),
        'g_mlstm_out': 1.0 + nrm(ks[16], (DEPTH, MLSTM_WIDTH), 0.02),
        'w_out': nrm(ks[17], (DEPTH, MIX_WIDTH, D_MODEL), MIX_WIDTH ** -0.5),
        'g_ffn_norm': 1.0 + nrm(ks[18], (DEPTH, D_MODEL), 0.02),
        'w_router_group': nrm(ks[19], (DEPTH, D_MODEL, N_GROUPS), D_MODEL ** -0.5),
        'w_router_expert': nrm(ks[20], (DEPTH, D_MODEL, N_EXPERTS), D_MODEL ** -0.5),
        'w_exp_gate': nrm(ks[21], (DEPTH, N_EXPERTS, D_MODEL, D_EXPERT), D_MODEL ** -0.5),
        'w_exp_up': nrm(ks[22], (DEPTH, N_EXPERTS, D_MODEL, D_EXPERT), D_MODEL ** -0.5),
        'w_exp_down': nrm(ks[23], (DEPTH, N_EXPERTS, D_EXPERT, D_MODEL), D_EXPERT ** -0.5),
        'g_final': 1.0 + nrm(ks[24], (D_MODEL,), 0.02),
    }


def reference(x_prompt, x_sample, state_gla_S, state_mlstm_C, state_mlstm_n, state_mlstm_m,
              cache_mlstm_conv, g_mix_norm, w_in, w_gla_gate_up, b_gla_gate_up, g_gla_out,
              w_mlstm_conv, b_mlstm_conv, b_mlstm_i, b_mlstm_f, g_mlstm_out, w_out, g_ffn_norm,
              w_router_group, w_router_expert, w_exp_gate, w_exp_up, w_exp_down, g_final):
    dt = x_prompt.dtype
    Bp = x_prompt.shape[0]
    z_S = jnp.zeros((DEPTH, Bp, GLA_HEADS, GLA_DK, GLA_DV), dt)
    z_C = jnp.zeros((DEPTH, Bp, MLSTM_HEADS, MLSTM_DQK, MLSTM_DV), dt)
    z_n = jnp.zeros((DEPTH, Bp, MLSTM_HEADS, MLSTM_DQK), dt)
    z_m = jnp.zeros((DEPTH, Bp, MLSTM_HEADS), dt)
    z_cv = jnp.zeros((DEPTH, Bp, CONV_W - 1, M_QK_CH), dt)
    weights = (g_mix_norm, w_in, w_gla_gate_up, b_gla_gate_up, g_gla_out, w_mlstm_conv,
               b_mlstm_conv, b_mlstm_i, b_mlstm_f, g_mlstm_out, w_out, g_ffn_norm,
               w_router_group, w_router_expert, w_exp_gate, w_exp_up, w_exp_down, g_final)
    y_prompt, p_S, p_C, p_n, p_m, p_cv = _trunk(x_prompt, z_S, z_C, z_n, z_m, z_cv, *weights)
    y_sample, s_S, s_C, s_n, s_m, s_cv = _trunk(x_sample, state_gla_S, state_mlstm_C, state_mlstm_n,
                                                state_mlstm_m, cache_mlstm_conv, *weights)
    return (y_prompt, y_sample, p_S, p_C, p_n, p_m, p_cv, s_S, s_C, s_n, s_m, s_cv)
```

```python
import functools

import jax
import jax.numpy as jnp
from jax import lax
from jax.experimental import pallas as pl
from jax.experimental.pallas import tpu as pltpu

F32 = jnp.float32
BF16 = jnp.bfloat16

D_MODEL = 2048
N_HEADS = 4
DK = 128
DV = 256
GLA_GATE_RANK = 16
GLA_GATE_NORM = 16.0
CONV_W = 4
QK_CH = 2 * N_HEADS * DK
N_GROUPS = 4
EXPERTS_PER_GROUP = 8
N_EXPERTS = N_GROUPS * EXPERTS_PER_GROUP
D_EXPERT = 256
EPS = 1e-6

LANES = 128
GLA_SUB = 16
ROW_TILE = 512
PROJ_BIG = 6 * 1024
LANE_I = GLA_GATE_RANK
LANE_F = GLA_GATE_RANK + N_HEADS
ROUTER_E0 = N_GROUPS

VMEM_LIMIT = 56 * 1024 * 1024


def _dot(a, b):
    return jnp.dot(a, b, preferred_element_type=F32)


def _dot_nt(a, b):
    return lax.dot_general(a, b, (((1,), (1,)), ((), ())), preferred_element_type=F32)


def _dot_tn(a, b):
    return lax.dot_general(a, b, (((0,), (0,)), ((), ())), preferred_element_type=F32)


def _split2(x):
    hi = x.astype(BF16)
    lo = (x - hi.astype(F32)).astype(BF16)
    return hi, lo


def _split3(x):
    hi = x.astype(BF16)
    r = x - hi.astype(F32)
    mid = r.astype(BF16)
    lo = (r - mid.astype(F32)).astype(BF16)
    return hi, mid, lo


def _dot_exact_lhs(m_bf16, x):
    hi, mid, lo = _split3(x)
    return _dot(m_bf16, hi) + _dot(m_bf16, mid) + _dot(m_bf16, lo)


def _dot_hilo(a_hi, a_lo, b_hi, b_lo):
    return _dot(a_hi, b_hi) + _dot(a_lo, b_hi) + _dot(a_hi, b_lo)


def _log_sigmoid(z):
    return jnp.minimum(z, 0.0) - jnp.log1p(jnp.exp(-jnp.abs(z)))


def _sigmoid(z):
    return 1.0 / (1.0 + jnp.exp(-z))


def _rmsnorm(x, g):
    return x * lax.rsqrt(jnp.mean(x * x, axis=-1, keepdims=True) + EPS) * g


def _inproj_kernel(n_prompt_tiles, xp_ref, xs_ref, g_ref, w_ref, ws_hi_ref, ws_lo_ref,
                   p_ref, ps_ref, h_s):
    i = pl.program_id(0)
    j = pl.program_id(1)

    def norm(x_ref):
        y = _rmsnorm(x_ref[...], g_ref[...])
        y_hi, y_lo = _split2(y)
        h_s[...] = y_hi
        ps_ref[...] = _dot_hilo(y_hi, y_lo, ws_hi_ref[...], ws_lo_ref[...])

    @pl.when((j == 0) & (i < n_prompt_tiles))
    def _():
        norm(xp_ref)

    @pl.when((j == 0) & (i >= n_prompt_tiles))
    def _():
        norm(xs_ref)

    p_ref[...] = _dot(h_s[...], w_ref[...])


def _inproj(xp, xs, g, w_big, ws_hi, ws_lo):
    n_p, n_s = xp.shape[0], xs.shape[0]
    assert n_p % ROW_TILE == 0 and n_s == ROW_TILE
    npt = n_p // ROW_TILE
    n = n_p + n_s
    tn = 1024
    grid = (npt + 1, PROJ_BIG // tn)
    return pl.pallas_call(
        functools.partial(_inproj_kernel, npt),
        grid=grid,
        in_specs=[
            pl.BlockSpec((ROW_TILE, D_MODEL), lambda i, j: (jnp.minimum(i, npt - 1), 0)),
            pl.BlockSpec((ROW_TILE, D_MODEL), lambda i, j: (0, 0)),
            pl.BlockSpec((1, D_MODEL), lambda i, j: (0, 0)),
            pl.BlockSpec((D_MODEL, tn), lambda i, j: (0, j)),
            pl.BlockSpec((D_MODEL, LANES), lambda i, j: (0, 0)),
            pl.BlockSpec((D_MODEL, LANES), lambda i, j: (0, 0)),
        ],
        out_specs=[
            pl.BlockSpec((ROW_TILE, tn), lambda i, j: (i, j)),
            pl.BlockSpec((ROW_TILE, LANES), lambda i, j: (i, 0)),
        ],
        out_shape=[
            jax.ShapeDtypeStruct((n, PROJ_BIG), F32),
            jax.ShapeDtypeStruct((n, LANES), F32),
        ],
        scratch_shapes=[pltpu.VMEM((ROW_TILE, D_MODEL), BF16)],
        compiler_params=pltpu.CompilerParams(
            dimension_semantics=("arbitrary", "arbitrary"), vmem_limit_bytes=VMEM_LIMIT),
        name="inproj",
    )(xp, xs, g, w_big, ws_hi, ws_lo)


def _mixer_kernel(rows, gq_ref, gk_ref, gv_ref, gg_ref, mqk_ref, mv_ref, mo_ref, sm_ref,
                  s0_ref, c0_ref, n0_ref, m0_ref, cv0_ref,
                  wz_hi_ref, wz_lo_ref, bz_ref, g_gla_ref, wc_ref, bc_ref, bsm_ref, g_ml_ref,
                  o_ref, s_out_ref, c_out_ref, n_out_ref, m_out_ref, cv_out_ref,
                  st_s, c_s, n_s, m_s, cb_s, b_s):
    blk = pl.program_id(1)
    n_blk = pl.num_programs(1)

    @pl.when(blk == 0)
    def _():
        for h in range(N_HEADS):
            st_s[h] = s0_ref[0, h].T
            c_s[h] = c0_ref[0, h]
            n_s[h] = n0_ref[0, h:h + 1, :]
            m_s[h] = jnp.broadcast_to(m0_ref[0, :, h:h + 1], (1, LANES))
        cb_s[0:8, :] = jnp.zeros((8, QK_CH), F32)
        cb_s[8 - (CONV_W - 1):8, :] = cv0_ref[0]

    small = sm_ref[...]
    row_i = lax.broadcasted_iota(jnp.int32, (rows, rows), 0)
    col_i = lax.broadcasted_iota(jnp.int32, (rows, rows), 1)
    causal = col_i <= row_i

    sm_hi, sm_lo = _split2(small)
    z = _dot_hilo(sm_hi, sm_lo, wz_hi_ref[...], wz_lo_ref[...]) + bz_ref[...]
    log_a = _log_sigmoid(z) * (1.0 / GLA_GATE_NORM)
    same_sub = (row_i // GLA_SUB) == (col_i // GLA_SUB)
    blk_tri = jnp.where(causal & same_sub, 1.0, 0.0).astype(BF16)
    b_s[...] = _dot_exact_lhs(blk_tri, log_a)

    sub_r = lax.broadcasted_iota(jnp.int32, (GLA_SUB, LANES), 0)
    sub_l = lax.broadcasted_iota(jnp.int32, (GLA_SUB, LANES), 1)

    def gla_sub(c, carry):
        r0 = pl.multiple_of(c * GLA_SUB, GLA_SUB)
        rs = pl.ds(r0, GLA_SUB)
        for h in range(N_HEADS):
            kc = slice(h * DK, (h + 1) * DK)
            vc = slice(h * DV, (h + 1) * DV)
            bh = b_s[rs, kc]
            qh = gq_ref[rs, kc] * (DK ** -0.5)
            kh = gk_ref[rs, kc]
            vh = gv_ref[rs, vc].astype(BF16)
            st = st_s[h]
            o = _dot_nt((qh * jnp.exp(bh)).astype(BF16), st.astype(BF16))
            a = jnp.zeros((GLA_SUB, LANES), F32)
            for s in range(GLA_SUB):
                e = jnp.exp(jnp.minimum(bh - bh[s:s + 1, :], 0.0))
                col = jnp.sum(qh * (kh[s:s + 1, :] * e), axis=1, keepdims=True)
                a = jnp.where((sub_l == s) & (sub_r >= s), col, a)
            o = o + _dot(a[:, :GLA_SUB].astype(BF16), vh)
            b_end = bh[GLA_SUB - 1:GLA_SUB, :]
            k_dec = kh * jnp.exp(b_end - bh)
            st_s[h] = st * jnp.exp(b_end) + _dot_tn(vh, k_dec.astype(BF16))
            y = o * lax.rsqrt(jnp.mean(o * o, axis=-1, keepdims=True) + EPS) * g_gla_ref[:, vc]
            gate = gg_ref[rs, vc]
            o_ref[rs, vc] = (y * (gate * _sigmoid(gate))).astype(o_ref.dtype)
        return carry

    lax.fori_loop(0, rows // GLA_SUB, gla_sub, 0)

    cb_s[8:8 + rows, :] = mqk_ref[...]
    conv = bc_ref[...]
    for j in range(CONV_W):
        conv = conv + cb_s[8 - (CONV_W - 1) + j:8 - (CONV_W - 1) + j + rows, :] * wc_ref[j:j + 1, :]
    cb_s[0:8, :] = cb_s[rows:rows + 8, :]
    qk = conv * _sigmoid(conv)

    pre = small + bsm_ref[...]
    log_f = _log_sigmoid(pre)
    tri = jnp.where(causal, 1.0, 0.0).astype(BF16)
    f_cum = _dot_exact_lhs(tri, log_f)
    eye = jnp.where(row_i == col_i, 1.0, 0.0).astype(BF16)
    f_cum_t = sum(_dot_tn(p, eye) for p in _split3(f_cum))
    pre_t = sum(_dot_tn(p, eye) for p in _split3(pre))
    lane_1 = lax.broadcasted_iota(jnp.int32, (1, LANES), 1)
    m_new = jnp.zeros((1, LANES), F32)

    for h in range(N_HEADS):
        kc = slice(h * DK, (h + 1) * DK)
        vc = slice(h * DV, (h + 1) * DV)
        f_col = f_cum[:, LANE_F + h:LANE_F + h + 1]
        i_col = pre[:, LANE_I + h:LANE_I + h + 1]
        f_row = f_cum_t[LANE_F + h:LANE_F + h + 1, :]
        i_row = pre_t[LANE_I + h:LANE_I + h + 1, :]
        dm = jnp.where(causal, f_col - f_row + i_row, -jnp.inf)
        m_prev = m_s[h][:, 0:1]
        inter = f_col + m_prev
        m_t = jnp.maximum(inter, jnp.max(dm, axis=1, keepdims=True))
        w = jnp.exp(dm - m_t)
        w_inter = jnp.exp(inter - m_t)
        q = qk[:, kc]
        k = qk[:, N_HEADS * DK + h * DK:N_HEADS * DK + (h + 1) * DK] * (DK ** -0.5)
        v = mv_ref[:, vc].astype(BF16)
        q_b = q.astype(BF16)
        c_prev = c_s[h]
        n_prev = n_s[h]
        s_qk = _dot_nt(q_b, k.astype(BF16)) * w
        num = _dot(s_qk.astype(BF16), v) + w_inter * _dot(q_b, c_prev.astype(BF16))
        den = (jnp.sum(s_qk, axis=1, keepdims=True)
               + w_inter * jnp.sum(q * n_prev, axis=1, keepdims=True))
        hh = num / jnp.maximum(jnp.abs(den), jnp.exp(-m_t))
        m_end = m_t[rows - 1:rows, :]
        w_s = jnp.exp(f_col[rows - 1:rows, :] - f_col + i_col - m_end)
        dec = w_inter[rows - 1:rows, :]
        k_w = k * w_s
        c_s[h] = dec * c_prev + _dot_tn(k_w.astype(BF16), v)
        n_s[h] = dec * n_prev + jnp.sum(k_w, axis=0, keepdims=True)
        m_s[h] = jnp.broadcast_to(m_end, (1, LANES))
        m_new = jnp.where(lane_1 == h, m_end, m_new)
        y = hh * lax.rsqrt(jnp.mean(hh * hh, axis=-1, keepdims=True) + EPS) * g_ml_ref[:, vc]
        o_ref[:, N_HEADS * DV + h * DV:N_HEADS * DV + (h + 1) * DV] = (
            y * _sigmoid(mo_ref[:, vc])).astype(o_ref.dtype)

    @pl.when(blk == n_blk - 1)
    def _():
        for h in range(N_HEADS):
            s_out_ref[0, h] = st_s[h].T
            c_out_ref[0, h] = c_s[h]
            n_out_ref[0, h:h + 1, :] = n_s[h]
        m_out_ref[0] = m_new
        cv_out_ref[0] = cb_s[8 - (CONV_W - 1):8, :]


def _mixers(p_big, p_small, row0, n_streams, t_len, rows, s0, c0, n0, m0, cv0, wts):
    assert t_len % rows == 0 and row0 % rows == 0 and rows % GLA_SUB == 0
    n_blk = t_len // rows
    b0 = row0 // rows

    def rmap(col):
        return lambda s, b: (b0 + s * n_blk + b, col)

    def smap(*zeros):
        return lambda s, b: (s,) + zeros

    def wmap(s, b):
        return (0, 0)

    wz_hi, wz_lo, bz, g_gla, wc, bc, bsm, g_ml = wts
    in_specs = [
        pl.BlockSpec((rows, N_HEADS * DK), rmap(0)),
        pl.BlockSpec((rows, N_HEADS * DK), rmap(1)),
        pl.BlockSpec((rows, N_HEADS * DV), rmap(1)),
        pl.BlockSpec((rows, N_HEADS * DV), rmap(2)),
        pl.BlockSpec((rows, QK_CH), rmap(3)),
        pl.BlockSpec((rows, N_HEADS * DV), rmap(4)),
        pl.BlockSpec((rows, N_HEADS * DV), rmap(5)),
        pl.BlockSpec((rows, LANES), rmap(0)),
        pl.BlockSpec((1, N_HEADS, DK, DV), smap(0, 0, 0)),
        pl.BlockSpec((1, N_HEADS, DK, DV), smap(0, 0, 0)),
        pl.BlockSpec((1, N_HEADS, DK), smap(0, 0)),
        pl.BlockSpec((1, 1, N_HEADS), smap(0, 0)),
        pl.BlockSpec((1, CONV_W - 1, QK_CH), smap(0, 0)),
        pl.BlockSpec(wz_hi.shape, wmap), pl.BlockSpec(wz_lo.shape, wmap),
        pl.BlockSpec(bz.shape, wmap), pl.BlockSpec(g_gla.shape, wmap),
        pl.BlockSpec(wc.shape, wmap), pl.BlockSpec(bc.shape, wmap),
        pl.BlockSpec(bsm.shape, wmap), pl.BlockSpec(g_ml.shape, wmap),
    ]
    n_rows = n_streams * t_len
    out_shape = [
        jax.ShapeDtypeStruct((n_rows, D_MODEL), BF16),
        jax.ShapeDtypeStruct((n_streams, N_HEADS, DK, DV), F32),
        jax.ShapeDtypeStruct((n_streams, N_HEADS, DK, DV), F32),
        jax.ShapeDtypeStruct((n_streams, N_HEADS, DK), F32),
        jax.ShapeDtypeStruct((n_streams, 1, LANES), F32),
        jax.ShapeDtypeStruct((n_streams, CONV_W - 1, QK_CH), F32),
    ]
    out_specs = [
        pl.BlockSpec((rows, D_MODEL), lambda s, b: (s * n_blk + b, 0)),
        pl.BlockSpec((1, N_HEADS, DK, DV), smap(0, 0, 0)),
        pl.BlockSpec((1, N_HEADS, DK, DV), smap(0, 0, 0)),
        pl.BlockSpec((1, N_HEADS, DK), smap(0, 0)),
        pl.BlockSpec((1, 1, LANES), smap(0, 0)),
        pl.BlockSpec((1, CONV_W - 1, QK_CH), smap(0, 0)),
    ]
    scratch = [
        pltpu.VMEM((N_HEADS, DV, DK), F32),
        pltpu.VMEM((N_HEADS, DK, DV), F32),
        pltpu.VMEM((N_HEADS, 1, DK), F32),
        pltpu.VMEM((N_HEADS, 1, LANES), F32),
        pltpu.VMEM((rows + 8, QK_CH), F32),
        pltpu.VMEM((rows, N_HEADS * DK), F32),
    ]
    return pl.pallas_call(
        functools.partial(_mixer_kernel, rows),
        grid=(n_streams, n_blk),
        in_specs=in_specs, out_specs=out_specs, out_shape=out_shape,
        scratch_shapes=scratch,
        compiler_params=pltpu.CompilerParams(
            dimension_semantics=("arbitrary", "arbitrary"), vmem_limit_bytes=VMEM_LIMIT),
        name=f"mixers_r{rows}",
    )(p_big, p_big, p_big, p_big, p_big, p_big, p_big, p_small, s0, c0, n0, m0, cv0, *wts)


def _outproj_kernel(n_prompt_tiles, op_ref, os_ref, xp_ref, xs_ref, w_ref, g_ref, wr_hi_ref, wr_lo_ref,
                    x1_ref, xn_ref, comb_ref):
    i = pl.program_id(0)

    def body(o_ref, x_ref):
        x1 = x_ref[...] + _dot(o_ref[...], w_ref[...])
        x1_ref[...] = x1
        xn = _rmsnorm(x1, g_ref[...])
        xn_hi, xn_lo = _split2(xn)
        xn_ref[...] = xn_hi
        lg = _dot_hilo(xn_hi, xn_lo, wr_hi_ref[...], wr_lo_ref[...])
        lane = lax.broadcasted_iota(jnp.int32, lg.shape, 1).astype(F32)
        neg = -jnp.inf
        lgm = jnp.where(lane < N_GROUPS, lg, neg)
        mg = jnp.max(lgm, axis=1, keepdims=True)
        g_idx = jnp.min(jnp.where(lgm == mg, lane, float(LANES)), axis=1, keepdims=True)
        g_val = 1.0 / jnp.sum(jnp.where(lane < N_GROUPS, jnp.exp(lg - mg), 0.0), axis=1, keepdims=True)
        e0 = ROUTER_E0 + EXPERTS_PER_GROUP * g_idx
        le = jnp.where((lane >= e0) & (lane < e0 + EXPERTS_PER_GROUP), lg, neg)
        v1 = jnp.max(le, axis=1, keepdims=True)
        i1 = jnp.min(jnp.where(le == v1, lane, float(LANES)), axis=1, keepdims=True)
        le2 = jnp.where(lane == i1, neg, le)
        v2 = jnp.max(le2, axis=1, keepdims=True)
        i2 = jnp.min(jnp.where(le2 == v2, lane, float(LANES)), axis=1, keepdims=True)
        t = jnp.exp(v2 - v1)
        w1 = g_val / (1.0 + t)
        w2 = g_val * t / (1.0 + t)
        comb_ref[...] = jnp.where(lane == i1, w1, 0.0) + jnp.where(lane == i2, w2, 0.0)

    @pl.when(i < n_prompt_tiles)
    def _():
        body(op_ref, xp_ref)

    @pl.when(i >= n_prompt_tiles)
    def _():
        body(os_ref, xs_ref)


def _outproj(o_p, o_s, xp, xs, w_out, g, wr_hi, wr_lo):
    n_p, n_s = xp.shape[0], xs.shape[0]
    npt = n_p // ROW_TILE
    n = n_p + n_s

    def pmap(i):
        return (jnp.minimum(i, npt - 1), 0)

    def cmap(i):
        return (0, 0)

    def omap(i):
        return (i, 0)

    return pl.pallas_call(
        functools.partial(_outproj_kernel, npt),
        grid=(npt + 1,),
        in_specs=[
            pl.BlockSpec((ROW_TILE, D_MODEL), pmap), pl.BlockSpec((ROW_TILE, D_MODEL), cmap),
            pl.BlockSpec((ROW_TILE, D_MODEL), pmap), pl.BlockSpec((ROW_TILE, D_MODEL), cmap),
            pl.BlockSpec((D_MODEL, D_MODEL), cmap), pl.BlockSpec((1, D_MODEL), cmap),
            pl.BlockSpec((D_MODEL, LANES), cmap), pl.BlockSpec((D_MODEL, LANES), cmap),
        ],
        out_specs=[
            pl.BlockSpec((ROW_TILE, D_MODEL), omap), pl.BlockSpec((ROW_TILE, D_MODEL), omap),
            pl.BlockSpec((ROW_TILE, LANES), omap),
        ],
        out_shape=[
            jax.ShapeDtypeStruct((n, D_MODEL), F32),
            jax.ShapeDtypeStruct((n, D_MODEL), BF16),
            jax.ShapeDtypeStruct((n, LANES), F32),
        ],
        compiler_params=pltpu.CompilerParams(
            dimension_semantics=("arbitrary",), vmem_limit_bytes=VMEM_LIMIT),
        name="outproj_router",
    )(o_p, o_s, xp, xs, w_out, g, wr_hi, wr_lo)


def _moe_kernel(xn_ref, x1_ref, comb_ref, wg_ref, wu_ref, wd_ref, g_ref, y_ref, acc_s):
    e = pl.program_id(1)

    @pl.when(e == 0)
    def _():
        acc_s[...] = jnp.zeros_like(acc_s)

    comb = comb_ref[...]
    lane = lax.broadcasted_iota(jnp.int32, comb.shape, 1)
    c_e = jnp.sum(jnp.where(lane == e + ROUTER_E0, comb, 0.0), axis=1, keepdims=True)
    xn = xn_ref[...]
    hg = _dot(xn, wg_ref[0])
    hu = _dot(xn, wu_ref[0])
    act = hg * _sigmoid(hg) * hu * c_e
    acc_s[...] += _dot(act.astype(BF16), wd_ref[0])

    @pl.when(e == pl.num_programs(1) - 1)
    def _():
        y_ref[...] = _rmsnorm(x1_ref[...] + acc_s[...], g_ref[...])


def _moe(xn, x1, comb, wg, wu, wd, g):
    n = xn.shape[0]
    return pl.pallas_call(
        _moe_kernel,
        grid=(n // ROW_TILE, N_EXPERTS),
        in_specs=[
            pl.BlockSpec((ROW_TILE, D_MODEL), lambda i, e: (i, 0)),
            pl.BlockSpec((ROW_TILE, D_MODEL), lambda i, e: (i, 0)),
            pl.BlockSpec((ROW_TILE, LANES), lambda i, e: (i, 0)),
            pl.BlockSpec((1, D_MODEL, D_EXPERT), lambda i, e: (e, 0, 0)),
            pl.BlockSpec((1, D_MODEL, D_EXPERT), lambda i, e: (e, 0, 0)),
            pl.BlockSpec((1, D_EXPERT, D_MODEL), lambda i, e: (e, 0, 0)),
            pl.BlockSpec((1, D_MODEL), lambda i, e: (0, 0)),
        ],
        out_specs=pl.BlockSpec((ROW_TILE, D_MODEL), lambda i, e: (i, 0)),
        out_shape=jax.ShapeDtypeStruct((n, D_MODEL), F32),
        scratch_shapes=[pltpu.VMEM((ROW_TILE, D_MODEL), F32)],
        compiler_params=pltpu.CompilerParams(
            dimension_semantics=("arbitrary", "arbitrary"), vmem_limit_bytes=VMEM_LIMIT),
        name="moe_dense",
    )(xn, x1, comb, wg, wu, wd, g)


def _pad_lanes(w):
    return jnp.pad(w, ((0, 0), (0, LANES - w.shape[1])))


def kernel(x_prompt, x_sample, state_gla_S, state_mlstm_C, state_mlstm_n, state_mlstm_m, cache_mlstm_conv, g_mix_norm, w_in, w_gla_gate_up, b_gla_gate_up, g_gla_out, w_mlstm_conv, b_mlstm_conv, b_mlstm_i, b_mlstm_f, g_mlstm_out, w_out, g_ffn_norm, w_router_group, w_router_expert, w_exp_gate, w_exp_up, w_exp_down, g_final):
    depth = w_in.shape[0]
    assert depth == 1
    bp, t_p, _ = x_prompt.shape
    bs, t_s, _ = x_sample.shape
    assert bp == 1
    xp = x_prompt.reshape(bp * t_p, D_MODEL)
    xs = x_sample.reshape(bs * t_s, D_MODEL)
    n_p = xp.shape[0]

    w = w_in[0]
    c_gz = 3072
    c_mqk = c_gz + GLA_GATE_RANK
    c_mi = c_mqk + 3072
    w_big = jnp.concatenate([w[:, :c_gz], w[:, c_mqk:c_mi]], axis=1).astype(BF16)
    ws_hi, ws_lo = _split2(_pad_lanes(jnp.concatenate([w[:, c_gz:c_mqk], w[:, c_mi:]], axis=1)))
    wz = jnp.pad(w_gla_gate_up[0], ((0, LANES - GLA_GATE_RANK), (0, 0)))
    wz_hi, wz_lo = _split2(wz)
    bsm = _pad_lanes(jnp.concatenate(
        [jnp.zeros((1, GLA_GATE_RANK), F32), b_mlstm_i[0][None], b_mlstm_f[0][None]], axis=1))
    mix_w = (wz_hi, wz_lo, b_gla_gate_up[0][None], g_gla_out[0][None], w_mlstm_conv[0],
             b_mlstm_conv[0][None], bsm, g_mlstm_out[0][None])
    wr_hi, wr_lo = _split2(_pad_lanes(jnp.concatenate([w_router_group[0], w_router_expert[0]], axis=1)))

    p_big, p_small = _inproj(xp, xs, g_mix_norm[0][None], w_big, ws_hi, ws_lo)

    dt = x_prompt.dtype
    z_s = jnp.zeros((bp, N_HEADS, DK, DV), dt)
    z_n = jnp.zeros((bp, N_HEADS, DK), dt)
    z_m = jnp.zeros((bp, 1, N_HEADS), dt)
    z_cv = jnp.zeros((bp, CONV_W - 1, QK_CH), dt)
    o_p, p_S, p_C, p_n, p_m, p_cv = _mixers(p_big, p_small, 0, bp, t_p, 64, z_s, z_s, z_n, z_m, z_cv, mix_w)
    o_s, s_S, s_C, s_n, s_m, s_cv = _mixers(
        p_big, p_small, n_p, bs, t_s, t_s, state_gla_S[0], state_mlstm_C[0], state_mlstm_n[0],
        state_mlstm_m[0][:, None, :], cache_mlstm_conv[0], mix_w)

    x1, xn, comb = _outproj(o_p, o_s, xp, xs, w_out[0].astype(BF16), g_ffn_norm[0][None], wr_hi, wr_lo)
    y = _moe(xn, x1, comb, w_exp_gate[0].astype(BF16), w_exp_up[0].astype(BF16),
             w_exp_down[0].astype(BF16), g_final[None])

    y_prompt = y[:n_p].reshape(x_prompt.shape)
    y_sample = y[n_p:].reshape(x_sample.shape)
    return (y_prompt, y_sample,
            p_S[None], p_C[None], p_n[None], p_m[:, 0, :N_HEADS][None], p_cv[None],
            s_S[None], s_C[None], s_n[None], s_m[:, 0, :N_HEADS][None], s_cv[None])
```

```python
import functools

import jax
import jax.numpy as jnp
from jax import lax
from jax.experimental import pallas as pl
from jax.experimental.pallas import tpu as pltpu

F32 = jnp.float32
BF16 = jnp.bfloat16

D_MODEL = 2048
N_HEADS = 4
DK = 128
DV = 256
GLA_GATE_RANK = 16
GLA_GATE_NORM = 16.0
CONV_W = 4
QK_CH = 2 * N_HEADS * DK
N_GROUPS = 4
EXPERTS_PER_GROUP = 8
N_EXPERTS = N_GROUPS * EXPERTS_PER_GROUP
D_EXPERT = 256
EPS = 1e-6

LANES = 128
GLA_SUB = 16
ROW_TILE = 512
PROJ_BIG = 6 * 1024
LANE_I = GLA_GATE_RANK
LANE_F = GLA_GATE_RANK + N_HEADS
ROUTER_E0 = N_GROUPS
HALF = D_MODEL // 2
REC_ROWS = HALF // LANES
EXP_TILE = 256
DMA_UNROLL = 8

VMEM_LIMIT = 56 * 1024 * 1024


def _dot(a, b):
    return jnp.dot(a, b, preferred_element_type=F32)


def _dot_nt(a, b):
    return lax.dot_general(a, b, (((1,), (1,)), ((), ())), preferred_element_type=F32)


def _dot_tn(a, b):
    return lax.dot_general(a, b, (((0,), (0,)), ((), ())), preferred_element_type=F32)


def _split2(x):
    hi = x.astype(BF16)
    lo = (x - hi.astype(F32)).astype(BF16)
    return hi, lo


def _split3(x):
    hi = x.astype(BF16)
    r = x - hi.astype(F32)
    mid = r.astype(BF16)
    lo = (r - mid.astype(F32)).astype(BF16)
    return hi, mid, lo


def _dot_exact_lhs(m_bf16, x):
    hi, mid, lo = _split3(x)
    return _dot(m_bf16, hi) + _dot(m_bf16, mid) + _dot(m_bf16, lo)


def _dot_hilo(a_hi, a_lo, b_hi, b_lo):
    return _dot(a_hi, b_hi) + _dot(a_lo, b_hi) + _dot(a_hi, b_lo)


def _log_sigmoid(z):
    return jnp.minimum(z, 0.0) - jnp.log1p(jnp.exp(-jnp.abs(z)))


def _sigmoid(z):
    return 1.0 / (1.0 + jnp.exp(-z))


def _rmsnorm(x, g):
    return x * lax.rsqrt(jnp.mean(x * x, axis=-1, keepdims=True) + EPS) * g


def _inproj_kernel(n_prompt_tiles, xp_ref, xs_ref, g_ref, w_ref, ws_hi_ref, ws_lo_ref,
                   p_ref, ps_ref, h_s):
    i = pl.program_id(0)
    j = pl.program_id(1)

    def norm(x_ref):
        y = _rmsnorm(x_ref[...], g_ref[...])
        y_hi, y_lo = _split2(y)
        h_s[...] = y_hi
        ps_ref[...] = _dot_hilo(y_hi, y_lo, ws_hi_ref[...], ws_lo_ref[...])

    @pl.when((j == 0) & (i < n_prompt_tiles))
    def _():
        norm(xp_ref)

    @pl.when((j == 0) & (i >= n_prompt_tiles))
    def _():
        norm(xs_ref)

    p_ref[...] = _dot(h_s[...], w_ref[...])


def _inproj(xp, xs, g, w_big, ws_hi, ws_lo):
    n_p, n_s = xp.shape[0], xs.shape[0]
    assert n_p % ROW_TILE == 0 and n_s == ROW_TILE
    npt = n_p // ROW_TILE
    n = n_p + n_s
    tn = 1024
    grid = (npt + 1, PROJ_BIG // tn)
    return pl.pallas_call(
        functools.partial(_inproj_kernel, npt),
        grid=grid,
        in_specs=[
            pl.BlockSpec((ROW_TILE, D_MODEL), lambda i, j: (jnp.minimum(i, npt - 1), 0)),
            pl.BlockSpec((ROW_TILE, D_MODEL), lambda i, j: (0, 0)),
            pl.BlockSpec((1, D_MODEL), lambda i, j: (0, 0)),
            pl.BlockSpec((D_MODEL, tn), lambda i, j: (0, j)),
            pl.BlockSpec((D_MODEL, LANES), lambda i, j: (0, 0)),
            pl.BlockSpec((D_MODEL, LANES), lambda i, j: (0, 0)),
        ],
        out_specs=[
            pl.BlockSpec((ROW_TILE, tn), lambda i, j: (i, j)),
            pl.BlockSpec((ROW_TILE, LANES), lambda i, j: (i, 0)),
        ],
        out_shape=[
            jax.ShapeDtypeStruct((n, PROJ_BIG), F32),
            jax.ShapeDtypeStruct((n, LANES), F32),
        ],
        scratch_shapes=[pltpu.VMEM((ROW_TILE, D_MODEL), BF16)],
        compiler_params=pltpu.CompilerParams(
            dimension_semantics=("arbitrary", "arbitrary"), vmem_limit_bytes=VMEM_LIMIT),
        name="inproj",
    )(xp, xs, g, w_big, ws_hi, ws_lo)


def _mixer_kernel(rows, gq_ref, gk_ref, gv_ref, gg_ref, mqk_ref, mv_ref, mo_ref, sm_ref,
                  s0_ref, c0_ref, n0_ref, m0_ref, cv0_ref,
                  wz_hi_ref, wz_lo_ref, bz_ref, g_gla_ref, wc_ref, bc_ref, bsm_ref, g_ml_ref,
                  o_ref, s_out_ref, c_out_ref, n_out_ref, m_out_ref, cv_out_ref,
                  st_s, c_s, n_s, m_s, cb_s, b_s):
    blk = pl.program_id(1)
    n_blk = pl.num_programs(1)

    @pl.when(blk == 0)
    def _():
        for h in range(N_HEADS):
            st_s[h] = s0_ref[0, h].T
            c_s[h] = c0_ref[0, h]
            n_s[h] = n0_ref[0, h:h + 1, :]
            m_s[h] = jnp.broadcast_to(m0_ref[0, :, h:h + 1], (1, LANES))
        cb_s[0:8, :] = jnp.zeros((8, QK_CH), F32)
        cb_s[8 - (CONV_W - 1):8, :] = cv0_ref[0]

    small = sm_ref[...]
    row_i = lax.broadcasted_iota(jnp.int32, (rows, rows), 0)
    col_i = lax.broadcasted_iota(jnp.int32, (rows, rows), 1)
    causal = col_i <= row_i

    sm_hi, sm_lo = _split2(small)
    z = _dot_hilo(sm_hi, sm_lo, wz_hi_ref[...], wz_lo_ref[...]) + bz_ref[...]
    log_a = _log_sigmoid(z) * (1.0 / GLA_GATE_NORM)
    same_sub = (row_i // GLA_SUB) == (col_i // GLA_SUB)
    blk_tri = jnp.where(causal & same_sub, 1.0, 0.0).astype(BF16)
    b_s[...] = _dot_exact_lhs(blk_tri, log_a)

    sub_r = lax.broadcasted_iota(jnp.int32, (GLA_SUB, LANES), 0)
    sub_l = lax.broadcasted_iota(jnp.int32, (GLA_SUB, LANES), 1)

    def gla_sub(c, carry):
        r0 = pl.multiple_of(c * GLA_SUB, GLA_SUB)
        rs = pl.ds(r0, GLA_SUB)
        for h in range(N_HEADS):
            kc = slice(h * DK, (h + 1) * DK)
            vc = slice(h * DV, (h + 1) * DV)
            bh = b_s[rs, kc]
            qh = gq_ref[rs, kc] * (DK ** -0.5)
            kh = gk_ref[rs, kc]
            vh = gv_ref[rs, vc].astype(BF16)
            st = st_s[h]
            o = _dot_nt((qh * jnp.exp(bh)).astype(BF16), st.astype(BF16))
            a = jnp.zeros((GLA_SUB, LANES), F32)
            for s in range(GLA_SUB):
                e = jnp.exp(jnp.minimum(bh - bh[s:s + 1, :], 0.0))
                col = jnp.sum(qh * (kh[s:s + 1, :] * e), axis=1, keepdims=True)
                a = jnp.where((sub_l == s) & (sub_r >= s), col, a)
            o = o + _dot(a[:, :GLA_SUB].astype(BF16), vh)
            b_end = bh[GLA_SUB - 1:GLA_SUB, :]
            k_dec = kh * jnp.exp(b_end - bh)
            st_s[h] = st * jnp.exp(b_end) + _dot_tn(vh, k_dec.astype(BF16))
            y = o * lax.rsqrt(jnp.mean(o * o, axis=-1, keepdims=True) + EPS) * g_gla_ref[:, vc]
            gate = gg_ref[rs, vc]
            o_ref[rs, vc] = (y * (gate * _sigmoid(gate))).astype(o_ref.dtype)
        return carry

    lax.fori_loop(0, rows // GLA_SUB, gla_sub, 0)

    cb_s[8:8 + rows, :] = mqk_ref[...]
    conv = bc_ref[...]
    for j in range(CONV_W):
        conv = conv + cb_s[8 - (CONV_W - 1) + j:8 - (CONV_W - 1) + j + rows, :] * wc_ref[j:j + 1, :]
    cb_s[0:8, :] = cb_s[rows:rows + 8, :]
    qk = conv * _sigmoid(conv)

    pre = small + bsm_ref[...]
    log_f = _log_sigmoid(pre)
    tri = jnp.where(causal, 1.0, 0.0).astype(BF16)
    f_cum = _dot_exact_lhs(tri, log_f)
    eye = jnp.where(row_i == col_i, 1.0, 0.0).astype(BF16)
    f_cum_t = sum(_dot_tn(p, eye) for p in _split3(f_cum))
    pre_t = sum(_dot_tn(p, eye) for p in _split3(pre))
    lane_1 = lax.broadcasted_iota(jnp.int32, (1, LANES), 1)
    m_new = jnp.zeros((1, LANES), F32)

    for h in range(N_HEADS):
        kc = slice(h * DK, (h + 1) * DK)
        vc = slice(h * DV, (h + 1) * DV)
        f_col = f_cum[:, LANE_F + h:LANE_F + h + 1]
        i_col = pre[:, LANE_I + h:LANE_I + h + 1]
        f_row = f_cum_t[LANE_F + h:LANE_F + h + 1, :]
        i_row = pre_t[LANE_I + h:LANE_I + h + 1, :]
        dm = jnp.where(causal, f_col - f_row + i_row, -jnp.inf)
        m_prev = m_s[h][:, 0:1]
        inter = f_col + m_prev
        m_t = jnp.maximum(inter, jnp.max(dm, axis=1, keepdims=True))
        w = jnp.exp(dm - m_t)
        w_inter = jnp.exp(inter - m_t)
        q = qk[:, kc]
        k = qk[:, N_HEADS * DK + h * DK:N_HEADS * DK + (h + 1) * DK] * (DK ** -0.5)
        v = mv_ref[:, vc].astype(BF16)
        q_b = q.astype(BF16)
        c_prev = c_s[h]
        n_prev = n_s[h]
        s_qk = _dot_nt(q_b, k.astype(BF16)) * w
        num = _dot(s_qk.astype(BF16), v) + w_inter * _dot(q_b, c_prev.astype(BF16))
        den = (jnp.sum(s_qk, axis=1, keepdims=True)
               + w_inter * jnp.sum(q * n_prev, axis=1, keepdims=True))
        hh = num / jnp.maximum(jnp.abs(den), jnp.exp(-m_t))
        m_end = m_t[rows - 1:rows, :]
        w_s = jnp.exp(f_col[rows - 1:rows, :] - f_col + i_col - m_end)
        dec = w_inter[rows - 1:rows, :]
        k_w = k * w_s
        c_s[h] = dec * c_prev + _dot_tn(k_w.astype(BF16), v)
        n_s[h] = dec * n_prev + jnp.sum(k_w, axis=0, keepdims=True)
        m_s[h] = jnp.broadcast_to(m_end, (1, LANES))
        m_new = jnp.where(lane_1 == h, m_end, m_new)
        y = hh * lax.rsqrt(jnp.mean(hh * hh, axis=-1, keepdims=True) + EPS) * g_ml_ref[:, vc]
        o_ref[:, N_HEADS * DV + h * DV:N_HEADS * DV + (h + 1) * DV] = (
            y * _sigmoid(mo_ref[:, vc])).astype(o_ref.dtype)

    @pl.when(blk == n_blk - 1)
    def _():
        for h in range(N_HEADS):
            s_out_ref[0, h] = st_s[h].T
            c_out_ref[0, h] = c_s[h]
            n_out_ref[0, h:h + 1, :] = n_s[h]
        m_out_ref[0] = m_new
        cv_out_ref[0] = cb_s[8 - (CONV_W - 1):8, :]


def _mixers(p_big, p_small, row0, n_streams, t_len, rows, s0, c0, n0, m0, cv0, wts):
    assert t_len % rows == 0 and row0 % rows == 0 and rows % GLA_SUB == 0
    n_blk = t_len // rows
    b0 = row0 // rows

    def rmap(col):
        return lambda s, b: (b0 + s * n_blk + b, col)

    def smap(*zeros):
        return lambda s, b: (s,) + zeros

    def wmap(s, b):
        return (0, 0)

    wz_hi, wz_lo, bz, g_gla, wc, bc, bsm, g_ml = wts
    in_specs = [
        pl.BlockSpec((rows, N_HEADS * DK), rmap(0)),
        pl.BlockSpec((rows, N_HEADS * DK), rmap(1)),
        pl.BlockSpec((rows, N_HEADS * DV), rmap(1)),
        pl.BlockSpec((rows, N_HEADS * DV), rmap(2)),
        pl.BlockSpec((rows, QK_CH), rmap(3)),
        pl.BlockSpec((rows, N_HEADS * DV), rmap(4)),
        pl.BlockSpec((rows, N_HEADS * DV), rmap(5)),
        pl.BlockSpec((rows, LANES), rmap(0)),
        pl.BlockSpec((1, N_HEADS, DK, DV), smap(0, 0, 0)),
        pl.BlockSpec((1, N_HEADS, DK, DV), smap(0, 0, 0)),
        pl.BlockSpec((1, N_HEADS, DK), smap(0, 0)),
        pl.BlockSpec((1, 1, N_HEADS), smap(0, 0)),
        pl.BlockSpec((1, CONV_W - 1, QK_CH), smap(0, 0)),
        pl.BlockSpec(wz_hi.shape, wmap), pl.BlockSpec(wz_lo.shape, wmap),
        pl.BlockSpec(bz.shape, wmap), pl.BlockSpec(g_gla.shape, wmap),
        pl.BlockSpec(wc.shape, wmap), pl.BlockSpec(bc.shape, wmap),
        pl.BlockSpec(bsm.shape, wmap), pl.BlockSpec(g_ml.shape, wmap),
    ]
    n_rows = n_streams * t_len
    out_shape = [
        jax.ShapeDtypeStruct((n_rows, D_MODEL), BF16),
        jax.ShapeDtypeStruct((n_streams, N_HEADS, DK, DV), F32),
        jax.ShapeDtypeStruct((n_streams, N_HEADS, DK, DV), F32),
        jax.ShapeDtypeStruct((n_streams, N_HEADS, DK), F32),
        jax.ShapeDtypeStruct((n_streams, 1, LANES), F32),
        jax.ShapeDtypeStruct((n_streams, CONV_W - 1, QK_CH), F32),
    ]
    out_specs = [
        pl.BlockSpec((rows, D_MODEL), lambda s, b: (s * n_blk + b, 0)),
        pl.BlockSpec((1, N_HEADS, DK, DV), smap(0, 0, 0)),
        pl.BlockSpec((1, N_HEADS, DK, DV), smap(0, 0, 0)),
        pl.BlockSpec((1, N_HEADS, DK), smap(0, 0)),
        pl.BlockSpec((1, 1, LANES), smap(0, 0)),
        pl.BlockSpec((1, CONV_W - 1, QK_CH), smap(0, 0)),
    ]
    scratch = [
        pltpu.VMEM((N_HEADS, DV, DK), F32),
        pltpu.VMEM((N_HEADS, DK, DV), F32),
        pltpu.VMEM((N_HEADS, 1, DK), F32),
        pltpu.VMEM((N_HEADS, 1, LANES), F32),
        pltpu.VMEM((rows + 8, QK_CH), F32),
        pltpu.VMEM((rows, N_HEADS * DK), F32),
    ]
    return pl.pallas_call(
        functools.partial(_mixer_kernel, rows),
        grid=(n_streams, n_blk),
        in_specs=in_specs, out_specs=out_specs, out_shape=out_shape,
        scratch_shapes=scratch,
        compiler_params=pltpu.CompilerParams(
            dimension_semantics=("arbitrary", "arbitrary"), vmem_limit_bytes=VMEM_LIMIT),
        name=f"mixers_r{rows}",
    )(p_big, p_big, p_big, p_big, p_big, p_big, p_big, p_small, s0, c0, n0, m0, cv0, *wts)


def _pack_rows(ref, val):
    word = pltpu.pack_elementwise([val[:, :HALF], val[:, HALF:]], packed_dtype=BF16)
    rows = val.shape[0]
    for j in range(REC_ROWS):
        ref[pl.ds(j, rows, stride=REC_ROWS), :] = word[:, j * LANES:(j + 1) * LANES]


def _unpack_rows(ref, rows):
    word = jnp.concatenate(
        [ref[pl.ds(j, rows, stride=REC_ROWS), :] for j in range(REC_ROWS)], axis=1)
    lo = pltpu.unpack_elementwise(word, index=0, packed_dtype=BF16, unpacked_dtype=F32)
    hi = pltpu.unpack_elementwise(word, index=1, packed_dtype=BF16, unpacked_dtype=F32)
    return lo, hi


def _outproj_kernel(n_prompt_tiles, op_ref, os_ref, xp_ref, xs_ref, w_ref, g_ref, wr_ref,
                    x1_ref, xn_ref, route_ref, cnt_ref, cnt_s):
    i = pl.program_id(0)

    @pl.when(i == 0)
    def _():
        cnt_s[...] = jnp.zeros_like(cnt_s)

    def body(o_ref, x_ref):
        x1 = x_ref[...] + _dot(o_ref[...], w_ref[...])
        x1_ref[...] = x1
        xn = _rmsnorm(x1, g_ref[...])
        _pack_rows(xn_ref, xn)
        xn_hi, xn_mid, xn_lo = _split3(xn)
        pa = _dot(xn_hi, wr_ref[...])
        pb = _dot(xn_mid, wr_ref[:, :2 * LANES])
        pc = _dot(xn_lo, wr_ref[:, :LANES])
        lg = (((pa[:, 2 * LANES:] + pc + pb[:, LANES:]) + (pa[:, LANES:2 * LANES] + pb[:, :LANES]))
              + pa[:, :LANES])
        lane = lax.broadcasted_iota(jnp.int32, lg.shape, 1).astype(F32)
        neg = -jnp.inf
        lgm = jnp.where(lane < N_GROUPS, lg, neg)
        mg = jnp.max(lgm, axis=1, keepdims=True)
        g_idx = jnp.min(jnp.where(lgm == mg, lane, float(LANES)), axis=1, keepdims=True)
        g_val = 1.0 / jnp.sum(jnp.where(lane < N_GROUPS, jnp.exp(lg - mg), 0.0), axis=1, keepdims=True)
        e0 = ROUTER_E0 + EXPERTS_PER_GROUP * g_idx
        le = jnp.where((lane >= e0) & (lane < e0 + EXPERTS_PER_GROUP), lg, neg)
        v1 = jnp.max(le, axis=1, keepdims=True)
        i1 = jnp.min(jnp.where(le == v1, lane, float(LANES)), axis=1, keepdims=True)
        le2 = jnp.where(lane == i1, neg, le)
        v2 = jnp.max(le2, axis=1, keepdims=True)
        i2 = jnp.min(jnp.where(le2 == v2, lane, float(LANES)), axis=1, keepdims=True)
        t = jnp.exp(v2 - v1)
        w1 = g_val / (1.0 + t)
        w2 = g_val * t / (1.0 + t)
        oh1 = lane == i1
        oh2 = lane == i2
        hot = jnp.where(oh1 | oh2, 1.0, 0.0)
        r_i = lax.broadcasted_iota(jnp.int32, (ROW_TILE, ROW_TILE), 0)
        c_i = lax.broadcasted_iota(jnp.int32, (ROW_TILE, ROW_TILE), 1)
        before = jnp.where(c_i < r_i, 1.0, 0.0).astype(BF16)
        seen = _dot(before, hot.astype(BF16)) + cnt_s[...]
        rank1 = jnp.sum(jnp.where(oh1, seen, 0.0), axis=1, keepdims=True)
        rank2 = jnp.sum(jnp.where(oh2, seen, 0.0), axis=1, keepdims=True)
        cnt_s[...] += jnp.sum(hot, axis=0, keepdims=True)
        route = jnp.zeros_like(lg)
        for k, col in enumerate((i1 - ROUTER_E0, i2 - ROUTER_E0, rank1, rank2, w1, w2)):
            route = jnp.where(lane == k, col, route)
        route_ref[...] = route

    @pl.when(i < n_prompt_tiles)
    def _():
        body(op_ref, xp_ref)

    @pl.when(i >= n_prompt_tiles)
    def _():
        body(os_ref, xs_ref)

    cnt_ref[...] = cnt_s[...]


def _outproj(o_p, o_s, xp, xs, w_out, g, wr):
    n_p, n_s = xp.shape[0], xs.shape[0]
    npt = n_p // ROW_TILE
    n = n_p + n_s

    def pmap(i):
        return (jnp.minimum(i, npt - 1), 0)

    def cmap(i):
        return (0, 0)

    def omap(i):
        return (i, 0)

    return pl.pallas_call(
        functools.partial(_outproj_kernel, npt),
        grid=(npt + 1,),
        in_specs=[
            pl.BlockSpec((ROW_TILE, D_MODEL), pmap), pl.BlockSpec((ROW_TILE, D_MODEL), cmap),
            pl.BlockSpec((ROW_TILE, D_MODEL), pmap), pl.BlockSpec((ROW_TILE, D_MODEL), cmap),
            pl.BlockSpec((D_MODEL, D_MODEL), cmap), pl.BlockSpec((1, D_MODEL), cmap),
            pl.BlockSpec((D_MODEL, 3 * LANES), cmap),
        ],
        out_specs=[
            pl.BlockSpec((ROW_TILE, D_MODEL), omap), pl.BlockSpec((ROW_TILE * REC_ROWS, LANES), omap),
            pl.BlockSpec((ROW_TILE, LANES), omap), pl.BlockSpec((1, LANES), cmap),
        ],
        out_shape=[
            jax.ShapeDtypeStruct((n, D_MODEL), F32),
            jax.ShapeDtypeStruct((n * REC_ROWS, LANES), jnp.uint32),
            jax.ShapeDtypeStruct((n, LANES), F32),
            jax.ShapeDtypeStruct((1, LANES), F32),
        ],
        scratch_shapes=[pltpu.VMEM((1, LANES), F32)],
        compiler_params=pltpu.CompilerParams(
            dimension_semantics=("arbitrary",), vmem_limit_bytes=VMEM_LIMIT),
        name="outproj_router",
    )(o_p, o_s, xp, xs, w_out, g, wr)


def _rec(ref, idx):
    return ref.at[pl.ds(pl.multiple_of(idx * REC_ROWS, REC_ROWS), REC_ROWS)]


def _zero_records_kernel(o_ref):
    z = jnp.zeros(o_ref.shape, F32)
    o_ref[...] = pltpu.pack_elementwise([z, z], packed_dtype=BF16)


def _zero_records(n_rec):
    blk = EXP_TILE * REC_ROWS
    return pl.pallas_call(
        _zero_records_kernel,
        grid=(n_rec // EXP_TILE,),
        out_specs=pl.BlockSpec((blk, LANES), lambda i: (i, 0)),
        out_shape=jax.ShapeDtypeStruct((n_rec * REC_ROWS, LANES), jnp.uint32),
        compiler_params=pltpu.CompilerParams(dimension_semantics=("arbitrary",)),
        name="moe_zero_slots",
    )()


def _dispatch_kernel(pos1_ref, pos2_ref, xn_ref, init_ref, xs_ref, sem):
    del init_ref
    base = pl.program_id(0) * ROW_TILE

    def copies(r):
        src = _rec(xn_ref, r)
        return (pltpu.make_async_copy(src, _rec(xs_ref, pos1_ref[base + r]), sem),
                pltpu.make_async_copy(src, _rec(xs_ref, pos2_ref[base + r]), sem))

    def start(g, c):
        for u in range(DMA_UNROLL):
            for cp in copies(g * DMA_UNROLL + u):
                cp.start()
        return c

    def wait(g, c):
        for u in range(DMA_UNROLL):
            for cp in copies(g * DMA_UNROLL + u):
                cp.wait()
        return c

    lax.fori_loop(0, ROW_TILE // DMA_UNROLL, start, 0)
    lax.fori_loop(0, ROW_TILE // DMA_UNROLL, wait, 0)


def _dispatch(pos1, pos2, xn_rec, xs_init):
    n = pos1.shape[0]
    return pl.pallas_call(
        _dispatch_kernel,
        grid_spec=pltpu.PrefetchScalarGridSpec(
            num_scalar_prefetch=2,
            grid=(n // ROW_TILE,),
            in_specs=[pl.BlockSpec((ROW_TILE * REC_ROWS, LANES), lambda i, p1, p2: (i, 0)),
                      pl.BlockSpec(memory_space=pl.ANY)],
            out_specs=pl.BlockSpec(memory_space=pl.ANY),
            scratch_shapes=[pltpu.SemaphoreType.DMA],
        ),
        out_shape=jax.ShapeDtypeStruct(xs_init.shape, xs_init.dtype),
        input_output_aliases={3: 0},
        compiler_params=pltpu.CompilerParams(
            dimension_semantics=("arbitrary",), vmem_limit_bytes=VMEM_LIMIT),
        name="moe_dispatch",
    )(pos1, pos2, xn_rec, xs_init)


def _experts_kernel(te_ref, nt_ref, xs_ref, wg_ref, wu_ref, wd_ref, ys_ref, wg_s, wu_s, wd_s):
    t = pl.program_id(0)
    prev = te_ref[jnp.maximum(t - 1, 0)]

    @pl.when((t == 0) | (te_ref[t] != prev))
    def _():
        wg_s[...] = wg_ref[0].astype(BF16)
        wu_s[...] = wu_ref[0].astype(BF16)
        wd_s[...] = wd_ref[0].astype(BF16)

    @pl.when(t < nt_ref[0])
    def _():
        lo, hi = _unpack_rows(xs_ref, EXP_TILE)
        x = jnp.concatenate([lo.astype(BF16), hi.astype(BF16)], axis=1)
        hg = _dot(x, wg_s[...])
        hu = _dot(x, wu_s[...])
        act = hg * _sigmoid(hg) * hu
        _pack_rows(ys_ref, _dot(act.astype(BF16), wd_s[...]))

    @pl.when(t >= nt_ref[0])
    def _():
        _zero_records_kernel(ys_ref)


def _experts(tile_expert, n_tiles, xs, wg, wu, wd):
    t_max = tile_expert.shape[0]

    def tmap(t, te, nt):
        return (jnp.minimum(t, nt[0] - 1), 0)

    def wmap(t, te, nt):
        return (te[t], 0, 0)

    return pl.pallas_call(
        _experts_kernel,
        grid_spec=pltpu.PrefetchScalarGridSpec(
            num_scalar_prefetch=2,
            grid=(t_max,),
            in_specs=[pl.BlockSpec((EXP_TILE * REC_ROWS, LANES), tmap),
                      pl.BlockSpec((1, D_MODEL, D_EXPERT), wmap),
                      pl.BlockSpec((1, D_MODEL, D_EXPERT), wmap),
                      pl.BlockSpec((1, D_EXPERT, D_MODEL), wmap)],
            out_specs=pl.BlockSpec((EXP_TILE * REC_ROWS, LANES), lambda t, te, nt: (t, 0)),
            scratch_shapes=[pltpu.VMEM((D_MODEL, D_EXPERT), BF16), pltpu.VMEM((D_MODEL, D_EXPERT), BF16),
                            pltpu.VMEM((D_EXPERT, D_MODEL), BF16)],
        ),
        out_shape=jax.ShapeDtypeStruct(xs.shape, xs.dtype),
        compiler_params=pltpu.CompilerParams(
            dimension_semantics=("arbitrary",), vmem_limit_bytes=VMEM_LIMIT),
        name="moe_experts",
    )(tile_expert, n_tiles, xs, wg, wu, wd)


def _combine_kernel(n_prompt_tiles, pos1_ref, pos2_ref, x1_ref, route_ref, g_ref, ys_ref,
                    yp_ref, ysm_ref, a_s, b_s, sem):
    i = pl.program_id(0)
    n_steps = pl.num_programs(0)

    def copies(step, slot, r):
        tok = step * ROW_TILE + r
        return (pltpu.make_async_copy(_rec(ys_ref, pos1_ref[tok]), _rec(a_s.at[slot], r), sem.at[slot]),
                pltpu.make_async_copy(_rec(ys_ref, pos2_ref[tok]), _rec(b_s.at[slot], r), sem.at[slot]))

    def start_all(step, slot):
        def start(g, c):
            for u in range(DMA_UNROLL):
                for cp in copies(step, slot, g * DMA_UNROLL + u):
                    cp.start()
            return c
        lax.fori_loop(0, ROW_TILE // DMA_UNROLL, start, 0)

    def wait_all(step, slot):
        def wait(g, c):
            for u in range(DMA_UNROLL):
                for cp in copies(step, slot, g * DMA_UNROLL + u):
                    cp.wait()
            return c
        lax.fori_loop(0, ROW_TILE // DMA_UNROLL, wait, 0)

    slot = i % 2

    @pl.when(i == 0)
    def _():
        start_all(0, 0)

    @pl.when(i + 1 < n_steps)
    def _():
        start_all(i + 1, 1 - slot)

    wait_all(i, slot)
    a_lo, a_hi = _unpack_rows(a_s.at[slot], ROW_TILE)
    b_lo, b_hi = _unpack_rows(b_s.at[slot], ROW_TILE)
    route = route_ref[...]
    w1 = route[:, 4:5]
    w2 = route[:, 5:6]
    moe = jnp.concatenate([w1 * a_lo + w2 * b_lo, w1 * a_hi + w2 * b_hi], axis=1)
    y = _rmsnorm(x1_ref[...] + moe, g_ref[...])

    @pl.when(i < n_prompt_tiles)
    def _():
        yp_ref[...] = y

    @pl.when(i >= n_prompt_tiles)
    def _():
        ysm_ref[...] = y


def _combine(pos1, pos2, x1, route, g, ys, n_p):
    n = x1.shape[0]
    npt = n_p // ROW_TILE

    def omap(i, p1, p2):
        return (i, 0)

    return pl.pallas_call(
        functools.partial(_combine_kernel, npt),
        grid_spec=pltpu.PrefetchScalarGridSpec(
            num_scalar_prefetch=2,
            grid=(n // ROW_TILE,),
            in_specs=[pl.BlockSpec((ROW_TILE, D_MODEL), omap),
                      pl.BlockSpec((ROW_TILE, LANES), omap),
                      pl.BlockSpec((1, D_MODEL), lambda i, p1, p2: (0, 0)),
                      pl.BlockSpec(memory_space=pl.ANY)],
            out_specs=[pl.BlockSpec((ROW_TILE, D_MODEL), lambda i, p1, p2: (jnp.minimum(i, npt - 1), 0)),
                       pl.BlockSpec((ROW_TILE, D_MODEL), lambda i, p1, p2: (jnp.maximum(i - npt, 0), 0))],
            scratch_shapes=[pltpu.VMEM((2, ROW_TILE * REC_ROWS, LANES), jnp.uint32),
                            pltpu.VMEM((2, ROW_TILE * REC_ROWS, LANES), jnp.uint32),
                            pltpu.SemaphoreType.DMA((2,))],
        ),
        out_shape=[jax.ShapeDtypeStruct((n_p, D_MODEL), F32),
                   jax.ShapeDtypeStruct((n - n_p, D_MODEL), F32)],
        compiler_params=pltpu.CompilerParams(
            dimension_semantics=("arbitrary",), vmem_limit_bytes=VMEM_LIMIT),
        name="moe_combine",
    )(pos1, pos2, x1, route, g, ys)


def _pad_lanes(w):
    return jnp.pad(w, ((0, 0), (0, LANES - w.shape[1])))


def _moe_plan(route, counts):
    cnt = counts[0, ROUTER_E0:ROUTER_E0 + N_EXPERTS].astype(jnp.int32)
    tiles = (cnt + EXP_TILE - 1) // EXP_TILE
    tile_end = jnp.cumsum(tiles)
    row0 = (tile_end - tiles) * EXP_TILE
    e1 = route[:, 0].astype(jnp.int32)
    e2 = route[:, 1].astype(jnp.int32)
    pos1 = row0[e1] + route[:, 2].astype(jnp.int32)
    pos2 = row0[e2] + route[:, 3].astype(jnp.int32)
    n_slots = 2 * route.shape[0]
    t_max = n_slots // EXP_TILE + N_EXPERTS
    tile_ids = jnp.arange(t_max, dtype=jnp.int32)
    tile_expert = jnp.minimum(
        jnp.sum((tile_ids[:, None] >= tile_end[None, :]).astype(jnp.int32), axis=1), N_EXPERTS - 1)
    return pos1, pos2, tile_expert, tile_end[-1:].astype(jnp.int32), t_max


def kernel(x_prompt, x_sample, state_gla_S, state_mlstm_C, state_mlstm_n, state_mlstm_m, cache_mlstm_conv, g_mix_norm, w_in, w_gla_gate_up, b_gla_gate_up, g_gla_out, w_mlstm_conv, b_mlstm_conv, b_mlstm_i, b_mlstm_f, g_mlstm_out, w_out, g_ffn_norm, w_router_group, w_router_expert, w_exp_gate, w_exp_up, w_exp_down, g_final):
    depth = w_in.shape[0]
    assert depth == 1
    bp, t_p, _ = x_prompt.shape
    bs, t_s, _ = x_sample.shape
    assert bp == 1
    xp = x_prompt.reshape(bp * t_p, D_MODEL)
    xs = x_sample.reshape(bs * t_s, D_MODEL)
    n_p = xp.shape[0]

    w = w_in[0]
    c_gz = 3072
    c_mqk = c_gz + GLA_GATE_RANK
    c_mi = c_mqk + 3072
    w_big = jnp.concatenate([w[:, :c_gz], w[:, c_mqk:c_mi]], axis=1).astype(BF16)
    ws_hi, ws_lo = _split2(_pad_lanes(jnp.concatenate([w[:, c_gz:c_mqk], w[:, c_mi:]], axis=1)))
    wz = jnp.pad(w_gla_gate_up[0], ((0, LANES - GLA_GATE_RANK), (0, 0)))
    wz_hi, wz_lo = _split2(wz)
    bsm = _pad_lanes(jnp.concatenate(
        [jnp.zeros((1, GLA_GATE_RANK), F32), b_mlstm_i[0][None], b_mlstm_f[0][None]], axis=1))
    mix_w = (wz_hi, wz_lo, b_gla_gate_up[0][None], g_gla_out[0][None], w_mlstm_conv[0],
             b_mlstm_conv[0][None], bsm, g_mlstm_out[0][None])
    wr = jnp.concatenate(
        _split3(_pad_lanes(jnp.concatenate([w_router_group[0], w_router_expert[0]], axis=1))), axis=1)

    p_big, p_small = _inproj(xp, xs, g_mix_norm[0][None], w_big, ws_hi, ws_lo)

    dt = x_prompt.dtype
    z_s = jnp.zeros((bp, N_HEADS, DK, DV), dt)
    z_n = jnp.zeros((bp, N_HEADS, DK), dt)
    z_m = jnp.zeros((bp, 1, N_HEADS), dt)
    z_cv = jnp.zeros((bp, CONV_W - 1, QK_CH), dt)
    o_p, p_S, p_C, p_n, p_m, p_cv = _mixers(p_big, p_small, 0, bp, t_p, 64, z_s, z_s, z_n, z_m, z_cv, mix_w)
    o_s, s_S, s_C, s_n, s_m, s_cv = _mixers(
        p_big, p_small, n_p, bs, t_s, t_s, state_gla_S[0], state_mlstm_C[0], state_mlstm_n[0],
        state_mlstm_m[0][:, None, :], cache_mlstm_conv[0], mix_w)

    x1, xn_rec, route, counts = _outproj(
        o_p, o_s, xp, xs, w_out[0].astype(BF16), g_ffn_norm[0][None], wr)
    pos1, pos2, tile_expert, n_tiles, t_max = _moe_plan(route, counts)
    xs_rec = _dispatch(pos1, pos2, xn_rec, _zero_records(t_max * EXP_TILE))
    ys_rec = _experts(tile_expert, n_tiles, xs_rec, w_exp_gate[0], w_exp_up[0], w_exp_down[0])
    y_p, y_s = _combine(pos1, pos2, x1, route, g_final[None], ys_rec, n_p)

    return (y_p.reshape(x_prompt.shape), y_s.reshape(x_sample.shape),
            p_S[None], p_C[None], p_n[None], p_m[:, 0, :N_HEADS][None], p_cv[None],
            s_S[None], s_C[None], s_n[None], s_m[:, 0, :N_HEADS][None], s_cv[None])
```

```python
import functools

import jax
import jax.numpy as jnp
from jax import lax
from jax.experimental import pallas as pl
from jax.experimental.pallas import tpu as pltpu

F32 = jnp.float32
BF16 = jnp.bfloat16

D_MODEL = 2048
N_HEADS = 4
DK = 128
DV = 256
GLA_GATE_RANK = 16
GLA_GATE_NORM = 16.0
CONV_W = 4
QK_CH = 2 * N_HEADS * DK
N_GROUPS = 4
EXPERTS_PER_GROUP = 8
N_EXPERTS = N_GROUPS * EXPERTS_PER_GROUP
D_EXPERT = 256
EPS = 1e-6

LANES = 128
GLA_SUB = 16
ROW_TILE = 512
PROJ_BIG = 6 * 1024
LANE_I = GLA_GATE_RANK
LANE_F = GLA_GATE_RANK + N_HEADS
ROUTER_E0 = N_GROUPS
HALF = D_MODEL // 2
REC_ROWS = HALF // LANES
EXP_TILE = 256
DMA_UNROLL = 8
SLOT_CODE_BITS = 16
SLOT_CODE = 1 << SLOT_CODE_BITS

VMEM_LIMIT = 56 * 1024 * 1024


def _dot(a, b):
    return jnp.dot(a, b, preferred_element_type=F32)


def _dot_nt(a, b):
    return lax.dot_general(a, b, (((1,), (1,)), ((), ())), preferred_element_type=F32)


def _dot_tn(a, b):
    return lax.dot_general(a, b, (((0,), (0,)), ((), ())), preferred_element_type=F32)


def _split2(x):
    hi = x.astype(BF16)
    lo = (x - hi.astype(F32)).astype(BF16)
    return hi, lo


def _split3(x):
    hi = x.astype(BF16)
    r = x - hi.astype(F32)
    mid = r.astype(BF16)
    lo = (r - mid.astype(F32)).astype(BF16)
    return hi, mid, lo


def _dot_exact_lhs(m_bf16, x):
    hi, mid, lo = _split3(x)
    return _dot(m_bf16, hi) + _dot(m_bf16, mid) + _dot(m_bf16, lo)


def _dot_hilo(a_hi, a_lo, b_hi, b_lo):
    return _dot(a_hi, b_hi) + _dot(a_lo, b_hi) + _dot(a_hi, b_lo)


def _log_sigmoid(z):
    return jnp.minimum(z, 0.0) - jnp.log1p(jnp.exp(-jnp.abs(z)))


def _sigmoid(z):
    return 1.0 / (1.0 + jnp.exp(-z))


def _rmsnorm(x, g):
    return x * lax.rsqrt(jnp.mean(x * x, axis=-1, keepdims=True) + EPS) * g


def _inproj_kernel(n_prompt_tiles, xp_ref, xs_ref, g_ref, w_ref, ws_hi_ref, ws_lo_ref,
                   p_ref, ps_ref, h_s):
    i = pl.program_id(0)
    j = pl.program_id(1)

    def norm(x_ref):
        y = _rmsnorm(x_ref[...], g_ref[...])
        y_hi, y_lo = _split2(y)
        h_s[...] = y_hi
        ps_ref[...] = _dot_hilo(y_hi, y_lo, ws_hi_ref[...], ws_lo_ref[...])

    @pl.when((j == 0) & (i < n_prompt_tiles))
    def _():
        norm(xp_ref)

    @pl.when((j == 0) & (i >= n_prompt_tiles))
    def _():
        norm(xs_ref)

    tn = p_ref.shape[1]
    p_ref[...] = _dot(h_s[...], w_ref[:, pl.ds(pl.multiple_of(j * tn, tn), tn)])


def _inproj(xp, xs, g, w_big, ws_hi, ws_lo):
    n_p, n_s = xp.shape[0], xs.shape[0]
    assert n_p % ROW_TILE == 0 and n_s == ROW_TILE
    npt = n_p // ROW_TILE
    n = n_p + n_s
    tn = 1024
    grid = (npt + 1, PROJ_BIG // tn)
    once = pl.Buffered(1)
    return pl.pallas_call(
        functools.partial(_inproj_kernel, npt),
        grid=grid,
        in_specs=[
            pl.BlockSpec((ROW_TILE, D_MODEL), lambda i, j: (jnp.minimum(i, npt - 1), 0)),
            pl.BlockSpec((ROW_TILE, D_MODEL), lambda i, j: (0, 0)),
            pl.BlockSpec((1, D_MODEL), lambda i, j: (0, 0)),
            pl.BlockSpec((D_MODEL, PROJ_BIG), lambda i, j: (0, 0), pipeline_mode=once),
            pl.BlockSpec((D_MODEL, LANES), lambda i, j: (0, 0), pipeline_mode=once),
            pl.BlockSpec((D_MODEL, LANES), lambda i, j: (0, 0), pipeline_mode=once),
        ],
        out_specs=[
            pl.BlockSpec((ROW_TILE, tn), lambda i, j: (i, j)),
            pl.BlockSpec((ROW_TILE, LANES), lambda i, j: (i, 0)),
        ],
        out_shape=[
            jax.ShapeDtypeStruct((n, PROJ_BIG), F32),
            jax.ShapeDtypeStruct((n, LANES), F32),
        ],
        scratch_shapes=[pltpu.VMEM((ROW_TILE, D_MODEL), BF16)],
        compiler_params=pltpu.CompilerParams(
            dimension_semantics=("arbitrary", "arbitrary"), vmem_limit_bytes=VMEM_LIMIT),
        name="inproj",
    )(xp, xs, g, w_big, ws_hi, ws_lo)


def _mixer_kernel(rows, gq_ref, gk_ref, gv_ref, gg_ref, mqk_ref, mv_ref, mo_ref, sm_ref,
                  s0_ref, c0_ref, n0_ref, m0_ref, cv0_ref,
                  wz_hi_ref, wz_lo_ref, bz_ref, g_gla_ref, wc_ref, bc_ref, bsm_ref, g_ml_ref,
                  o_ref, s_out_ref, c_out_ref, n_out_ref, m_out_ref, cv_out_ref,
                  st_s, c_s, n_s, m_s, cb_s, b_s):
    blk = pl.program_id(1)
    n_blk = pl.num_programs(1)

    @pl.when(blk == 0)
    def _():
        for h in range(N_HEADS):
            st_s[h] = s0_ref[0, h].T
            c_s[h] = c0_ref[0, h]
            n_s[h] = n0_ref[0, h:h + 1, :]
            m_s[h] = jnp.broadcast_to(m0_ref[0, :, h:h + 1], (1, LANES))
        cb_s[0:8, :] = jnp.zeros((8, QK_CH), F32)
        cb_s[8 - (CONV_W - 1):8, :] = cv0_ref[0]

    small = sm_ref[...]
    row_i = lax.broadcasted_iota(jnp.int32, (rows, rows), 0)
    col_i = lax.broadcasted_iota(jnp.int32, (rows, rows), 1)
    causal = col_i <= row_i

    sm_hi, sm_lo = _split2(small)
    z = _dot_hilo(sm_hi, sm_lo, wz_hi_ref[...], wz_lo_ref[...]) + bz_ref[...]
    log_a = _log_sigmoid(z) * (1.0 / GLA_GATE_NORM)
    same_sub = (row_i // GLA_SUB) == (col_i // GLA_SUB)
    blk_tri = jnp.where(causal & same_sub, 1.0, 0.0).astype(BF16)
    b_s[...] = _dot_exact_lhs(blk_tri, log_a)

    sub_r = lax.broadcasted_iota(jnp.int32, (GLA_SUB, LANES), 0)
    sub_l = lax.broadcasted_iota(jnp.int32, (GLA_SUB, LANES), 1)

    def gla_sub(c, carry):
        r0 = pl.multiple_of(c * GLA_SUB, GLA_SUB)
        rs = pl.ds(r0, GLA_SUB)
        for h in range(N_HEADS):
            kc = slice(h * DK, (h + 1) * DK)
            vc = slice(h * DV, (h + 1) * DV)
            bh = b_s[rs, kc]
            qh = gq_ref[rs, kc] * (DK ** -0.5)
            kh = gk_ref[rs, kc]
            vh = gv_ref[rs, vc].astype(BF16)
            st = st_s[h]
            o = _dot_nt((qh * jnp.exp(bh)).astype(BF16), st.astype(BF16))
            a = jnp.zeros((GLA_SUB, LANES), F32)
            for s in range(GLA_SUB):
                e = jnp.exp(jnp.minimum(bh - bh[s:s + 1, :], 0.0))
                col = jnp.sum(qh * (kh[s:s + 1, :] * e), axis=1, keepdims=True)
                a = jnp.where((sub_l == s) & (sub_r >= s), col, a)
            o = o + _dot(a[:, :GLA_SUB].astype(BF16), vh)
            b_end = bh[GLA_SUB - 1:GLA_SUB, :]
            k_dec = kh * jnp.exp(b_end - bh)
            st_s[h] = st * jnp.exp(b_end) + _dot_tn(vh, k_dec.astype(BF16))
            y = o * lax.rsqrt(jnp.mean(o * o, axis=-1, keepdims=True) + EPS) * g_gla_ref[:, vc]
            gate = gg_ref[rs, vc]
            o_ref[rs, vc] = (y * (gate * _sigmoid(gate))).astype(o_ref.dtype)
        return carry

    lax.fori_loop(0, rows // GLA_SUB, gla_sub, 0)

    cb_s[8:8 + rows, :] = mqk_ref[...]
    conv = bc_ref[...]
    for j in range(CONV_W):
        conv = conv + cb_s[8 - (CONV_W - 1) + j:8 - (CONV_W - 1) + j + rows, :] * wc_ref[j:j + 1, :]
    cb_s[0:8, :] = cb_s[rows:rows + 8, :]
    qk = conv * _sigmoid(conv)

    pre = small + bsm_ref[...]
    log_f = _log_sigmoid(pre)
    tri = jnp.where(causal, 1.0, 0.0).astype(BF16)
    f_cum = _dot_exact_lhs(tri, log_f)
    eye = jnp.where(row_i == col_i, 1.0, 0.0).astype(BF16)
    f_cum_t = sum(_dot_tn(p, eye) for p in _split3(f_cum))
    pre_t = sum(_dot_tn(p, eye) for p in _split3(pre))
    lane_1 = lax.broadcasted_iota(jnp.int32, (1, LANES), 1)
    m_new = jnp.zeros((1, LANES), F32)

    for h in range(N_HEADS):
        kc = slice(h * DK, (h + 1) * DK)
        vc = slice(h * DV, (h + 1) * DV)
        f_col = f_cum[:, LANE_F + h:LANE_F + h + 1]
        i_col = pre[:, LANE_I + h:LANE_I + h + 1]
        f_row = f_cum_t[LANE_F + h:LANE_F + h + 1, :]
        i_row = pre_t[LANE_I + h:LANE_I + h + 1, :]
        dm = jnp.where(causal, f_col - f_row + i_row, -jnp.inf)
        m_prev = m_s[h][:, 0:1]
        inter = f_col + m_prev
        m_t = jnp.maximum(inter, jnp.max(dm, axis=1, keepdims=True))
        w = jnp.exp(dm - m_t)
        w_inter = jnp.exp(inter - m_t)
        q = qk[:, kc]
        k = qk[:, N_HEADS * DK + h * DK:N_HEADS * DK + (h + 1) * DK] * (DK ** -0.5)
        v = mv_ref[:, vc].astype(BF16)
        q_b = q.astype(BF16)
        c_prev = c_s[h]
        n_prev = n_s[h]
        s_qk = _dot_nt(q_b, k.astype(BF16)) * w
        num = _dot(s_qk.astype(BF16), v) + w_inter * _dot(q_b, c_prev.astype(BF16))
        den = (jnp.sum(s_qk, axis=1, keepdims=True)
               + w_inter * jnp.sum(q * n_prev, axis=1, keepdims=True))
        hh = num / jnp.maximum(jnp.abs(den), jnp.exp(-m_t))
        m_end = m_t[rows - 1:rows, :]
        w_s = jnp.exp(f_col[rows - 1:rows, :] - f_col + i_col - m_end)
        dec = w_inter[rows - 1:rows, :]
        k_w = k * w_s
        c_s[h] = dec * c_prev + _dot_tn(k_w.astype(BF16), v)
        n_s[h] = dec * n_prev + jnp.sum(k_w, axis=0, keepdims=True)
        m_s[h] = jnp.broadcast_to(m_end, (1, LANES))
        m_new = jnp.where(lane_1 == h, m_end, m_new)
        y = hh * lax.rsqrt(jnp.mean(hh * hh, axis=-1, keepdims=True) + EPS) * g_ml_ref[:, vc]
        o_ref[:, N_HEADS * DV + h * DV:N_HEADS * DV + (h + 1) * DV] = (
            y * _sigmoid(mo_ref[:, vc])).astype(o_ref.dtype)

    @pl.when(blk == n_blk - 1)
    def _():
        for h in range(N_HEADS):
            s_out_ref[0, h] = st_s[h].T
            c_out_ref[0, h] = c_s[h]
            n_out_ref[0, h:h + 1, :] = n_s[h]
        m_out_ref[0] = m_new
        cv_out_ref[0] = cb_s[8 - (CONV_W - 1):8, :]


def _mixers(p_big, p_small, row0, n_streams, t_len, rows, s0, c0, n0, m0, cv0, wts):
    assert t_len % rows == 0 and row0 % rows == 0 and rows % GLA_SUB == 0
    n_blk = t_len // rows
    b0 = row0 // rows

    def rmap(col):
        return lambda s, b: (b0 + s * n_blk + b, col)

    def smap(*zeros):
        return lambda s, b: (s,) + zeros

    def wmap(s, b):
        return (0, 0)

    wz_hi, wz_lo, bz, g_gla, wc, bc, bsm, g_ml = wts
    in_specs = [
        pl.BlockSpec((rows, N_HEADS * DK), rmap(0)),
        pl.BlockSpec((rows, N_HEADS * DK), rmap(1)),
        pl.BlockSpec((rows, N_HEADS * DV), rmap(1)),
        pl.BlockSpec((rows, N_HEADS * DV), rmap(2)),
        pl.BlockSpec((rows, QK_CH), rmap(3)),
        pl.BlockSpec((rows, N_HEADS * DV), rmap(4)),
        pl.BlockSpec((rows, N_HEADS * DV), rmap(5)),
        pl.BlockSpec((rows, LANES), rmap(0)),
        pl.BlockSpec((1, N_HEADS, DK, DV), smap(0, 0, 0)),
        pl.BlockSpec((1, N_HEADS, DK, DV), smap(0, 0, 0)),
        pl.BlockSpec((1, N_HEADS, DK), smap(0, 0)),
        pl.BlockSpec((1, 1, N_HEADS), smap(0, 0)),
        pl.BlockSpec((1, CONV_W - 1, QK_CH), smap(0, 0)),
        pl.BlockSpec(wz_hi.shape, wmap), pl.BlockSpec(wz_lo.shape, wmap),
        pl.BlockSpec(bz.shape, wmap), pl.BlockSpec(g_gla.shape, wmap),
        pl.BlockSpec(wc.shape, wmap), pl.BlockSpec(bc.shape, wmap),
        pl.BlockSpec(bsm.shape, wmap), pl.BlockSpec(g_ml.shape, wmap),
    ]
    n_rows = n_streams * t_len
    out_shape = [
        jax.ShapeDtypeStruct((n_rows, D_MODEL), BF16),
        jax.ShapeDtypeStruct((n_streams, N_HEADS, DK, DV), F32),
        jax.ShapeDtypeStruct((n_streams, N_HEADS, DK, DV), F32),
        jax.ShapeDtypeStruct((n_streams, N_HEADS, DK), F32),
        jax.ShapeDtypeStruct((n_streams, 1, LANES), F32),
        jax.ShapeDtypeStruct((n_streams, CONV_W - 1, QK_CH), F32),
    ]
    out_specs = [
        pl.BlockSpec((rows, D_MODEL), lambda s, b: (s * n_blk + b, 0)),
        pl.BlockSpec((1, N_HEADS, DK, DV), smap(0, 0, 0)),
        pl.BlockSpec((1, N_HEADS, DK, DV), smap(0, 0, 0)),
        pl.BlockSpec((1, N_HEADS, DK), smap(0, 0)),
        pl.BlockSpec((1, 1, LANES), smap(0, 0)),
        pl.BlockSpec((1, CONV_W - 1, QK_CH), smap(0, 0)),
    ]
    scratch = [
        pltpu.VMEM((N_HEADS, DV, DK), F32),
        pltpu.VMEM((N_HEADS, DK, DV), F32),
        pltpu.VMEM((N_HEADS, 1, DK), F32),
        pltpu.VMEM((N_HEADS, 1, LANES), F32),
        pltpu.VMEM((rows + 8, QK_CH), F32),
        pltpu.VMEM((rows, N_HEADS * DK), F32),
    ]
    return pl.pallas_call(
        functools.partial(_mixer_kernel, rows),
        grid=(n_streams, n_blk),
        in_specs=in_specs, out_specs=out_specs, out_shape=out_shape,
        scratch_shapes=scratch,
        compiler_params=pltpu.CompilerParams(
            dimension_semantics=("arbitrary", "arbitrary"), vmem_limit_bytes=VMEM_LIMIT),
        name=f"mixers_r{rows}",
    )(p_big, p_big, p_big, p_big, p_big, p_big, p_big, p_small, s0, c0, n0, m0, cv0, *wts)


def _pack_rows(ref, val):
    word = pltpu.pack_elementwise([val[:, :HALF], val[:, HALF:]], packed_dtype=BF16)
    rows = val.shape[0]
    for j in range(REC_ROWS):
        ref[pl.ds(j, rows, stride=REC_ROWS), :] = word[:, j * LANES:(j + 1) * LANES]


def _unpack_rows(ref, rows):
    word = jnp.concatenate(
        [ref[pl.ds(j, rows, stride=REC_ROWS), :] for j in range(REC_ROWS)], axis=1)
    lo = pltpu.unpack_elementwise(word, index=0, packed_dtype=BF16, unpacked_dtype=F32)
    hi = pltpu.unpack_elementwise(word, index=1, packed_dtype=BF16, unpacked_dtype=F32)
    return lo, hi


def _outproj_kernel(n_prompt_tiles, op_ref, os_ref, xp_ref, xs_ref, w_ref, g_ref, wr_ref,
                    x1_ref, xn_ref, route_ref, code_ref, cnt_ref, cnt_s):
    i = pl.program_id(0)

    @pl.when(i == 0)
    def _():
        cnt_s[...] = jnp.zeros_like(cnt_s)

    def body(o_ref, x_ref):
        x1 = x_ref[...] + _dot(o_ref[...], w_ref[...])
        x1_ref[...] = x1
        xn = _rmsnorm(x1, g_ref[...])
        _pack_rows(xn_ref, xn)
        xn_hi, xn_mid, xn_lo = _split3(xn)
        pa = _dot(xn_hi, wr_ref[...])
        pb = _dot(xn_mid, wr_ref[:, :2 * LANES])
        pc = _dot(xn_lo, wr_ref[:, :LANES])
        lg = (((pa[:, 2 * LANES:] + pc + pb[:, LANES:]) + (pa[:, LANES:2 * LANES] + pb[:, :LANES]))
              + pa[:, :LANES])
        lane = lax.broadcasted_iota(jnp.int32, lg.shape, 1).astype(F32)
        neg = -jnp.inf
        lgm = jnp.where(lane < N_GROUPS, lg, neg)
        mg = jnp.max(lgm, axis=1, keepdims=True)
        g_idx = jnp.min(jnp.where(lgm == mg, lane, float(LANES)), axis=1, keepdims=True)
        g_val = 1.0 / jnp.sum(jnp.where(lane < N_GROUPS, jnp.exp(lg - mg), 0.0), axis=1, keepdims=True)
        e0 = ROUTER_E0 + EXPERTS_PER_GROUP * g_idx
        le = jnp.where((lane >= e0) & (lane < e0 + EXPERTS_PER_GROUP), lg, neg)
        v1 = jnp.max(le, axis=1, keepdims=True)
        i1 = jnp.min(jnp.where(le == v1, lane, float(LANES)), axis=1, keepdims=True)
        le2 = jnp.where(lane == i1, neg, le)
        v2 = jnp.max(le2, axis=1, keepdims=True)
        i2 = jnp.min(jnp.where(le2 == v2, lane, float(LANES)), axis=1, keepdims=True)
        t = jnp.exp(v2 - v1)
        w1 = g_val / (1.0 + t)
        w2 = g_val * t / (1.0 + t)
        oh1 = lane == i1
        oh2 = lane == i2
        hot = jnp.where(oh1 | oh2, 1.0, 0.0)
        r_i = lax.broadcasted_iota(jnp.int32, (ROW_TILE, ROW_TILE), 0)
        c_i = lax.broadcasted_iota(jnp.int32, (ROW_TILE, ROW_TILE), 1)
        before = jnp.where(c_i < r_i, 1.0, 0.0).astype(BF16)
        seen = _dot(before, hot.astype(BF16)) + cnt_s[...]
        rank1 = jnp.sum(jnp.where(oh1, seen, 0.0), axis=1, keepdims=True)
        rank2 = jnp.sum(jnp.where(oh2, seen, 0.0), axis=1, keepdims=True)
        cnt_s[...] += jnp.sum(hot, axis=0, keepdims=True)
        code1 = (i1 - ROUTER_E0) * float(SLOT_CODE) + rank1
        code2 = (i2 - ROUTER_E0) * float(SLOT_CODE) + rank2
        route = jnp.zeros_like(lg)
        for k, col in ((0, code1), (1, code2), (4, w1), (5, w2)):
            route = jnp.where(lane == k, col, route)
        route_ref[...] = route
        code_ref[0] = route.T[0:8, :].astype(jnp.int32)

    @pl.when(i < n_prompt_tiles)
    def _():
        body(op_ref, xp_ref)

    @pl.when(i >= n_prompt_tiles)
    def _():
        body(os_ref, xs_ref)

    cnt_ref[...] = cnt_s[...]


def _outproj(o_p, o_s, xp, xs, w_out, g, wr):
    n_p, n_s = xp.shape[0], xs.shape[0]
    npt = n_p // ROW_TILE
    n = n_p + n_s

    def pmap(i):
        return (jnp.minimum(i, npt - 1), 0)

    def cmap(i):
        return (0, 0)

    def omap(i):
        return (i, 0)

    return pl.pallas_call(
        functools.partial(_outproj_kernel, npt),
        grid=(npt + 1,),
        in_specs=[
            pl.BlockSpec((ROW_TILE, D_MODEL), pmap), pl.BlockSpec((ROW_TILE, D_MODEL), cmap),
            pl.BlockSpec((ROW_TILE, D_MODEL), pmap), pl.BlockSpec((ROW_TILE, D_MODEL), cmap),
            pl.BlockSpec((D_MODEL, D_MODEL), cmap), pl.BlockSpec((1, D_MODEL), cmap),
            pl.BlockSpec((D_MODEL, 3 * LANES), cmap),
        ],
        out_specs=[
            pl.BlockSpec((ROW_TILE, D_MODEL), omap), pl.BlockSpec((ROW_TILE * REC_ROWS, LANES), omap),
            pl.BlockSpec((ROW_TILE, LANES), omap), pl.BlockSpec((1, 8, ROW_TILE), lambda i: (i, 0, 0)),
            pl.BlockSpec((1, LANES), cmap),
        ],
        out_shape=[
            jax.ShapeDtypeStruct((n, D_MODEL), F32),
            jax.ShapeDtypeStruct((n * REC_ROWS, LANES), jnp.uint32),
            jax.ShapeDtypeStruct((n, LANES), F32),
            jax.ShapeDtypeStruct((npt + 1, 8, ROW_TILE), jnp.int32),
            jax.ShapeDtypeStruct((1, LANES), F32),
        ],
        scratch_shapes=[pltpu.VMEM((1, LANES), F32)],
        compiler_params=pltpu.CompilerParams(
            dimension_semantics=("arbitrary",), vmem_limit_bytes=VMEM_LIMIT),
        name="outproj_router",
    )(o_p, o_s, xp, xs, w_out, g, wr)


def _rec(ref, idx):
    return ref.at[pl.ds(pl.multiple_of(idx * REC_ROWS, REC_ROWS), REC_ROWS)]


def _zero_records_kernel(o_ref):
    z = jnp.zeros(o_ref.shape, F32)
    o_ref[...] = pltpu.pack_elementwise([z, z], packed_dtype=BF16)


def _slot_row(code, row0_ref):
    return row0_ref[lax.shift_right_logical(code, SLOT_CODE_BITS)] + (code & (SLOT_CODE - 1))


def _dispatch_kernel(t_max, c1_ref, c2_ref, row0_ref, padlo_ref, padn_ref, nt_ref, xn_ref, xs_ref,
                     zero_s, sem, zsem):
    step = pl.program_id(0)
    base = step * ROW_TILE

    @pl.when(step == 0)
    def _():
        _zero_records_kernel(zero_s)

        def pad_copy(row):
            return pltpu.make_async_copy(_rec(zero_s, 0), _rec(xs_ref, row), zsem)

        def per_expert(e, c):
            lo = padlo_ref[e]
            n = padn_ref[e]
            lax.fori_loop(0, n, lambda r, cc: (pad_copy(lo + r).start(), cc)[1], 0)
            lax.fori_loop(0, n, lambda r, cc: (pad_copy(lo + r).wait(), cc)[1], 0)
            return c

        lax.fori_loop(0, N_EXPERTS, per_expert, 0)

        def per_unused_tile(t, c):
            rows = EXP_TILE * REC_ROWS
            cp = pltpu.make_async_copy(zero_s, xs_ref.at[pl.ds(pl.multiple_of(t * rows, rows), rows)], zsem)
            cp.start()
            cp.wait()
            return c

        lax.fori_loop(nt_ref[0], t_max, per_unused_tile, 0)

    def copies(r):
        src = _rec(xn_ref, r)
        return (pltpu.make_async_copy(src, _rec(xs_ref, _slot_row(c1_ref[base + r], row0_ref)), sem),
                pltpu.make_async_copy(src, _rec(xs_ref, _slot_row(c2_ref[base + r], row0_ref)), sem))

    def start(g, c):
        for u in range(DMA_UNROLL):
            for cp in copies(g * DMA_UNROLL + u):
                cp.start()
        return c

    def wait(g, c):
        for u in range(DMA_UNROLL):
            for cp in copies(g * DMA_UNROLL + u):
                cp.wait()
        return c

    lax.fori_loop(0, ROW_TILE // DMA_UNROLL, start, 0)
    lax.fori_loop(0, ROW_TILE // DMA_UNROLL, wait, 0)


def _dispatch(code1, code2, row0, pad_lo, pad_n, n_tiles, xn_rec, t_max):
    n = code1.shape[0]
    return pl.pallas_call(
        functools.partial(_dispatch_kernel, t_max),
        grid_spec=pltpu.PrefetchScalarGridSpec(
            num_scalar_prefetch=6,
            grid=(n // ROW_TILE,),
            in_specs=[pl.BlockSpec((ROW_TILE * REC_ROWS, LANES), lambda i, *_: (i, 0))],
            out_specs=pl.BlockSpec(memory_space=pl.ANY),
            scratch_shapes=[pltpu.VMEM((EXP_TILE * REC_ROWS, LANES), jnp.uint32),
                            pltpu.SemaphoreType.DMA, pltpu.SemaphoreType.DMA],
        ),
        out_shape=jax.ShapeDtypeStruct((t_max * EXP_TILE * REC_ROWS, LANES), jnp.uint32),
        compiler_params=pltpu.CompilerParams(
            dimension_semantics=("arbitrary",), vmem_limit_bytes=VMEM_LIMIT),
        name="moe_dispatch",
    )(code1, code2, row0, pad_lo, pad_n, n_tiles, xn_rec)


def _experts_kernel(te_ref, nt_ref, xs_ref, wg_ref, wu_ref, wd_ref, ys_ref, wg_s, wu_s, wd_s):
    t = pl.program_id(0)
    prev = te_ref[jnp.maximum(t - 1, 0)]

    @pl.when((t == 0) | (te_ref[t] != prev))
    def _():
        wg_s[...] = wg_ref[0].astype(BF16)
        wu_s[...] = wu_ref[0].astype(BF16)
        wd_s[...] = wd_ref[0].astype(BF16)

    @pl.when(t < nt_ref[0])
    def _():
        lo, hi = _unpack_rows(xs_ref, EXP_TILE)
        x = jnp.concatenate([lo.astype(BF16), hi.astype(BF16)], axis=1)
        hg = _dot(x, wg_s[...])
        hu = _dot(x, wu_s[...])
        act = hg * _sigmoid(hg) * hu
        _pack_rows(ys_ref, _dot(act.astype(BF16), wd_s[...]))

    @pl.when(t >= nt_ref[0])
    def _():
        _zero_records_kernel(ys_ref)


def _experts(tile_expert, n_tiles, xs, wg, wu, wd):
    t_max = tile_expert.shape[0]

    def tmap(t, te, nt):
        return (jnp.minimum(t, nt[0] - 1), 0)

    def wmap(t, te, nt):
        return (te[t], 0, 0)

    return pl.pallas_call(
        _experts_kernel,
        grid_spec=pltpu.PrefetchScalarGridSpec(
            num_scalar_prefetch=2,
            grid=(t_max,),
            in_specs=[pl.BlockSpec((EXP_TILE * REC_ROWS, LANES), tmap),
                      pl.BlockSpec((1, D_MODEL, D_EXPERT), wmap),
                      pl.BlockSpec((1, D_MODEL, D_EXPERT), wmap),
                      pl.BlockSpec((1, D_EXPERT, D_MODEL), wmap)],
            out_specs=pl.BlockSpec((EXP_TILE * REC_ROWS, LANES), lambda t, te, nt: (t, 0)),
            scratch_shapes=[pltpu.VMEM((D_MODEL, D_EXPERT), BF16), pltpu.VMEM((D_MODEL, D_EXPERT), BF16),
                            pltpu.VMEM((D_EXPERT, D_MODEL), BF16)],
        ),
        out_shape=jax.ShapeDtypeStruct(xs.shape, xs.dtype),
        compiler_params=pltpu.CompilerParams(
            dimension_semantics=("arbitrary",), vmem_limit_bytes=VMEM_LIMIT),
        name="moe_experts",
    )(tile_expert, n_tiles, xs, wg, wu, wd)


def _combine_kernel(n_prompt_tiles, c1_ref, c2_ref, row0_ref, x1_ref, route_ref, g_ref, ys_ref,
                    yp_ref, ysm_ref, a_s, b_s, sem):
    i = pl.program_id(0)
    n_steps = pl.num_programs(0)

    def copies(step, slot, r):
        tok = step * ROW_TILE + r
        return (pltpu.make_async_copy(_rec(ys_ref, _slot_row(c1_ref[tok], row0_ref)),
                                      _rec(a_s.at[slot], r), sem.at[slot]),
                pltpu.make_async_copy(_rec(ys_ref, _slot_row(c2_ref[tok], row0_ref)),
                                      _rec(b_s.at[slot], r), sem.at[slot]))

    def start_all(step, slot):
        def start(g, c):
            for u in range(DMA_UNROLL):
                for cp in copies(step, slot, g * DMA_UNROLL + u):
                    cp.start()
            return c
        lax.fori_loop(0, ROW_TILE // DMA_UNROLL, start, 0)

    def wait_all(step, slot):
        def wait(g, c):
            for u in range(DMA_UNROLL):
                for cp in copies(step, slot, g * DMA_UNROLL + u):
                    cp.wait()
            return c
        lax.fori_loop(0, ROW_TILE // DMA_UNROLL, wait, 0)

    slot = i % 2

    @pl.when(i == 0)
    def _():
        start_all(0, 0)

    @pl.when(i + 1 < n_steps)
    def _():
        start_all(i + 1, 1 - slot)

    wait_all(i, slot)
    a_lo, a_hi = _unpack_rows(a_s.at[slot], ROW_TILE)
    b_lo, b_hi = _unpack_rows(b_s.at[slot], ROW_TILE)
    route = route_ref[...]
    w1 = route[:, 4:5]
    w2 = route[:, 5:6]
    moe = jnp.concatenate([w1 * a_lo + w2 * b_lo, w1 * a_hi + w2 * b_hi], axis=1)
    y = _rmsnorm(x1_ref[...] + moe, g_ref[...])

    @pl.when(i < n_prompt_tiles)
    def _():
        yp_ref[...] = y

    @pl.when(i >= n_prompt_tiles)
    def _():
        ysm_ref[...] = y


def _combine(code1, code2, row0, x1, route, g, ys, n_p):
    n = x1.shape[0]
    npt = n_p // ROW_TILE

    def omap(i, *_):
        return (i, 0)

    return pl.pallas_call(
        functools.partial(_combine_kernel, npt),
        grid_spec=pltpu.PrefetchScalarGridSpec(
            num_scalar_prefetch=3,
            grid=(n // ROW_TILE,),
            in_specs=[pl.BlockSpec((ROW_TILE, D_MODEL), omap),
                      pl.BlockSpec((ROW_TILE, LANES), omap),
                      pl.BlockSpec((1, D_MODEL), lambda i, *_: (0, 0)),
                      pl.BlockSpec(memory_space=pl.ANY)],
            out_specs=[pl.BlockSpec((ROW_TILE, D_MODEL), lambda i, *_: (jnp.minimum(i, npt - 1), 0)),
                       pl.BlockSpec((ROW_TILE, D_MODEL), lambda i, *_: (jnp.maximum(i - npt, 0), 0))],
            scratch_shapes=[pltpu.VMEM((2, ROW_TILE * REC_ROWS, LANES), jnp.uint32),
                            pltpu.VMEM((2, ROW_TILE * REC_ROWS, LANES), jnp.uint32),
                            pltpu.SemaphoreType.DMA((2,))],
        ),
        out_shape=[jax.ShapeDtypeStruct((n_p, D_MODEL), F32),
                   jax.ShapeDtypeStruct((n - n_p, D_MODEL), F32)],
        compiler_params=pltpu.CompilerParams(
            dimension_semantics=("arbitrary",), vmem_limit_bytes=VMEM_LIMIT),
        name="moe_combine",
    )(code1, code2, row0, x1, route, g, ys)


def _pad_lanes(w):
    return jnp.pad(w, ((0, 0), (0, LANES - w.shape[1])))


def _moe_plan(counts, n_tokens):
    cnt = counts[0, ROUTER_E0:ROUTER_E0 + N_EXPERTS].astype(jnp.int32)
    tiles = (cnt + EXP_TILE - 1) // EXP_TILE
    tile_end = jnp.cumsum(tiles)
    row0 = (tile_end - tiles) * EXP_TILE
    t_max = 2 * n_tokens // EXP_TILE + N_EXPERTS
    tile_ids = jnp.arange(t_max, dtype=jnp.int32)
    tile_expert = jnp.minimum(
        jnp.sum((tile_ids[:, None] >= tile_end[None, :]).astype(jnp.int32), axis=1), N_EXPERTS - 1)
    return row0, row0 + cnt, tiles * EXP_TILE - cnt, tile_expert, tile_end[-1:].astype(jnp.int32), t_max


def kernel(x_prompt, x_sample, state_gla_S, state_mlstm_C, state_mlstm_n, state_mlstm_m, cache_mlstm_conv, g_mix_norm, w_in, w_gla_gate_up, b_gla_gate_up, g_gla_out, w_mlstm_conv, b_mlstm_conv, b_mlstm_i, b_mlstm_f, g_mlstm_out, w_out, g_ffn_norm, w_router_group, w_router_expert, w_exp_gate, w_exp_up, w_exp_down, g_final):
    depth = w_in.shape[0]
    assert depth == 1
    bp, t_p, _ = x_prompt.shape
    bs, t_s, _ = x_sample.shape
    assert bp == 1
    xp = x_prompt.reshape(bp * t_p, D_MODEL)
    xs = x_sample.reshape(bs * t_s, D_MODEL)
    n_p = xp.shape[0]

    w = w_in[0]
    c_gz = 3072
    c_mqk = c_gz + GLA_GATE_RANK
    c_mi = c_mqk + 3072
    w_big = jnp.concatenate([w[:, :c_gz].astype(BF16), w[:, c_mqk:c_mi].astype(BF16)], axis=1)
    ws_hi, ws_lo = _split2(_pad_lanes(jnp.concatenate([w[:, c_gz:c_mqk], w[:, c_mi:]], axis=1)))
    wz = jnp.pad(w_gla_gate_up[0], ((0, LANES - GLA_GATE_RANK), (0, 0)))
    wz_hi, wz_lo = _split2(wz)
    bsm = _pad_lanes(jnp.concatenate(
        [jnp.zeros((1, GLA_GATE_RANK), F32), b_mlstm_i[0][None], b_mlstm_f[0][None]], axis=1))
    mix_w = (wz_hi, wz_lo, b_gla_gate_up[0][None], g_gla_out[0][None], w_mlstm_conv[0],
             b_mlstm_conv[0][None], bsm, g_mlstm_out[0][None])
    wr = jnp.concatenate(
        _split3(_pad_lanes(jnp.concatenate([w_router_group[0], w_router_expert[0]], axis=1))), axis=1)

    p_big, p_small = _inproj(xp, xs, g_mix_norm[0][None], w_big, ws_hi, ws_lo)

    dt = x_prompt.dtype
    z_s = jnp.zeros((bp, N_HEADS, DK, DV), dt)
    z_n = jnp.zeros((bp, N_HEADS, DK), dt)
    z_m = jnp.zeros((bp, 1, N_HEADS), dt)
    z_cv = jnp.zeros((bp, CONV_W - 1, QK_CH), dt)
    o_p, p_S, p_C, p_n, p_m, p_cv = _mixers(p_big, p_small, 0, bp, t_p, 64, z_s, z_s, z_n, z_m, z_cv, mix_w)
    o_s, s_S, s_C, s_n, s_m, s_cv = _mixers(
        p_big, p_small, n_p, bs, t_s, t_s, state_gla_S[0], state_mlstm_C[0], state_mlstm_n[0],
        state_mlstm_m[0][:, None, :], cache_mlstm_conv[0], mix_w)

    x1, xn_rec, route, codes, counts = _outproj(
        o_p, o_s, xp, xs, w_out[0].astype(BF16), g_ffn_norm[0][None], wr)
    code1 = codes[:, 0, :].reshape(-1)
    code2 = codes[:, 1, :].reshape(-1)
    row0, pad_lo, pad_n, tile_expert, n_tiles, t_max = _moe_plan(counts, x1.shape[0])
    xs_rec = _dispatch(code1, code2, row0, pad_lo, pad_n, n_tiles, xn_rec, t_max)
    ys_rec = _experts(tile_expert, n_tiles, xs_rec, w_exp_gate[0], w_exp_up[0], w_exp_down[0])
    y_p, y_s = _combine(code1, code2, row0, x1, route, g_final[None], ys_rec, n_p)

    return (y_p.reshape(x_prompt.shape), y_s.reshape(x_sample.shape),
            p_S[None], p_C[None], p_n[None], p_m[:, 0, :N_HEADS][None], p_cv[None],
            s_S[None], s_C[None], s_n[None], s_m[:, 0, :N_HEADS][None], s_cv[None])
```

```python
import functools

import jax
import jax.numpy as jnp
from jax import lax
from jax.experimental import pallas as pl
from jax.experimental.pallas import tpu as pltpu

F32 = jnp.float32
BF16 = jnp.bfloat16

D_MODEL = 2048
N_HEADS = 4
DK = 128
DV = 256
GLA_GATE_RANK = 16
GLA_GATE_NORM = 16.0
CONV_W = 4
QK_CH = 2 * N_HEADS * DK
N_GROUPS = 4
EXPERTS_PER_GROUP = 8
N_EXPERTS = N_GROUPS * EXPERTS_PER_GROUP
D_EXPERT = 256
EPS = 1e-6

LANES = 128
GLA_SUB = 16
GLA_SAFE_DECAY = 60.0
MIX_ROWS = 128
ROW_TILE = 512
PROJ_BIG = 6 * 1024
LANE_I = GLA_GATE_RANK
LANE_F = GLA_GATE_RANK + N_HEADS
ROUTER_E0 = N_GROUPS
HALF = D_MODEL // 2
REC_ROWS = HALF // LANES
EXP_TILE = 256
DMA_UNROLL = 8
SLOT_CODE_BITS = 16
SLOT_CODE = 1 << SLOT_CODE_BITS

VMEM_LIMIT = 56 * 1024 * 1024


def _dot(a, b):
    return jnp.dot(a, b, preferred_element_type=F32)


def _dot_nt(a, b):
    return lax.dot_general(a, b, (((1,), (1,)), ((), ())), preferred_element_type=F32)


def _dot_tn(a, b):
    return lax.dot_general(a, b, (((0,), (0,)), ((), ())), preferred_element_type=F32)


def _split2(x):
    hi = x.astype(BF16)
    lo = (x - hi.astype(F32)).astype(BF16)
    return hi, lo


def _split3(x):
    hi = x.astype(BF16)
    r = x - hi.astype(F32)
    mid = r.astype(BF16)
    lo = (r - mid.astype(F32)).astype(BF16)
    return hi, mid, lo


def _dot_exact_lhs(m_bf16, x):
    hi, mid, lo = _split3(x)
    return _dot(m_bf16, hi) + _dot(m_bf16, mid) + _dot(m_bf16, lo)


def _dot_hilo(a_hi, a_lo, b_hi, b_lo):
    return _dot(a_hi, b_hi) + _dot(a_lo, b_hi) + _dot(a_hi, b_lo)


def _log_sigmoid(z):
    return jnp.minimum(z, 0.0) - jnp.log1p(jnp.exp(-jnp.abs(z)))


def _sigmoid(z):
    return 1.0 / (1.0 + jnp.exp(-z))


def _rmsnorm(x, g):
    return x * lax.rsqrt(jnp.mean(x * x, axis=-1, keepdims=True) + EPS) * g


def _inproj_kernel(n_prompt_tiles, xp_ref, xs_ref, g_ref, w_ref, ws_hi_ref, ws_lo_ref,
                   p_ref, ps_ref, h_s):
    i = pl.program_id(0)
    j = pl.program_id(1)

    def norm(x_ref):
        y = _rmsnorm(x_ref[...], g_ref[...])
        y_hi, y_lo = _split2(y)
        h_s[...] = y_hi
        ps_ref[...] = _dot_hilo(y_hi, y_lo, ws_hi_ref[...], ws_lo_ref[...])

    @pl.when((j == 0) & (i < n_prompt_tiles))
    def _():
        norm(xp_ref)

    @pl.when((j == 0) & (i >= n_prompt_tiles))
    def _():
        norm(xs_ref)

    tn = p_ref.shape[1]
    p_ref[...] = _dot(h_s[...], w_ref[:, pl.ds(pl.multiple_of(j * tn, tn), tn)])


def _inproj(xp, xs, g, w_big, ws_hi, ws_lo):
    n_p, n_s = xp.shape[0], xs.shape[0]
    assert n_p % ROW_TILE == 0 and n_s == ROW_TILE
    npt = n_p // ROW_TILE
    n = n_p + n_s
    tn = 1024
    grid = (npt + 1, PROJ_BIG // tn)
    once = pl.Buffered(1)
    return pl.pallas_call(
        functools.partial(_inproj_kernel, npt),
        grid=grid,
        in_specs=[
            pl.BlockSpec((ROW_TILE, D_MODEL), lambda i, j: (jnp.minimum(i, npt - 1), 0)),
            pl.BlockSpec((ROW_TILE, D_MODEL), lambda i, j: (0, 0)),
            pl.BlockSpec((1, D_MODEL), lambda i, j: (0, 0)),
            pl.BlockSpec((D_MODEL, PROJ_BIG), lambda i, j: (0, 0), pipeline_mode=once),
            pl.BlockSpec((D_MODEL, LANES), lambda i, j: (0, 0), pipeline_mode=once),
            pl.BlockSpec((D_MODEL, LANES), lambda i, j: (0, 0), pipeline_mode=once),
        ],
        out_specs=[
            pl.BlockSpec((ROW_TILE, tn), lambda i, j: (i, j)),
            pl.BlockSpec((ROW_TILE, LANES), lambda i, j: (i, 0)),
        ],
        out_shape=[
            jax.ShapeDtypeStruct((n, PROJ_BIG), F32),
            jax.ShapeDtypeStruct((n, LANES), F32),
        ],
        scratch_shapes=[pltpu.VMEM((ROW_TILE, D_MODEL), BF16)],
        compiler_params=pltpu.CompilerParams(
            dimension_semantics=("arbitrary", "arbitrary"), vmem_limit_bytes=VMEM_LIMIT),
        name="inproj",
    )(xp, xs, g, w_big, ws_hi, ws_lo)


def _mixer_kernel(rows, gq_ref, gk_ref, gv_ref, gg_ref, mqk_ref, mv_ref, mo_ref, sm_ref,
                  s0_ref, c0_ref, n0_ref, m0_ref, cv0_ref,
                  wz_hi_ref, wz_lo_ref, bz_ref, g_gla_ref, wc_ref, bc_ref, bsm_ref, g_ml_ref,
                  o_ref, s_out_ref, c_out_ref, n_out_ref, m_out_ref, cv_out_ref,
                  st_s, c_s, n_s, m_s, cb_s, b_s):
    blk = pl.program_id(1)
    n_blk = pl.num_programs(1)

    @pl.when(blk == 0)
    def _():
        for h in range(N_HEADS):
            st_s[h] = s0_ref[0, h].T
            c_s[h] = c0_ref[0, h]
            n_s[h] = n0_ref[0, h:h + 1, :]
            m_s[h] = jnp.broadcast_to(m0_ref[0, :, h:h + 1], (1, LANES))
        cb_s[0:8, :] = jnp.zeros((8, QK_CH), F32)
        cb_s[8 - (CONV_W - 1):8, :] = cv0_ref[0]

    small = sm_ref[...]
    row_i = lax.broadcasted_iota(jnp.int32, (rows, rows), 0)
    col_i = lax.broadcasted_iota(jnp.int32, (rows, rows), 1)
    causal = col_i <= row_i

    sm_hi, sm_lo = _split2(small)
    z = _dot_hilo(sm_hi, sm_lo, wz_hi_ref[...], wz_lo_ref[...]) + bz_ref[...]
    log_a = _log_sigmoid(z) * (1.0 / GLA_GATE_NORM)
    tri = jnp.where(causal, 1.0, 0.0).astype(BF16)
    b_blk = _dot_exact_lhs(tri, log_a)
    b_last = b_blk[rows - 1:rows, :]
    factorable = jnp.min(b_last) >= -GLA_SAFE_DECAY

    def gla_finish(o, vc, gate):
        y = o * lax.rsqrt(jnp.mean(o * o, axis=-1, keepdims=True) + EPS) * g_gla_ref[:, vc]
        return (y * (gate * _sigmoid(gate))).astype(o_ref.dtype)

    @pl.when(factorable)
    def _():
        for h in range(N_HEADS):
            kc = slice(h * DK, (h + 1) * DK)
            vc = slice(h * DV, (h + 1) * DV)
            bh = b_blk[:, kc]
            qh = gq_ref[:, kc] * (DK ** -0.5)
            kh = gk_ref[:, kc]
            vh = gv_ref[:, vc].astype(BF16)
            st = st_s[h]
            q_dec = (qh * jnp.exp(bh)).astype(BF16)
            k_inv = (kh * jnp.exp(-bh)).astype(BF16)
            a = jnp.where(causal, _dot_nt(q_dec, k_inv), 0.0)
            o = _dot(a.astype(BF16), vh) + _dot_nt(q_dec, st.astype(BF16))
            bh_end = b_last[:, kc]
            k_dec = (kh * jnp.exp(bh_end - bh)).astype(BF16)
            st_s[h] = st * jnp.exp(bh_end) + _dot_tn(vh, k_dec)
            o_ref[:, vc] = gla_finish(o, vc, gg_ref[:, vc])

    sub_r = lax.broadcasted_iota(jnp.int32, (GLA_SUB, LANES), 0)
    sub_l = lax.broadcasted_iota(jnp.int32, (GLA_SUB, LANES), 1)

    def gla_sub(c, carry):
        r0 = pl.multiple_of(c * GLA_SUB, GLA_SUB)
        rs = pl.ds(r0, GLA_SUB)
        for h in range(N_HEADS):
            kc = slice(h * DK, (h + 1) * DK)
            vc = slice(h * DV, (h + 1) * DV)
            bh = b_s[rs, kc]
            qh = gq_ref[rs, kc] * (DK ** -0.5)
            kh = gk_ref[rs, kc]
            vh = gv_ref[rs, vc].astype(BF16)
            st = st_s[h]
            o = _dot_nt((qh * jnp.exp(bh)).astype(BF16), st.astype(BF16))
            a = jnp.zeros((GLA_SUB, LANES), F32)
            for s in range(GLA_SUB):
                e = jnp.exp(jnp.minimum(bh - bh[s:s + 1, :], 0.0))
                col = jnp.sum(qh * (kh[s:s + 1, :] * e), axis=1, keepdims=True)
                a = jnp.where((sub_l == s) & (sub_r >= s), col, a)
            o = o + _dot(a[:, :GLA_SUB].astype(BF16), vh)
            b_end = bh[GLA_SUB - 1:GLA_SUB, :]
            k_dec = kh * jnp.exp(b_end - bh)
            st_s[h] = st * jnp.exp(b_end) + _dot_tn(vh, k_dec.astype(BF16))
            o_ref[rs, vc] = gla_finish(o, vc, gg_ref[rs, vc])
        return carry

    @pl.when(jnp.logical_not(factorable))
    def _():
        same_sub = (row_i // GLA_SUB) == (col_i // GLA_SUB)
        blk_tri = jnp.where(causal & same_sub, 1.0, 0.0).astype(BF16)
        b_s[...] = _dot_exact_lhs(blk_tri, log_a)
        lax.fori_loop(0, rows // GLA_SUB, gla_sub, 0)

    cb_s[8:8 + rows, :] = mqk_ref[...]
    conv = bc_ref[...]
    for j in range(CONV_W):
        conv = conv + cb_s[8 - (CONV_W - 1) + j:8 - (CONV_W - 1) + j + rows, :] * wc_ref[j:j + 1, :]
    cb_s[0:8, :] = cb_s[rows:rows + 8, :]
    qk = conv * _sigmoid(conv)

    pre = small + bsm_ref[...]
    log_f = _log_sigmoid(pre)
    f_cum =_dot_exact_lhs(tri, log_f)
    eye = jnp.where(row_i == col_i, 1.0, 0.0).astype(BF16)
    f_cum_t = sum(_dot_tn(p, eye) for p in _split3(f_cum))
    pre_t = sum(_dot_tn(p, eye) for p in _split3(pre))
    lane_1 = lax.broadcasted_iota(jnp.int32, (1, LANES), 1)
    m_new = jnp.zeros((1, LANES), F32)

    for h in range(N_HEADS):
        kc = slice(h * DK, (h + 1) * DK)
        vc = slice(h * DV, (h + 1) * DV)
        f_col = f_cum[:, LANE_F + h:LANE_F + h + 1]
        i_col = pre[:, LANE_I + h:LANE_I + h + 1]
        f_row = f_cum_t[LANE_F + h:LANE_F + h + 1, :]
        i_row = pre_t[LANE_I + h:LANE_I + h + 1, :]
        dm = jnp.where(causal, f_col - f_row + i_row, -jnp.inf)
        m_prev = m_s[h][:, 0:1]
        inter = f_col + m_prev
        m_t = jnp.maximum(inter, jnp.max(dm, axis=1, keepdims=True))
        w = jnp.exp(dm - m_t)
        w_inter = jnp.exp(inter - m_t)
        q = qk[:, kc]
        k = qk[:, N_HEADS * DK + h * DK:N_HEADS * DK + (h + 1) * DK] * (DK ** -0.5)
        v = mv_ref[:, vc].astype(BF16)
        q_b = q.astype(BF16)
        c_prev = c_s[h]
        n_prev = n_s[h]
        s_qk = _dot_nt(q_b, k.astype(BF16)) * w
        num = _dot(s_qk.astype(BF16), v) + w_inter * _dot(q_b, c_prev.astype(BF16))
        den = (jnp.sum(s_qk, axis=1, keepdims=True)
               + w_inter * jnp.sum(q * n_prev, axis=1, keepdims=True))
        hh = num / jnp.maximum(jnp.abs(den), jnp.exp(-m_t))
        m_end = m_t[rows - 1:rows, :]
        w_s = jnp.exp(f_col[rows - 1:rows, :] - f_col + i_col - m_end)
        dec = w_inter[rows - 1:rows, :]
        k_w = k * w_s
        c_s[h] = dec * c_prev + _dot_tn(k_w.astype(BF16), v)
        n_s[h] = dec * n_prev + jnp.sum(k_w, axis=0, keepdims=True)
        m_s[h] = jnp.broadcast_to(m_end, (1, LANES))
        m_new = jnp.where(lane_1 == h, m_end, m_new)
        y = hh * lax.rsqrt(jnp.mean(hh * hh, axis=-1, keepdims=True) + EPS) * g_ml_ref[:, vc]
        o_ref[:, N_HEADS * DV + h * DV:N_HEADS * DV + (h + 1) * DV] = (
            y * _sigmoid(mo_ref[:, vc])).astype(o_ref.dtype)

    @pl.when(blk == n_blk - 1)
    def _():
        for h in range(N_HEADS):
            s_out_ref[0, h] = st_s[h].T
            c_out_ref[0, h] = c_s[h]
            n_out_ref[0, h:h + 1, :] = n_s[h]
        m_out_ref[0] = m_new
        cv_out_ref[0] = cb_s[8 - (CONV_W - 1):8, :]


def _mixers(p_big, p_small, row0, n_streams, t_len, rows, s0, c0, n0, m0, cv0, wts):
    assert t_len % rows == 0 and row0 % rows == 0 and rows % GLA_SUB == 0
    n_blk = t_len // rows
    b0 = row0 // rows

    def rmap(col):
        return lambda s, b: (b0 + s * n_blk + b, col)

    def smap(*zeros):
        return lambda s, b: (s,) + zeros

    def wmap(s, b):
        return (0, 0)

    wz_hi, wz_lo, bz, g_gla, wc, bc, bsm, g_ml = wts
    in_specs = [
        pl.BlockSpec((rows, N_HEADS * DK), rmap(0)),
        pl.BlockSpec((rows, N_HEADS * DK), rmap(1)),
        pl.BlockSpec((rows, N_HEADS * DV), rmap(1)),
        pl.BlockSpec((rows, N_HEADS * DV), rmap(2)),
        pl.BlockSpec((rows, QK_CH), rmap(3)),
        pl.BlockSpec((rows, N_HEADS * DV), rmap(4)),
        pl.BlockSpec((rows, N_HEADS * DV), rmap(5)),
        pl.BlockSpec((rows, LANES), rmap(0)),
        pl.BlockSpec((1, N_HEADS, DK, DV), smap(0, 0, 0)),
        pl.BlockSpec((1, N_HEADS, DK, DV), smap(0, 0, 0)),
        pl.BlockSpec((1, N_HEADS, DK), smap(0, 0)),
        pl.BlockSpec((1, 1, N_HEADS), smap(0, 0)),
        pl.BlockSpec((1, CONV_W - 1, QK_CH), smap(0, 0)),
        pl.BlockSpec(wz_hi.shape, wmap), pl.BlockSpec(wz_lo.shape, wmap),
        pl.BlockSpec(bz.shape, wmap), pl.BlockSpec(g_gla.shape, wmap),
        pl.BlockSpec(wc.shape, wmap), pl.BlockSpec(bc.shape, wmap),
        pl.BlockSpec(bsm.shape, wmap), pl.BlockSpec(g_ml.shape, wmap),
    ]
    n_rows = n_streams * t_len
    out_shape = [
        jax.ShapeDtypeStruct((n_rows, D_MODEL), BF16),
        jax.ShapeDtypeStruct((n_streams, N_HEADS, DK, DV), F32),
        jax.ShapeDtypeStruct((n_streams, N_HEADS, DK, DV), F32),
        jax.ShapeDtypeStruct((n_streams, N_HEADS, DK), F32),
        jax.ShapeDtypeStruct((n_streams, 1, LANES), F32),
        jax.ShapeDtypeStruct((n_streams, CONV_W - 1, QK_CH), F32),
    ]
    out_specs = [
        pl.BlockSpec((rows, D_MODEL), lambda s, b: (s * n_blk + b, 0)),
        pl.BlockSpec((1, N_HEADS, DK, DV), smap(0, 0, 0)),
        pl.BlockSpec((1, N_HEADS, DK, DV), smap(0, 0, 0)),
        pl.BlockSpec((1, N_HEADS, DK), smap(0, 0)),
        pl.BlockSpec((1, 1, LANES), smap(0, 0)),
        pl.BlockSpec((1, CONV_W - 1, QK_CH), smap(0, 0)),
    ]
    scratch = [
        pltpu.VMEM((N_HEADS, DV, DK), F32),
        pltpu.VMEM((N_HEADS, DK, DV), F32),
        pltpu.VMEM((N_HEADS, 1, DK), F32),
        pltpu.VMEM((N_HEADS, 1, LANES), F32),
        pltpu.VMEM((rows + 8, QK_CH), F32),
        pltpu.VMEM((rows, N_HEADS * DK), F32),
    ]
    return pl.pallas_call(
        functools.partial(_mixer_kernel, rows),
        grid=(n_streams, n_blk),
        in_specs=in_specs, out_specs=out_specs, out_shape=out_shape,
        scratch_shapes=scratch,
        compiler_params=pltpu.CompilerParams(
            dimension_semantics=("arbitrary", "arbitrary"), vmem_limit_bytes=VMEM_LIMIT),
        name=f"mixers_r{rows}",
    )(p_big, p_big, p_big, p_big, p_big, p_big, p_big, p_small, s0, c0, n0, m0, cv0, *wts)


def _pack_rows(ref, val):
    word = pltpu.pack_elementwise([val[:, :HALF], val[:, HALF:]], packed_dtype=BF16)
    rows = val.shape[0]
    for j in range(REC_ROWS):
        ref[pl.ds(j, rows, stride=REC_ROWS), :] = word[:, j * LANES:(j + 1) * LANES]


def _unpack_rows(ref, rows):
    word = jnp.concatenate(
        [ref[pl.ds(j, rows, stride=REC_ROWS), :] for j in range(REC_ROWS)], axis=1)
    lo = pltpu.unpack_elementwise(word, index=0, packed_dtype=BF16, unpacked_dtype=F32)
    hi = pltpu.unpack_elementwise(word, index=1, packed_dtype=BF16, unpacked_dtype=F32)
    return lo, hi


def _outproj_kernel(n_prompt_tiles, op_ref, os_ref, xp_ref, xs_ref, w_ref, g_ref, wr_ref,
                    x1_ref, xn_ref, route_ref, code_ref, cnt_ref, cnt_s):
    i = pl.program_id(0)

    @pl.when(i == 0)
    def _():
        cnt_s[...] = jnp.zeros_like(cnt_s)

    def body(o_ref, x_ref):
        x1 = x_ref[...] + _dot(o_ref[...], w_ref[...])
        x1_ref[...] = x1
        xn = _rmsnorm(x1, g_ref[...])
        _pack_rows(xn_ref, xn)
        xn_hi, xn_mid, xn_lo = _split3(xn)
        pa = _dot(xn_hi, wr_ref[...])
        pb = _dot(xn_mid, wr_ref[:, :2 * LANES])
        pc = _dot(xn_lo, wr_ref[:, :LANES])
        lg = (((pa[:, 2 * LANES:] + pc + pb[:, LANES:]) + (pa[:, LANES:2 * LANES] + pb[:, :LANES]))
              + pa[:, :LANES])
        lane = lax.broadcasted_iota(jnp.int32, lg.shape, 1).astype(F32)
        neg = -jnp.inf
        lgm = jnp.where(lane < N_GROUPS, lg, neg)
        mg = jnp.max(lgm, axis=1, keepdims=True)
        g_idx = jnp.min(jnp.where(lgm == mg, lane, float(LANES)), axis=1, keepdims=True)
        g_val = 1.0 / jnp.sum(jnp.where(lane < N_GROUPS, jnp.exp(lg - mg), 0.0), axis=1, keepdims=True)
        e0 = ROUTER_E0 + EXPERTS_PER_GROUP * g_idx
        le = jnp.where((lane >= e0) & (lane < e0 + EXPERTS_PER_GROUP), lg, neg)
        v1 = jnp.max(le, axis=1, keepdims=True)
        i1 = jnp.min(jnp.where(le == v1, lane, float(LANES)), axis=1, keepdims=True)
        le2 = jnp.where(lane == i1, neg, le)
        v2 = jnp.max(le2, axis=1, keepdims=True)
        i2 = jnp.min(jnp.where(le2 == v2, lane, float(LANES)), axis=1, keepdims=True)
        t = jnp.exp(v2 - v1)
        w1 = g_val / (1.0 + t)
        w2 = g_val * t / (1.0 + t)
        oh1 = lane == i1
        oh2 = lane == i2
        hot = jnp.where(oh1 | oh2, 1.0, 0.0)
        r_i = lax.broadcasted_iota(jnp.int32, (ROW_TILE, ROW_TILE), 0)
        c_i = lax.broadcasted_iota(jnp.int32, (ROW_TILE, ROW_TILE), 1)
        before = jnp.where(c_i < r_i, 1.0, 0.0).astype(BF16)
        seen = _dot(before, hot.astype(BF16)) + cnt_s[...]
        rank1 = jnp.sum(jnp.where(oh1, seen, 0.0), axis=1, keepdims=True)
        rank2 = jnp.sum(jnp.where(oh2, seen, 0.0), axis=1, keepdims=True)
        cnt_s[...] += jnp.sum(hot, axis=0, keepdims=True)
        code1 = (i1 - ROUTER_E0) * float(SLOT_CODE) + rank1
        code2 = (i2 - ROUTER_E0) * float(SLOT_CODE) + rank2
        route = jnp.zeros_like(lg)
        for k, col in ((0, code1), (1, code2), (4, w1), (5, w2)):
            route = jnp.where(lane == k, col, route)
        route_ref[...] = route
        code_ref[0] = route.T[0:8, :].astype(jnp.int32)

    @pl.when(i < n_prompt_tiles)
    def _():
        body(op_ref, xp_ref)

    @pl.when(i >= n_prompt_tiles)
    def _():
        body(os_ref, xs_ref)

    cnt_ref[...] = cnt_s[...]


def _outproj(o_p, o_s, xp, xs, w_out, g, wr):
    n_p, n_s = xp.shape[0], xs.shape[0]
    npt = n_p // ROW_TILE
    n = n_p + n_s

    def pmap(i):
        return (jnp.minimum(i, npt - 1), 0)

    def cmap(i):
        return (0, 0)

    def omap(i):
        return (i, 0)

    return pl.pallas_call(
        functools.partial(_outproj_kernel, npt),
        grid=(npt + 1,),
        in_specs=[
            pl.BlockSpec((ROW_TILE, D_MODEL), pmap), pl.BlockSpec((ROW_TILE, D_MODEL), cmap),
            pl.BlockSpec((ROW_TILE, D_MODEL), pmap), pl.BlockSpec((ROW_TILE, D_MODEL), cmap),
            pl.BlockSpec((D_MODEL, D_MODEL), cmap), pl.BlockSpec((1, D_MODEL), cmap),
            pl.BlockSpec((D_MODEL, 3 * LANES), cmap),
        ],
        out_specs=[
            pl.BlockSpec((ROW_TILE, D_MODEL), omap), pl.BlockSpec((ROW_TILE * REC_ROWS, LANES), omap),
            pl.BlockSpec((ROW_TILE, LANES), omap), pl.BlockSpec((1, 8, ROW_TILE), lambda i: (i, 0, 0)),
            pl.BlockSpec((1, LANES), cmap),
        ],
        out_shape=[
            jax.ShapeDtypeStruct((n, D_MODEL), F32),
            jax.ShapeDtypeStruct((n * REC_ROWS, LANES), jnp.uint32),
            jax.ShapeDtypeStruct((n, LANES), F32),
            jax.ShapeDtypeStruct((npt + 1, 8, ROW_TILE), jnp.int32),
            jax.ShapeDtypeStruct((1, LANES), F32),
        ],
        scratch_shapes=[pltpu.VMEM((1, LANES), F32)],
        compiler_params=pltpu.CompilerParams(
            dimension_semantics=("arbitrary",), vmem_limit_bytes=VMEM_LIMIT),
        name="outproj_router",
    )(o_p, o_s, xp, xs, w_out, g, wr)


def _rec(ref, idx):
    return ref.at[pl.ds(pl.multiple_of(idx * REC_ROWS, REC_ROWS), REC_ROWS)]


def _zero_records_kernel(o_ref):
    z = jnp.zeros(o_ref.shape, F32)
    o_ref[...] = pltpu.pack_elementwise([z, z], packed_dtype=BF16)


def _slot_row(code, row0_ref):
    return row0_ref[lax.shift_right_logical(code, SLOT_CODE_BITS)] + (code & (SLOT_CODE - 1))


def _dispatch_kernel(t_max, c1_ref, c2_ref, row0_ref, padlo_ref, padn_ref, nt_ref, xn_ref, xs_ref,
                     zero_s, sem, zsem):
    step = pl.program_id(0)
    base = step * ROW_TILE

    @pl.when(step == 0)
    def _():
        _zero_records_kernel(zero_s)

        def pad_copy(row):
            return pltpu.make_async_copy(_rec(zero_s, 0), _rec(xs_ref, row), zsem)

        def per_expert(e, c):
            lo = padlo_ref[e]
            n = padn_ref[e]
            lax.fori_loop(0, n, lambda r, cc: (pad_copy(lo + r).start(), cc)[1], 0)
            lax.fori_loop(0, n, lambda r, cc: (pad_copy(lo + r).wait(), cc)[1], 0)
            return c

        lax.fori_loop(0, N_EXPERTS, per_expert, 0)

        def per_unused_tile(t, c):
            rows = EXP_TILE * REC_ROWS
            cp = pltpu.make_async_copy(zero_s, xs_ref.at[pl.ds(pl.multiple_of(t * rows, rows), rows)], zsem)
            cp.start()
            cp.wait()
            return c

        lax.fori_loop(nt_ref[0], t_max, per_unused_tile, 0)

    def copies(r):
        src = _rec(xn_ref, r)
        return (pltpu.make_async_copy(src, _rec(xs_ref, _slot_row(c1_ref[base + r], row0_ref)), sem),
                pltpu.make_async_copy(src, _rec(xs_ref, _slot_row(c2_ref[base + r], row0_ref)), sem))

    def start(g, c):
        for u in range(DMA_UNROLL):
            for cp in copies(g * DMA_UNROLL + u):
                cp.start()
        return c

    def wait(g, c):
        for u in range(DMA_UNROLL):
            for cp in copies(g * DMA_UNROLL + u):
                cp.wait()
        return c

    lax.fori_loop(0, ROW_TILE // DMA_UNROLL, start, 0)
    lax.fori_loop(0, ROW_TILE // DMA_UNROLL, wait, 0)


def _dispatch(code1, code2, row0, pad_lo, pad_n, n_tiles, xn_rec, t_max):
    n = code1.shape[0]
    return pl.pallas_call(
        functools.partial(_dispatch_kernel, t_max),
        grid_spec=pltpu.PrefetchScalarGridSpec(
            num_scalar_prefetch=6,
            grid=(n // ROW_TILE,),
            in_specs=[pl.BlockSpec((ROW_TILE * REC_ROWS, LANES), lambda i, *_: (i, 0))],
            out_specs=pl.BlockSpec(memory_space=pl.ANY),
            scratch_shapes=[pltpu.VMEM((EXP_TILE * REC_ROWS, LANES), jnp.uint32),
                            pltpu.SemaphoreType.DMA, pltpu.SemaphoreType.DMA],
        ),
        out_shape=jax.ShapeDtypeStruct((t_max * EXP_TILE * REC_ROWS, LANES), jnp.uint32),
        compiler_params=pltpu.CompilerParams(
            dimension_semantics=("arbitrary",), vmem_limit_bytes=VMEM_LIMIT),
        name="moe_dispatch",
    )(code1, code2, row0, pad_lo, pad_n, n_tiles, xn_rec)


def _experts_kernel(te_ref, nt_ref, xs_ref, wg_ref, wu_ref, wd_ref, ys_ref, wg_s, wu_s, wd_s):
    t = pl.program_id(0)
    prev = te_ref[jnp.maximum(t - 1, 0)]

    @pl.when((t == 0) | (te_ref[t] != prev))
    def _():
        wg_s[...] = wg_ref[0].astype(BF16)
        wu_s[...] = wu_ref[0].astype(BF16)
        wd_s[...] = wd_ref[0].astype(BF16)

    @pl.when(t < nt_ref[0])
    def _():
        lo, hi = _unpack_rows(xs_ref, EXP_TILE)
        x = jnp.concatenate([lo.astype(BF16), hi.astype(BF16)], axis=1)
        hg = _dot(x, wg_s[...])
        hu = _dot(x, wu_s[...])
        act = hg * _sigmoid(hg) * hu
        _pack_rows(ys_ref, _dot(act.astype(BF16), wd_s[...]))

    @pl.when(t >= nt_ref[0])
    def _():
        _zero_records_kernel(ys_ref)


def _experts(tile_expert, n_tiles, xs, wg, wu, wd):
    t_max = tile_expert.shape[0]

    def tmap(t, te, nt):
        return (jnp.minimum(t, nt[0] - 1), 0)

    def wmap(t, te, nt):
        return (te[t], 0, 0)

    return pl.pallas_call(
        _experts_kernel,
        grid_spec=pltpu.PrefetchScalarGridSpec(
            num_scalar_prefetch=2,
            grid=(t_max,),
            in_specs=[pl.BlockSpec((EXP_TILE * REC_ROWS, LANES), tmap),
                      pl.BlockSpec((1, D_MODEL, D_EXPERT), wmap),
                      pl.BlockSpec((1, D_MODEL, D_EXPERT), wmap),
                      pl.BlockSpec((1, D_EXPERT, D_MODEL), wmap)],
            out_specs=pl.BlockSpec((EXP_TILE * REC_ROWS, LANES), lambda t, te, nt: (t, 0)),
            scratch_shapes=[pltpu.VMEM((D_MODEL, D_EXPERT), BF16), pltpu.VMEM((D_MODEL, D_EXPERT), BF16),
                            pltpu.VMEM((D_EXPERT, D_MODEL), BF16)],
        ),
        out_shape=jax.ShapeDtypeStruct(xs.shape, xs.dtype),
        compiler_params=pltpu.CompilerParams(
            dimension_semantics=("arbitrary",), vmem_limit_bytes=VMEM_LIMIT),
        name="moe_experts",
    )(tile_expert, n_tiles, xs, wg, wu, wd)


def _combine_kernel(n_prompt_tiles, c1_ref, c2_ref, row0_ref, x1_ref, route_ref, g_ref, ys_ref,
                    yp_ref, ysm_ref, a_s, b_s, sem):
    i = pl.program_id(0)
    n_steps = pl.num_programs(0)

    def copies(step, slot, r):
        tok = step * ROW_TILE + r
        return (pltpu.make_async_copy(_rec(ys_ref, _slot_row(c1_ref[tok], row0_ref)),
                                      _rec(a_s.at[slot], r), sem.at[slot]),
                pltpu.make_async_copy(_rec(ys_ref, _slot_row(c2_ref[tok], row0_ref)),
                                      _rec(b_s.at[slot], r), sem.at[slot]))

    def start_all(step, slot):
        def start(g, c):
            for u in range(DMA_UNROLL):
                for cp in copies(step, slot, g * DMA_UNROLL + u):
                    cp.start()
            return c
        lax.fori_loop(0, ROW_TILE // DMA_UNROLL, start, 0)

    def wait_all(step, slot):
        def wait(g, c):
            for u in range(DMA_UNROLL):
                for cp in copies(step, slot, g * DMA_UNROLL + u):
                    cp.wait()
            return c
        lax.fori_loop(0, ROW_TILE // DMA_UNROLL, wait, 0)

    slot = i % 2

    @pl.when(i == 0)
    def _():
        start_all(0, 0)

    @pl.when(i + 1 < n_steps)
    def _():
        start_all(i + 1, 1 - slot)

    wait_all(i, slot)
    a_lo, a_hi = _unpack_rows(a_s.at[slot], ROW_TILE)
    b_lo, b_hi = _unpack_rows(b_s.at[slot], ROW_TILE)
    route = route_ref[...]
    w1 = route[:, 4:5]
    w2 = route[:, 5:6]
    moe = jnp.concatenate([w1 * a_lo + w2 * b_lo, w1 * a_hi + w2 * b_hi], axis=1)
    y = _rmsnorm(x1_ref[...] + moe, g_ref[...])

    @pl.when(i < n_prompt_tiles)
    def _():
        yp_ref[...] = y

    @pl.when(i >= n_prompt_tiles)
    def _():
        ysm_ref[...] = y


def _combine(code1, code2, row0, x1, route, g, ys, n_p):
    n = x1.shape[0]
    npt = n_p // ROW_TILE

    def omap(i, *_):
        return (i, 0)

    return pl.pallas_call(
        functools.partial(_combine_kernel, npt),
        grid_spec=pltpu.PrefetchScalarGridSpec(
            num_scalar_prefetch=3,
            grid=(n // ROW_TILE,),
            in_specs=[pl.BlockSpec((ROW_TILE, D_MODEL), omap),
                      pl.BlockSpec((ROW_TILE, LANES), omap),
                      pl.BlockSpec((1, D_MODEL), lambda i, *_: (0, 0)),
                      pl.BlockSpec(memory_space=pl.ANY)],
            out_specs=[pl.BlockSpec((ROW_TILE, D_MODEL), lambda i, *_: (jnp.minimum(i, npt - 1), 0)),
                       pl.BlockSpec((ROW_TILE, D_MODEL), lambda i, *_: (jnp.maximum(i - npt, 0), 0))],
            scratch_shapes=[pltpu.VMEM((2, ROW_TILE * REC_ROWS, LANES), jnp.uint32),
                            pltpu.VMEM((2, ROW_TILE * REC_ROWS, LANES), jnp.uint32),
                            pltpu.SemaphoreType.DMA((2,))],
        ),
        out_shape=[jax.ShapeDtypeStruct((n_p, D_MODEL), F32),
                   jax.ShapeDtypeStruct((n - n_p, D_MODEL), F32)],
        compiler_params=pltpu.CompilerParams(
            dimension_semantics=("arbitrary",), vmem_limit_bytes=VMEM_LIMIT),
        name="moe_combine",
    )(code1, code2, row0, x1, route, g, ys)


def _pad_lanes(w):
    return jnp.pad(w, ((0, 0), (0, LANES - w.shape[1])))


def _moe_plan(counts, n_tokens):
    cnt = counts[0, ROUTER_E0:ROUTER_E0 + N_EXPERTS].astype(jnp.int32)
    tiles = (cnt + EXP_TILE - 1) // EXP_TILE
    tile_end = jnp.cumsum(tiles)
    row0 = (tile_end - tiles) * EXP_TILE
    t_max = 2 * n_tokens // EXP_TILE + N_EXPERTS
    tile_ids = jnp.arange(t_max, dtype=jnp.int32)
    tile_expert = jnp.minimum(
        jnp.sum((tile_ids[:, None] >= tile_end[None, :]).astype(jnp.int32), axis=1), N_EXPERTS - 1)
    return row0, row0 + cnt, tiles * EXP_TILE - cnt, tile_expert, tile_end[-1:].astype(jnp.int32), t_max


def kernel(x_prompt, x_sample, state_gla_S, state_mlstm_C, state_mlstm_n, state_mlstm_m, cache_mlstm_conv, g_mix_norm, w_in, w_gla_gate_up, b_gla_gate_up, g_gla_out, w_mlstm_conv, b_mlstm_conv, b_mlstm_i, b_mlstm_f, g_mlstm_out, w_out, g_ffn_norm, w_router_group, w_router_expert, w_exp_gate, w_exp_up, w_exp_down, g_final):
    depth = w_in.shape[0]
    assert depth == 1
    bp, t_p, _ = x_prompt.shape
    bs, t_s, _ = x_sample.shape
    assert bp == 1
    xp = x_prompt.reshape(bp * t_p, D_MODEL)
    xs = x_sample.reshape(bs * t_s, D_MODEL)
    n_p = xp.shape[0]

    w = w_in[0]
    c_gz = 3072
    c_mqk = c_gz + GLA_GATE_RANK
    c_mi = c_mqk + 3072
    w_big = jnp.concatenate([w[:, :c_gz].astype(BF16), w[:, c_mqk:c_mi].astype(BF16)], axis=1)
    ws_hi, ws_lo = _split2(_pad_lanes(jnp.concatenate([w[:, c_gz:c_mqk], w[:, c_mi:]], axis=1)))
    wz = jnp.pad(w_gla_gate_up[0], ((0, LANES - GLA_GATE_RANK), (0, 0)))
    wz_hi, wz_lo = _split2(wz)
    bsm = _pad_lanes(jnp.concatenate(
        [jnp.zeros((1, GLA_GATE_RANK), F32), b_mlstm_i[0][None], b_mlstm_f[0][None]], axis=1))
    mix_w = (wz_hi, wz_lo, b_gla_gate_up[0][None], g_gla_out[0][None], w_mlstm_conv[0],
             b_mlstm_conv[0][None], bsm, g_mlstm_out[0][None])
    wr = jnp.concatenate(
        _split3(_pad_lanes(jnp.concatenate([w_router_group[0], w_router_expert[0]], axis=1))), axis=1)

    p_big, p_small = _inproj(xp, xs, g_mix_norm[0][None], w_big, ws_hi, ws_lo)

    dt = x_prompt.dtype
    z_s = jnp.zeros((bp, N_HEADS, DK, DV), dt)
    z_n = jnp.zeros((bp, N_HEADS, DK), dt)
    z_m = jnp.zeros((bp, 1, N_HEADS), dt)
    z_cv = jnp.zeros((bp, CONV_W - 1, QK_CH), dt)
    o_p, p_S, p_C, p_n, p_m, p_cv = _mixers(
        p_big, p_small, 0, bp, t_p, MIX_ROWS, z_s, z_s, z_n, z_m, z_cv, mix_w)
    o_s, s_S, s_C, s_n, s_m, s_cv = _mixers(
        p_big, p_small, n_p, bs, t_s, t_s, state_gla_S[0], state_mlstm_C[0], state_mlstm_n[0],
        state_mlstm_m[0][:, None, :], cache_mlstm_conv[0], mix_w)

    x1, xn_rec, route, codes, counts = _outproj(
        o_p, o_s, xp, xs, w_out[0].astype(BF16), g_ffn_norm[0][None], wr)
    code1 = codes[:, 0, :].reshape(-1)
    code2 = codes[:, 1, :].reshape(-1)
    row0, pad_lo, pad_n, tile_expert, n_tiles, t_max = _moe_plan(counts, x1.shape[0])
    xs_rec = _dispatch(code1, code2, row0, pad_lo, pad_n, n_tiles, xn_rec, t_max)
    ys_rec = _experts(tile_expert, n_tiles, xs_rec, w_exp_gate[0], w_exp_up[0], w_exp_down[0])
    y_p, y_s = _combine(code1, code2, row0, x1, route, g_final[None], ys_rec, n_p)

    return (y_p.reshape(x_prompt.shape), y_s.reshape(x_sample.shape),
            p_S[None], p_C[None], p_n[None], p_m[:, 0, :N_HEADS][None], p_cv[None],
            s_S[None], s_C[None], s_n[None], s_m[:, 0, :N_HEADS][None], s_cv[None])
```

```python
import functools

import jax
import jax.numpy as jnp
from jax import lax
from jax.experimental import pallas as pl
from jax.experimental.pallas import tpu as pltpu

F32 = jnp.float32
BF16 = jnp.bfloat16

D_MODEL = 2048
N_HEADS = 4
DK = 128
DV = 256
GLA_GATE_RANK = 16
GLA_GATE_NORM = 16.0
CONV_W = 4
QK_CH = 2 * N_HEADS * DK
N_GROUPS = 4
EXPERTS_PER_GROUP = 8
N_EXPERTS = N_GROUPS * EXPERTS_PER_GROUP
D_EXPERT = 256
EPS = 1e-6

LANES = 128
GLA_SUB = 16
GLA_SAFE_DECAY = 60.0
MIX_ROWS = 128
ROW_TILE = 512
PROJ_BIG = 6 * 1024
PROJ_CHUNK = 1024
LANE_I = GLA_GATE_RANK
LANE_F = GLA_GATE_RANK + N_HEADS
ROUTER_E0 = N_GROUPS
HALF = D_MODEL // 2
REC_ROWS = HALF // LANES
EXP_TILE = 256
DMA_UNROLL = 8
SLOT_CODE_BITS = 16
SLOT_CODE = 1 << SLOT_CODE_BITS

VMEM_LIMIT = 56 * 1024 * 1024


def _dot(a, b):
    return jnp.dot(a, b, preferred_element_type=F32)


def _dot_nt(a, b):
    return lax.dot_general(a, b, (((1,), (1,)), ((), ())), preferred_element_type=F32)


def _dot_tn(a, b):
    return lax.dot_general(a, b, (((0,), (0,)), ((), ())), preferred_element_type=F32)


def _split2(x):
    hi = x.astype(BF16)
    lo = (x - hi.astype(F32)).astype(BF16)
    return hi, lo


def _split3(x):
    hi = x.astype(BF16)
    r = x - hi.astype(F32)
    mid = r.astype(BF16)
    lo = (r - mid.astype(F32)).astype(BF16)
    return hi, mid, lo


def _dot_exact_lhs(m_bf16, x):
    hi, mid, lo = _split3(x)
    return _dot(m_bf16, hi) + _dot(m_bf16, mid) + _dot(m_bf16, lo)


def _dot_hilo(a_hi, a_lo, b_hi, b_lo):
    return _dot(a_hi, b_hi) + _dot(a_lo, b_hi) + _dot(a_hi, b_lo)


def _log_sigmoid(z):
    return jnp.minimum(z, 0.0) - jnp.log1p(jnp.exp(-jnp.abs(z)))


def _sigmoid(z):
    return 1.0 / (1.0 + jnp.exp(-z))


def _rmsnorm(x, g):
    return x * lax.rsqrt(jnp.mean(x * x, axis=-1, keepdims=True) + EPS) * g


def _inproj_kernel(n_prompt_tiles, xp_ref, xs_ref, g_ref, w_ref, ws_hi_ref, ws_lo_ref,
                   p_ref, ps_ref):
    i = pl.program_id(0)

    def body(x_ref):
        y = _rmsnorm(x_ref[...], g_ref[...])
        y_hi, y_lo = _split2(y)
        ps_ref[...] = _dot_hilo(y_hi, y_lo, ws_hi_ref[...], ws_lo_ref[...])
        for c in range(PROJ_BIG // PROJ_CHUNK):
            cols = slice(c * PROJ_CHUNK, (c + 1) * PROJ_CHUNK)
            p_ref[:, cols] = _dot(y_hi, w_ref[:, cols]).astype(p_ref.dtype)

    @pl.when(i < n_prompt_tiles)
    def _():
        body(xp_ref)

    @pl.when(i >= n_prompt_tiles)
    def _():
        body(xs_ref)


def _inproj(xp, xs, g, w_big, ws_hi, ws_lo):
    n_p, n_s = xp.shape[0], xs.shape[0]
    assert n_p % ROW_TILE == 0 and n_s == ROW_TILE
    npt = n_p // ROW_TILE
    n = n_p + n_s
    once = pl.Buffered(1)
    return pl.pallas_call(
        functools.partial(_inproj_kernel, npt),
        grid=(npt + 1,),
        in_specs=[
            pl.BlockSpec((ROW_TILE, D_MODEL), lambda i: (jnp.minimum(i, npt - 1), 0)),
            pl.BlockSpec((ROW_TILE, D_MODEL), lambda i: (0, 0)),
            pl.BlockSpec((1, D_MODEL), lambda i: (0, 0)),
            pl.BlockSpec((D_MODEL, PROJ_BIG), lambda i: (0, 0), pipeline_mode=once),
            pl.BlockSpec((D_MODEL, LANES), lambda i: (0, 0), pipeline_mode=once),
            pl.BlockSpec((D_MODEL, LANES), lambda i: (0, 0), pipeline_mode=once),
        ],
        out_specs=[
            pl.BlockSpec((ROW_TILE, PROJ_BIG), lambda i: (i, 0)),
            pl.BlockSpec((ROW_TILE, LANES), lambda i: (i, 0)),
        ],
        out_shape=[
            jax.ShapeDtypeStruct((n, PROJ_BIG), BF16),
            jax.ShapeDtypeStruct((n, LANES), F32),
        ],
        compiler_params=pltpu.CompilerParams(
            dimension_semantics=("arbitrary",), vmem_limit_bytes=VMEM_LIMIT),
        name="inproj",
    )(xp, xs, g, w_big, ws_hi, ws_lo)


def _mixer_kernel(rows, gq_ref, gk_ref, gv_ref, gg_ref, mqk_ref, mv_ref, mo_ref, sm_ref,
                  s0_ref, c0_ref, n0_ref, m0_ref, cv0_ref,
                  wz_hi_ref, wz_lo_ref, bz_ref, g_gla_ref, wc_ref, bc_ref, bsm_ref, g_ml_ref,
                  o_ref, s_out_ref, c_out_ref, n_out_ref, m_out_ref, cv_out_ref,
                  st_s, c_s, n_s, m_s, cb_s, b_s):
    blk = pl.program_id(1)
    n_blk = pl.num_programs(1)

    @pl.when(blk == 0)
    def _():
        for h in range(N_HEADS):
            st_s[h] = s0_ref[0, h].T
            c_s[h] = c0_ref[0, h]
            n_s[h] = n0_ref[0, h:h + 1, :]
            m_s[h] = jnp.broadcast_to(m0_ref[0, :, h:h + 1], (1, LANES))
        cb_s[0:8, :] = jnp.zeros((8, QK_CH), F32)
        cb_s[8 - (CONV_W - 1):8, :] = cv0_ref[0]

    small = sm_ref[...]
    row_i = lax.broadcasted_iota(jnp.int32, (rows, rows), 0)
    col_i = lax.broadcasted_iota(jnp.int32, (rows, rows), 1)
    causal = col_i <= row_i

    sm_hi, sm_lo = _split2(small)
    z = _dot_hilo(sm_hi, sm_lo, wz_hi_ref[...], wz_lo_ref[...]) + bz_ref[...]
    log_a = _log_sigmoid(z) * (1.0 / GLA_GATE_NORM)
    tri = jnp.where(causal, 1.0, 0.0).astype(BF16)
    b_blk = _dot_exact_lhs(tri, log_a)
    b_last = b_blk[rows - 1:rows, :]
    factorable = jnp.min(b_last) >= -GLA_SAFE_DECAY

    def gla_finish(o, vc, gate):
        y = o * lax.rsqrt(jnp.mean(o * o, axis=-1, keepdims=True) + EPS) * g_gla_ref[:, vc]
        return (y * (gate * _sigmoid(gate))).astype(o_ref.dtype)

    @pl.when(factorable)
    def _():
        for h in range(N_HEADS):
            kc = slice(h * DK, (h + 1) * DK)
            vc = slice(h * DV, (h + 1) * DV)
            bh = b_blk[:, kc]
            qh = gq_ref[:, kc].astype(F32) * (DK ** -0.5)
            kh = gk_ref[:, kc].astype(F32)
            vh = gv_ref[:, vc]
            st = st_s[h]
            q_dec = (qh * jnp.exp(bh)).astype(BF16)
            k_inv = (kh * jnp.exp(-bh)).astype(BF16)
            a = jnp.where(causal, _dot_nt(q_dec, k_inv), 0.0)
            o = _dot(a.astype(BF16), vh) + _dot_nt(q_dec, st.astype(BF16))
            bh_end = b_last[:, kc]
            k_dec = (kh * jnp.exp(bh_end - bh)).astype(BF16)
            st_s[h] = st * jnp.exp(bh_end) + _dot_tn(vh, k_dec)
            o_ref[:, vc] = gla_finish(o, vc, gg_ref[:, vc].astype(F32))

    sub_r = lax.broadcasted_iota(jnp.int32, (GLA_SUB, LANES), 0)
    sub_l = lax.broadcasted_iota(jnp.int32, (GLA_SUB, LANES), 1)

    def gla_sub(c, carry):
        r0 = pl.multiple_of(c * GLA_SUB, GLA_SUB)
        rs = pl.ds(r0, GLA_SUB)
        for h in range(N_HEADS):
            kc = slice(h * DK, (h + 1) * DK)
            vc = slice(h * DV, (h + 1) * DV)
            bh = b_s[rs, kc]
            qh = gq_ref[rs, kc].astype(F32) * (DK ** -0.5)
            kh = gk_ref[rs, kc].astype(F32)
            vh = gv_ref[rs, vc]
            st = st_s[h]
            o = _dot_nt((qh * jnp.exp(bh)).astype(BF16), st.astype(BF16))
            a = jnp.zeros((GLA_SUB, LANES), F32)
            for s in range(GLA_SUB):
                e = jnp.exp(jnp.minimum(bh - bh[s:s + 1, :], 0.0))
                col = jnp.sum(qh * (kh[s:s + 1, :] * e), axis=1, keepdims=True)
                a = jnp.where((sub_l == s) & (sub_r >= s), col, a)
            o = o + _dot(a[:, :GLA_SUB].astype(BF16), vh)
            b_end = bh[GLA_SUB - 1:GLA_SUB, :]
            k_dec = kh * jnp.exp(b_end - bh)
            st_s[h] = st * jnp.exp(b_end) + _dot_tn(vh, k_dec.astype(BF16))
            o_ref[rs, vc] = gla_finish(o, vc, gg_ref[rs, vc].astype(F32))
        return carry

    @pl.when(jnp.logical_not(factorable))
    def _():
        same_sub = (row_i // GLA_SUB) == (col_i // GLA_SUB)
        blk_tri = jnp.where(causal & same_sub, 1.0, 0.0).astype(BF16)
        b_s[...] = _dot_exact_lhs(blk_tri, log_a)
        lax.fori_loop(0, rows // GLA_SUB, gla_sub, 0)

    cb_s[8:8 + rows, :] = mqk_ref[...].astype(F32)
    conv = bc_ref[...]
    for j in range(CONV_W):
        conv = conv + cb_s[8 - (CONV_W - 1) + j:8 - (CONV_W - 1) + j + rows, :] * wc_ref[j:j + 1, :]
    cb_s[0:8, :] = cb_s[rows:rows + 8, :]
    qk = conv * _sigmoid(conv)

    pre = small + bsm_ref[...]
    log_f = _log_sigmoid(pre)
    f_cum =_dot_exact_lhs(tri, log_f)
    eye = jnp.where(row_i == col_i, 1.0, 0.0).astype(BF16)
    f_cum_t = sum(_dot_tn(p, eye) for p in _split3(f_cum))
    pre_t = sum(_dot_tn(p, eye) for p in _split3(pre))
    lane_1 = lax.broadcasted_iota(jnp.int32, (1, LANES), 1)
    m_new = jnp.zeros((1, LANES), F32)

    for h in range(N_HEADS):
        kc = slice(h * DK, (h + 1) * DK)
        vc = slice(h * DV, (h + 1) * DV)
        f_col = f_cum[:, LANE_F + h:LANE_F + h + 1]
        i_col = pre[:, LANE_I + h:LANE_I + h + 1]
        f_row = f_cum_t[LANE_F + h:LANE_F + h + 1, :]
        i_row = pre_t[LANE_I + h:LANE_I + h + 1, :]
        dm = jnp.where(causal, f_col - f_row + i_row, -jnp.inf)
        m_prev = m_s[h][:, 0:1]
        inter = f_col + m_prev
        m_t = jnp.maximum(inter, jnp.max(dm, axis=1, keepdims=True))
        w = jnp.exp(dm - m_t)
        w_inter = jnp.exp(inter - m_t)
        q = qk[:, kc]
        k = qk[:, N_HEADS * DK + h * DK:N_HEADS * DK + (h + 1) * DK] * (DK ** -0.5)
        v = mv_ref[:, vc]
        q_b = q.astype(BF16)
        c_prev = c_s[h]
        n_prev = n_s[h]
        s_qk = _dot_nt(q_b, k.astype(BF16)) * w
        num = _dot(s_qk.astype(BF16), v) + w_inter * _dot(q_b, c_prev.astype(BF16))
        den = (jnp.sum(s_qk, axis=1, keepdims=True)
               + w_inter * jnp.sum(q * n_prev, axis=1, keepdims=True))
        hh = num / jnp.maximum(jnp.abs(den), jnp.exp(-m_t))
        m_end = m_t[rows - 1:rows, :]
        w_s = jnp.exp(f_col[rows - 1:rows, :] - f_col + i_col - m_end)
        dec = w_inter[rows - 1:rows, :]
        k_w = k * w_s
        c_s[h] = dec * c_prev + _dot_tn(k_w.astype(BF16), v)
        n_s[h] = dec * n_prev + jnp.sum(k_w, axis=0, keepdims=True)
        m_s[h] = jnp.broadcast_to(m_end, (1, LANES))
        m_new = jnp.where(lane_1 == h, m_end, m_new)
        y = hh * lax.rsqrt(jnp.mean(hh * hh, axis=-1, keepdims=True) + EPS) * g_ml_ref[:, vc]
        o_ref[:, N_HEADS * DV + h * DV:N_HEADS * DV + (h + 1) * DV] = (
            y * _sigmoid(mo_ref[:, vc].astype(F32))).astype(o_ref.dtype)

    @pl.when(blk == n_blk - 1)
    def _():
        for h in range(N_HEADS):
            s_out_ref[0, h] = st_s[h].T
            c_out_ref[0, h] = c_s[h]
            n_out_ref[0, h:h + 1, :] = n_s[h]
        m_out_ref[0] = m_new
        cv_out_ref[0] = cb_s[8 - (CONV_W - 1):8, :]


def _mixers(p_big, p_small, row0, n_streams, t_len, rows, s0, c0, n0, m0, cv0, wts):
    assert t_len % rows == 0 and row0 % rows == 0 and rows % GLA_SUB == 0
    n_blk = t_len // rows
    b0 = row0 // rows

    def rmap(col):
        return lambda s, b: (b0 + s * n_blk + b, col)

    def smap(*zeros):
        return lambda s, b: (s,) + zeros

    def wmap(s, b):
        return (0, 0)

    wz_hi, wz_lo, bz, g_gla, wc, bc, bsm, g_ml = wts
    in_specs = [
        pl.BlockSpec((rows, N_HEADS * DK), rmap(0)),
        pl.BlockSpec((rows, N_HEADS * DK), rmap(1)),
        pl.BlockSpec((rows, N_HEADS * DV), rmap(1)),
        pl.BlockSpec((rows, N_HEADS * DV), rmap(2)),
        pl.BlockSpec((rows, QK_CH), rmap(3)),
        pl.BlockSpec((rows, N_HEADS * DV), rmap(4)),
        pl.BlockSpec((rows, N_HEADS * DV), rmap(5)),
        pl.BlockSpec((rows, LANES), rmap(0)),
        pl.BlockSpec((1, N_HEADS, DK, DV), smap(0, 0, 0)),
        pl.BlockSpec((1, N_HEADS, DK, DV), smap(0, 0, 0)),
        pl.BlockSpec((1, N_HEADS, DK), smap(0, 0)),
        pl.BlockSpec((1, 1, N_HEADS), smap(0, 0)),
        pl.BlockSpec((1, CONV_W - 1, QK_CH), smap(0, 0)),
        pl.BlockSpec(wz_hi.shape, wmap), pl.BlockSpec(wz_lo.shape, wmap),
        pl.BlockSpec(bz.shape, wmap), pl.BlockSpec(g_gla.shape, wmap),
        pl.BlockSpec(wc.shape, wmap), pl.BlockSpec(bc.shape, wmap),
        pl.BlockSpec(bsm.shape, wmap), pl.BlockSpec(g_ml.shape, wmap),
    ]
    n_rows = n_streams * t_len
    out_shape = [
        jax.ShapeDtypeStruct((n_rows, D_MODEL), BF16),
        jax.ShapeDtypeStruct((n_streams, N_HEADS, DK, DV), F32),
        jax.ShapeDtypeStruct((n_streams, N_HEADS, DK, DV), F32),
        jax.ShapeDtypeStruct((n_streams, N_HEADS, DK), F32),
        jax.ShapeDtypeStruct((n_streams, 1, LANES), F32),
        jax.ShapeDtypeStruct((n_streams, CONV_W - 1, QK_CH), F32),
    ]
    out_specs = [
        pl.BlockSpec((rows, D_MODEL), lambda s, b: (s * n_blk + b, 0)),
        pl.BlockSpec((1, N_HEADS, DK, DV), smap(0, 0, 0)),
        pl.BlockSpec((1, N_HEADS, DK, DV), smap(0, 0, 0)),
        pl.BlockSpec((1, N_HEADS, DK), smap(0, 0)),
        pl.BlockSpec((1, 1, LANES), smap(0, 0)),
        pl.BlockSpec((1, CONV_W - 1, QK_CH), smap(0, 0)),
    ]
    scratch = [
        pltpu.VMEM((N_HEADS, DV, DK), F32),
        pltpu.VMEM((N_HEADS, DK, DV), F32),
        pltpu.VMEM((N_HEADS, 1, DK), F32),
        pltpu.VMEM((N_HEADS, 1, LANES), F32),
        pltpu.VMEM((rows + 8, QK_CH), F32),
        pltpu.VMEM((rows, N_HEADS * DK), F32),
    ]
    return pl.pallas_call(
        functools.partial(_mixer_kernel, rows),
        grid=(n_streams, n_blk),
        in_specs=in_specs, out_specs=out_specs, out_shape=out_shape,
        scratch_shapes=scratch,
        compiler_params=pltpu.CompilerParams(
            dimension_semantics=("arbitrary", "arbitrary"), vmem_limit_bytes=VMEM_LIMIT),
        name=f"mixers_r{rows}",
    )(p_big, p_big, p_big, p_big, p_big, p_big, p_big, p_small, s0, c0, n0, m0, cv0, *wts)


def _pack_rows(ref, val):
    word = pltpu.pack_elementwise([val[:, :HALF], val[:, HALF:]], packed_dtype=BF16)
    rows = val.shape[0]
    for j in range(REC_ROWS):
        ref[pl.ds(j, rows, stride=REC_ROWS), :] = word[:, j * LANES:(j + 1) * LANES]


def _unpack_rows(ref, rows):
    word = jnp.concatenate(
        [ref[pl.ds(j, rows, stride=REC_ROWS), :] for j in range(REC_ROWS)], axis=1)
    lo = pltpu.unpack_elementwise(word, index=0, packed_dtype=BF16, unpacked_dtype=F32)
    hi = pltpu.unpack_elementwise(word, index=1, packed_dtype=BF16, unpacked_dtype=F32)
    return lo, hi


def _outproj_kernel(n_prompt_tiles, op_ref, os_ref, xp_ref, xs_ref, w_ref, g_ref, wr_ref,
                    x1_ref, xn_ref, route_ref, code_ref, cnt_ref, cnt_s):
    i = pl.program_id(0)

    @pl.when(i == 0)
    def _():
        cnt_s[...] = jnp.zeros_like(cnt_s)

    def body(o_ref, x_ref):
        x1 = x_ref[...] + _dot(o_ref[...], w_ref[...])
        x1_ref[...] = x1
        xn = _rmsnorm(x1, g_ref[...])
        _pack_rows(xn_ref, xn)
        xn_hi, xn_mid, xn_lo = _split3(xn)
        pa = _dot(xn_hi, wr_ref[...])
        pb = _dot(xn_mid, wr_ref[:, :2 * LANES])
        pc = _dot(xn_lo, wr_ref[:, :LANES])
        lg = (((pa[:, 2 * LANES:] + pc + pb[:, LANES:]) + (pa[:, LANES:2 * LANES] + pb[:, :LANES]))
              + pa[:, :LANES])
        lane = lax.broadcasted_iota(jnp.int32, lg.shape, 1).astype(F32)
        neg = -jnp.inf
        lgm = jnp.where(lane < N_GROUPS, lg, neg)
        mg = jnp.max(lgm, axis=1, keepdims=True)
        g_idx = jnp.min(jnp.where(lgm == mg, lane, float(LANES)), axis=1, keepdims=True)
        g_val = 1.0 / jnp.sum(jnp.where(lane < N_GROUPS, jnp.exp(lg - mg), 0.0), axis=1, keepdims=True)
        e0 = ROUTER_E0 + EXPERTS_PER_GROUP * g_idx
        le = jnp.where((lane >= e0) & (lane < e0 + EXPERTS_PER_GROUP), lg, neg)
        v1 = jnp.max(le, axis=1, keepdims=True)
        i1 = jnp.min(jnp.where(le == v1, lane, float(LANES)), axis=1, keepdims=True)
        le2 = jnp.where(lane == i1, neg, le)
        v2 = jnp.max(le2, axis=1, keepdims=True)
        i2 = jnp.min(jnp.where(le2 == v2, lane, float(LANES)), axis=1, keepdims=True)
        t = jnp.exp(v2 - v1)
        w1 = g_val / (1.0 + t)
        w2 = g_val * t / (1.0 + t)
        oh1 = lane == i1
        oh2 = lane == i2
        hot = jnp.where(oh1 | oh2, 1.0, 0.0)
        r_i = lax.broadcasted_iota(jnp.int32, (ROW_TILE, ROW_TILE), 0)
        c_i = lax.broadcasted_iota(jnp.int32, (ROW_TILE, ROW_TILE), 1)
        before = jnp.where(c_i < r_i, 1.0, 0.0).astype(BF16)
        seen = _dot(before, hot.astype(BF16)) + cnt_s[...]
        rank1 = jnp.sum(jnp.where(oh1, seen, 0.0), axis=1, keepdims=True)
        rank2 = jnp.sum(jnp.where(oh2, seen, 0.0), axis=1, keepdims=True)
        cnt_s[...] += jnp.sum(hot, axis=0, keepdims=True)
        code1 = (i1 - ROUTER_E0) * float(SLOT_CODE) + rank1
        code2 = (i2 - ROUTER_E0) * float(SLOT_CODE) + rank2
        route = jnp.zeros_like(lg)
        for k, col in ((0, code1), (1, code2), (4, w1), (5, w2)):
            route = jnp.where(lane == k, col, route)
        route_ref[...] = route
        code_ref[0] = route.T[0:8, :].astype(jnp.int32)

    @pl.when(i < n_prompt_tiles)
    def _():
        body(op_ref, xp_ref)

    @pl.when(i >= n_prompt_tiles)
    def _():
        body(os_ref, xs_ref)

    cnt_ref[...] = cnt_s[...]


def _outproj(o_p, o_s, xp, xs, w_out, g, wr):
    n_p, n_s = xp.shape[0], xs.shape[0]
    npt = n_p // ROW_TILE
    n = n_p + n_s

    def pmap(i):
        return (jnp.minimum(i, npt - 1), 0)

    def cmap(i):
        return (0, 0)

    def omap(i):
        return (i, 0)

    return pl.pallas_call(
        functools.partial(_outproj_kernel, npt),
        grid=(npt + 1,),
        in_specs=[
            pl.BlockSpec((ROW_TILE, D_MODEL), pmap), pl.BlockSpec((ROW_TILE, D_MODEL), cmap),
            pl.BlockSpec((ROW_TILE, D_MODEL), pmap), pl.BlockSpec((ROW_TILE, D_MODEL), cmap),
            pl.BlockSpec((D_MODEL, D_MODEL), cmap), pl.BlockSpec((1, D_MODEL), cmap),
            pl.BlockSpec((D_MODEL, 3 * LANES), cmap),
        ],
        out_specs=[
            pl.BlockSpec((ROW_TILE, D_MODEL), omap), pl.BlockSpec((ROW_TILE * REC_ROWS, LANES), omap),
            pl.BlockSpec((ROW_TILE, LANES), omap), pl.BlockSpec((1, 8, ROW_TILE), lambda i: (i, 0, 0)),
            pl.BlockSpec((1, LANES), cmap),
        ],
        out_shape=[
            jax.ShapeDtypeStruct((n, D_MODEL), F32),
            jax.ShapeDtypeStruct((n * REC_ROWS, LANES), jnp.uint32),
            jax.ShapeDtypeStruct((n, LANES), F32),
            jax.ShapeDtypeStruct((npt + 1, 8, ROW_TILE), jnp.int32),
            jax.ShapeDtypeStruct((1, LANES), F32),
        ],
        scratch_shapes=[pltpu.VMEM((1, LANES), F32)],
        compiler_params=pltpu.CompilerParams(
            dimension_semantics=("arbitrary",), vmem_limit_bytes=VMEM_LIMIT),
        name="outproj_router",
    )(o_p, o_s, xp, xs, w_out, g, wr)


def _rec(ref, idx):
    return ref.at[pl.ds(pl.multiple_of(idx * REC_ROWS, REC_ROWS), REC_ROWS)]


def _zero_records_kernel(o_ref):
    z = jnp.zeros(o_ref.shape, F32)
    o_ref[...] = pltpu.pack_elementwise([z, z], packed_dtype=BF16)


def _slot_row(code, row0_ref):
    return row0_ref[lax.shift_right_logical(code, SLOT_CODE_BITS)] + (code & (SLOT_CODE - 1))


def _dispatch_kernel(t_max, c1_ref, c2_ref, row0_ref, padlo_ref, padn_ref, nt_ref, xn_ref, xs_ref,
                     zero_s, sem, zsem):
    step = pl.program_id(0)
    base = step * ROW_TILE

    def pad_copy(row):
        return pltpu.make_async_copy(_rec(zero_s, 0), _rec(xs_ref, row), zsem)

    def tile_copy(t):
        rows = EXP_TILE * REC_ROWS
        return pltpu.make_async_copy(zero_s, xs_ref.at[pl.ds(pl.multiple_of(t * rows, rows), rows)], zsem)

    def for_pad_rows(fn):
        def per_expert(e, c):
            lo = padlo_ref[e]
            lax.fori_loop(0, padn_ref[e], lambda r, cc: (fn(pad_copy(lo + r)), cc)[1], 0)
            return c
        lax.fori_loop(0, N_EXPERTS, per_expert, 0)
        lax.fori_loop(nt_ref[0], t_max, lambda t, c: (fn(tile_copy(t)), c)[1], 0)

    @pl.when(step == 0)
    def _():
        _zero_records_kernel(zero_s)
        for_pad_rows(lambda cp: cp.start(priority=1))

    def copies(r):
        src = _rec(xn_ref, r)
        return (pltpu.make_async_copy(src, _rec(xs_ref, _slot_row(c1_ref[base + r], row0_ref)), sem),
                pltpu.make_async_copy(src, _rec(xs_ref, _slot_row(c2_ref[base + r], row0_ref)), sem))

    def start(g, c):
        for u in range(DMA_UNROLL):
            for prio, cp in enumerate(copies(g * DMA_UNROLL + u)):
                cp.start(priority=prio)
        return c

    def wait(g, c):
        for u in range(DMA_UNROLL):
            for cp in copies(g * DMA_UNROLL + u):
                cp.wait()
        return c

    lax.fori_loop(0, ROW_TILE // DMA_UNROLL, start, 0)
    lax.fori_loop(0, ROW_TILE // DMA_UNROLL, wait, 0)

    @pl.when(step == 0)
    def _():
        for_pad_rows(lambda cp: cp.wait())


def _dispatch(code1, code2, row0, pad_lo, pad_n, n_tiles, xn_rec, t_max):
    n = code1.shape[0]
    return pl.pallas_call(
        functools.partial(_dispatch_kernel, t_max),
        grid_spec=pltpu.PrefetchScalarGridSpec(
            num_scalar_prefetch=6,
            grid=(n // ROW_TILE,),
            in_specs=[pl.BlockSpec((ROW_TILE * REC_ROWS, LANES), lambda i, *_: (i, 0))],
            out_specs=pl.BlockSpec(memory_space=pl.ANY),
            scratch_shapes=[pltpu.VMEM((EXP_TILE * REC_ROWS, LANES), jnp.uint32),
                            pltpu.SemaphoreType.DMA, pltpu.SemaphoreType.DMA],
        ),
        out_shape=jax.ShapeDtypeStruct((t_max * EXP_TILE * REC_ROWS, LANES), jnp.uint32),
        compiler_params=pltpu.CompilerParams(
            dimension_semantics=("arbitrary",), vmem_limit_bytes=VMEM_LIMIT),
        name="moe_dispatch",
    )(code1, code2, row0, pad_lo, pad_n, n_tiles, xn_rec)


def _experts_kernel(te_ref, nt_ref, xs_ref, wg_ref, wu_ref, wd_ref, ys_ref, wg_s, wu_s, wd_s):
    t = pl.program_id(0)
    prev = te_ref[jnp.maximum(t - 1, 0)]

    @pl.when((t == 0) | (te_ref[t] != prev))
    def _():
        wg_s[...] = wg_ref[0].astype(BF16)
        wu_s[...] = wu_ref[0].astype(BF16)
        wd_s[...] = wd_ref[0].astype(BF16)

    @pl.when(t < nt_ref[0])
    def _():
        lo, hi = _unpack_rows(xs_ref, EXP_TILE)
        x = jnp.concatenate([lo.astype(BF16), hi.astype(BF16)], axis=1)
        hg = _dot(x, wg_s[...])
        hu = _dot(x, wu_s[...])
        act = hg * _sigmoid(hg) * hu
        _pack_rows(ys_ref, _dot(act.astype(BF16), wd_s[...]))

    @pl.when(t >= nt_ref[0])
    def _():
        _zero_records_kernel(ys_ref)


def _experts(tile_expert, n_tiles, xs, wg, wu, wd):
    t_max = tile_expert.shape[0]

    def tmap(t, te, nt):
        return (jnp.minimum(t, nt[0] - 1), 0)

    def wmap(t, te, nt):
        return (te[t], 0, 0)

    return pl.pallas_call(
        _experts_kernel,
        grid_spec=pltpu.PrefetchScalarGridSpec(
            num_scalar_prefetch=2,
            grid=(t_max,),
            in_specs=[pl.BlockSpec((EXP_TILE * REC_ROWS, LANES), tmap),
                      pl.BlockSpec((1, D_MODEL, D_EXPERT), wmap),
                      pl.BlockSpec((1, D_MODEL, D_EXPERT), wmap),
                      pl.BlockSpec((1, D_EXPERT, D_MODEL), wmap)],
            out_specs=pl.BlockSpec((EXP_TILE * REC_ROWS, LANES), lambda t, te, nt: (t, 0)),
            scratch_shapes=[pltpu.VMEM((D_MODEL, D_EXPERT), BF16), pltpu.VMEM((D_MODEL, D_EXPERT), BF16),
                            pltpu.VMEM((D_EXPERT, D_MODEL), BF16)],
        ),
        out_shape=jax.ShapeDtypeStruct(xs.shape, xs.dtype),
        compiler_params=pltpu.CompilerParams(
            dimension_semantics=("arbitrary",), vmem_limit_bytes=VMEM_LIMIT),
        name="moe_experts",
    )(tile_expert, n_tiles, xs, wg, wu, wd)


def _combine_kernel(n_prompt_tiles, c1_ref, c2_ref, row0_ref, x1_ref, route_ref, g_ref, ys_ref,
                    yp_ref, ysm_ref, a_s, b_s, sem):
    i = pl.program_id(0)
    n_steps = pl.num_programs(0)

    def copies(step, slot, r):
        tok = step * ROW_TILE + r
        return (pltpu.make_async_copy(_rec(ys_ref, _slot_row(c1_ref[tok], row0_ref)),
                                      _rec(a_s.at[slot], r), sem.at[slot]),
                pltpu.make_async_copy(_rec(ys_ref, _slot_row(c2_ref[tok], row0_ref)),
                                      _rec(b_s.at[slot], r), sem.at[slot]))

    def start_all(step, slot):
        def start(g, c):
            for u in range(DMA_UNROLL):
                for prio, cp in enumerate(copies(step, slot, g * DMA_UNROLL + u)):
                    cp.start(priority=prio)
            return c
        lax.fori_loop(0, ROW_TILE // DMA_UNROLL, start, 0)

    def wait_all(step, slot):
        def wait(g, c):
            for u in range(DMA_UNROLL):
                for cp in copies(step, slot, g * DMA_UNROLL + u):
                    cp.wait()
            return c
        lax.fori_loop(0, ROW_TILE // DMA_UNROLL, wait, 0)

    slot = i % 2

    @pl.when(i == 0)
    def _():
        start_all(0, 0)

    @pl.when(i + 1 < n_steps)
    def _():
        start_all(i + 1, 1 - slot)

    wait_all(i, slot)
    a_lo, a_hi = _unpack_rows(a_s.at[slot], ROW_TILE)
    b_lo, b_hi = _unpack_rows(b_s.at[slot], ROW_TILE)
    route = route_ref[...]
    w1 = route[:, 4:5]
    w2 = route[:, 5:6]
    moe = jnp.concatenate([w1 * a_lo + w2 * b_lo, w1 * a_hi + w2 * b_hi], axis=1)
    y = _rmsnorm(x1_ref[...] + moe, g_ref[...])

    @pl.when(i < n_prompt_tiles)
    def _():
        yp_ref[...] = y

    @pl.when(i >= n_prompt_tiles)
    def _():
        ysm_ref[...] = y


def _combine(code1, code2, row0, x1, route, g, ys, n_p):
    n = x1.shape[0]
    npt = n_p // ROW_TILE

    def omap(i, *_):
        return (i, 0)

    return pl.pallas_call(
        functools.partial(_combine_kernel, npt),
        grid_spec=pltpu.PrefetchScalarGridSpec(
            num_scalar_prefetch=3,
            grid=(n // ROW_TILE,),
            in_specs=[pl.BlockSpec((ROW_TILE, D_MODEL), omap),
                      pl.BlockSpec((ROW_TILE, LANES), omap),
                      pl.BlockSpec((1, D_MODEL), lambda i, *_: (0, 0)),
                      pl.BlockSpec(memory_space=pl.ANY)],
            out_specs=[pl.BlockSpec((ROW_TILE, D_MODEL), lambda i, *_: (jnp.minimum(i, npt - 1), 0)),
                       pl.BlockSpec((ROW_TILE, D_MODEL), lambda i, *_: (jnp.maximum(i - npt, 0), 0))],
            scratch_shapes=[pltpu.VMEM((2, ROW_TILE * REC_ROWS, LANES), jnp.uint32),
                            pltpu.VMEM((2, ROW_TILE * REC_ROWS, LANES), jnp.uint32),
                            pltpu.SemaphoreType.DMA((2,))],
        ),
        out_shape=[jax.ShapeDtypeStruct((n_p, D_MODEL), F32),
                   jax.ShapeDtypeStruct((n - n_p, D_MODEL), F32)],
        compiler_params=pltpu.CompilerParams(
            dimension_semantics=("arbitrary",), vmem_limit_bytes=VMEM_LIMIT),
        name="moe_combine",
    )(code1, code2, row0, x1, route, g, ys)


def _pad_lanes(w):
    return jnp.pad(w, ((0, 0), (0, LANES - w.shape[1])))


def _moe_plan(counts, n_tokens):
    cnt = counts[0, ROUTER_E0:ROUTER_E0 + N_EXPERTS].astype(jnp.int32)
    tiles = (cnt + EXP_TILE - 1) // EXP_TILE
    tile_end = jnp.cumsum(tiles)
    row0 = (tile_end - tiles) * EXP_TILE
    t_max = 2 * n_tokens // EXP_TILE + N_EXPERTS
    tile_ids = jnp.arange(t_max, dtype=jnp.int32)
    tile_expert = jnp.minimum(
        jnp.sum((tile_ids[:, None] >= tile_end[None, :]).astype(jnp.int32), axis=1), N_EXPERTS - 1)
    return row0, row0 + cnt, tiles * EXP_TILE - cnt, tile_expert, tile_end[-1:].astype(jnp.int32), t_max


def kernel(x_prompt, x_sample, state_gla_S, state_mlstm_C, state_mlstm_n, state_mlstm_m, cache_mlstm_conv, g_mix_norm, w_in, w_gla_gate_up, b_gla_gate_up, g_gla_out, w_mlstm_conv, b_mlstm_conv, b_mlstm_i, b_mlstm_f, g_mlstm_out, w_out, g_ffn_norm, w_router_group, w_router_expert, w_exp_gate, w_exp_up, w_exp_down, g_final):
    depth = w_in.shape[0]
    assert depth == 1
    bp, t_p, _ = x_prompt.shape
    bs, t_s, _ = x_sample.shape
    assert bp == 1
    xp = x_prompt.reshape(bp * t_p, D_MODEL)
    xs = x_sample.reshape(bs * t_s, D_MODEL)
    n_p = xp.shape[0]

    w = w_in[0]
    c_gz = 3072
    c_mqk = c_gz + GLA_GATE_RANK
    c_mi = c_mqk + 3072
    w_big = jnp.concatenate([w[:, :c_gz].astype(BF16), w[:, c_mqk:c_mi].astype(BF16)], axis=1)
    ws_hi, ws_lo = _split2(_pad_lanes(jnp.concatenate([w[:, c_gz:c_mqk], w[:, c_mi:]], axis=1)))
    wz = jnp.pad(w_gla_gate_up[0], ((0, LANES - GLA_GATE_RANK), (0, 0)))
    wz_hi, wz_lo = _split2(wz)
    bsm = _pad_lanes(jnp.concatenate(
        [jnp.zeros((1, GLA_GATE_RANK), F32), b_mlstm_i[0][None], b_mlstm_f[0][None]], axis=1))
    mix_w = (wz_hi, wz_lo, b_gla_gate_up[0][None], g_gla_out[0][None], w_mlstm_conv[0],
             b_mlstm_conv[0][None], bsm, g_mlstm_out[0][None])
    wr = jnp.concatenate(
        _split3(_pad_lanes(jnp.concatenate([w_router_group[0], w_router_expert[0]], axis=1))), axis=1)

    p_big, p_small = _inproj(xp, xs, g_mix_norm[0][None], w_big, ws_hi, ws_lo)

    dt = x_prompt.dtype
    z_s = jnp.zeros((bp, N_HEADS, DK, DV), dt)
    z_n = jnp.zeros((bp, N_HEADS, DK), dt)
    z_m = jnp.zeros((bp, 1, N_HEADS), dt)
    z_cv = jnp.zeros((bp, CONV_W - 1, QK_CH), dt)
    o_p, p_S, p_C, p_n, p_m, p_cv = _mixers(
        p_big, p_small, 0, bp, t_p, MIX_ROWS, z_s, z_s, z_n, z_m, z_cv, mix_w)
    o_s, s_S, s_C, s_n, s_m, s_cv = _mixers(
        p_big, p_small, n_p, bs, t_s, t_s, state_gla_S[0], state_mlstm_C[0], state_mlstm_n[0],
        state_mlstm_m[0][:, None, :], cache_mlstm_conv[0], mix_w)

    x1, xn_rec, route, codes, counts = _outproj(
        o_p, o_s, xp, xs, w_out[0].astype(BF16), g_ffn_norm[0][None], wr)
    code1 = codes[:, 0, :].reshape(-1)
    code2 = codes[:, 1, :].reshape(-1)
    row0, pad_lo, pad_n, tile_expert, n_tiles, t_max = _moe_plan(counts, x1.shape[0])
    xs_rec = _dispatch(code1, code2, row0, pad_lo, pad_n, n_tiles, xn_rec, t_max)
    ys_rec = _experts(tile_expert, n_tiles, xs_rec, w_exp_gate[0], w_exp_up[0], w_exp_down[0])
    y_p, y_s = _combine(code1, code2, row0, x1, route, g_final[None], ys_rec, n_p)

    return (y_p.reshape(x_prompt.shape), y_s.reshape(x_sample.shape),
            p_S[None], p_C[None], p_n[None], p_m[:, 0, :N_HEADS][None], p_cv[None],
            s_S[None], s_C[None], s_n[None], s_m[:, 0, :N_HEADS][None], s_cv[None])
```

```python
import functools

import jax
import jax.numpy as jnp
from jax import lax
from jax.experimental import pallas as pl
from jax.experimental.pallas import tpu as pltpu

F32 = jnp.float32
BF16 = jnp.bfloat16

D_MODEL = 2048
N_HEADS = 4
DK = 128
DV = 256
GLA_GATE_RANK = 16
GLA_GATE_NORM = 16.0
CONV_W = 4
QK_CH = 2 * N_HEADS * DK
N_GROUPS = 4
EXPERTS_PER_GROUP = 8
N_EXPERTS = N_GROUPS * EXPERTS_PER_GROUP
D_EXPERT = 256
EPS = 1e-6

LANES = 128
GLA_SUB = 16
GLA_SAFE_DECAY = 60.0
MIX_ROWS = 128
ROW_TILE = 512
PROJ_BIG = 6 * 1024
PROJ_CHUNK = 1024
LANE_I = GLA_GATE_RANK
LANE_F = GLA_GATE_RANK + N_HEADS
ROUTER_E0 = N_GROUPS
HALF = D_MODEL // 2
REC_ROWS = HALF // LANES
EXP_TILE = 256
DMA_UNROLL = 8
SLOT_CODE_BITS = 16
SLOT_CODE = 1 << SLOT_CODE_BITS

VMEM_LIMIT = 56 * 1024 * 1024


def _dot(a, b):
    return jnp.dot(a, b, preferred_element_type=F32)


def _dot_nt(a, b):
    return lax.dot_general(a, b, (((1,), (1,)), ((), ())), preferred_element_type=F32)


def _dot_tn(a, b):
    return lax.dot_general(a, b, (((0,), (0,)), ((), ())), preferred_element_type=F32)


def _split2(x):
    hi = x.astype(BF16)
    lo = (x - hi.astype(F32)).astype(BF16)
    return hi, lo


def _split3(x):
    hi = x.astype(BF16)
    r = x - hi.astype(F32)
    mid = r.astype(BF16)
    lo = (r - mid.astype(F32)).astype(BF16)
    return hi, mid, lo


def _dot_exact_lhs(m_bf16, x):
    hi, mid, lo = _split3(x)
    return _dot(m_bf16, hi) + _dot(m_bf16, mid) + _dot(m_bf16, lo)


def _dot_hilo(a_hi, a_lo, b_hi, b_lo):
    return _dot(a_hi, b_hi) + _dot(a_lo, b_hi) + _dot(a_hi, b_lo)


def _hilo_cols(w):
    return jnp.concatenate(_split2(w), axis=1)


def _dot_hilo_cols(a_hi, a_lo, b_ref):
    p = _dot(a_hi, b_ref[...])
    return (p[:, LANES:] + _dot(a_lo, b_ref[:, :LANES])) + p[:, :LANES]


def _log_sigmoid(z):
    return jnp.minimum(z, 0.0) - jnp.log1p(jnp.exp(-jnp.abs(z)))


def _sigmoid(z):
    return 0.5 * jnp.tanh(0.5 * z) + 0.5


def _rmsnorm(x, g):
    return x * lax.rsqrt(jnp.mean(x * x, axis=-1, keepdims=True) + EPS) * g


def _pack_inproj_weight_kernel(c_gz, c_mqk, c_mi, w_ref, o_ref):
    o_ref[:, :c_gz] = w_ref[:, :c_gz].astype(o_ref.dtype)
    o_ref[:, c_gz:] = w_ref[:, c_mqk:c_mi].astype(o_ref.dtype)


def _pack_inproj_weight(w, c_gz, c_mqk, c_mi):
    rows = 256
    return pl.pallas_call(
        functools.partial(_pack_inproj_weight_kernel, c_gz, c_mqk, c_mi),
        grid=(w.shape[0] // rows,),
        in_specs=[pl.BlockSpec((rows, w.shape[1]), lambda i: (i, 0))],
        out_specs=pl.BlockSpec((rows, PROJ_BIG), lambda i: (i, 0)),
        out_shape=jax.ShapeDtypeStruct((w.shape[0], PROJ_BIG), BF16),
        compiler_params=pltpu.CompilerParams(dimension_semantics=("arbitrary",), vmem_limit_bytes=VMEM_LIMIT),
        name="pack_inproj_weight",
    )(w)


def _inproj_kernel(n_prompt_tiles, xp_ref, xs_ref, g_ref, w_ref, ws_ref,
                   p_ref, ps_ref):
    i = pl.program_id(0)

    def body(x_ref):
        y = _rmsnorm(x_ref[...], g_ref[...])
        y_hi, y_lo = _split2(y)
        ps_ref[...] = _dot_hilo_cols(y_hi, y_lo, ws_ref)
        for c in range(PROJ_BIG // PROJ_CHUNK):
            cols = slice(c * PROJ_CHUNK, (c + 1) * PROJ_CHUNK)
            p_ref[:, cols] = _dot(y_hi, w_ref[:, cols]).astype(p_ref.dtype)

    @pl.when(i < n_prompt_tiles)
    def _():
        body(xp_ref)

    @pl.when(i >= n_prompt_tiles)
    def _():
        body(xs_ref)


def _inproj(xp, xs, g, w_big, ws):
    n_p, n_s = xp.shape[0], xs.shape[0]
    assert n_p % ROW_TILE == 0 and n_s == ROW_TILE
    npt = n_p // ROW_TILE
    n = n_p + n_s
    once = pl.Buffered(1)
    return pl.pallas_call(
        functools.partial(_inproj_kernel, npt),
        grid=(npt + 1,),
        in_specs=[
            pl.BlockSpec((ROW_TILE, D_MODEL), lambda i: (jnp.minimum(i, npt - 1), 0)),
            pl.BlockSpec((ROW_TILE, D_MODEL), lambda i: (0, 0)),
            pl.BlockSpec((1, D_MODEL), lambda i: (0, 0)),
            pl.BlockSpec((D_MODEL, PROJ_BIG), lambda i: (0, 0), pipeline_mode=once),
            pl.BlockSpec((D_MODEL, 2 * LANES), lambda i: (0, 0), pipeline_mode=once),
        ],
        out_specs=[
            pl.BlockSpec((ROW_TILE, PROJ_BIG), lambda i: (i, 0)),
            pl.BlockSpec((ROW_TILE, LANES), lambda i: (i, 0)),
        ],
        out_shape=[
            jax.ShapeDtypeStruct((n, PROJ_BIG), BF16),
            jax.ShapeDtypeStruct((n, LANES), F32),
        ],
        compiler_params=pltpu.CompilerParams(
            dimension_semantics=("arbitrary",), vmem_limit_bytes=VMEM_LIMIT),
        name="inproj",
    )(xp, xs, g, w_big, ws)


def _mixer_kernel(rows, gq_ref, gk_ref, gv_ref, gg_ref, mqk_ref, mv_ref, mo_ref, sm_ref,
                  s0_ref, c0_ref, n0_ref, m0_ref, cv0_ref,
                  wz_hi_ref, wz_lo_ref, bz_ref, g_gla_ref, wc_ref, bc_ref, bsm_ref, g_ml_ref,
                  o_ref, s_out_ref, c_out_ref, n_out_ref, m_out_ref, cv_out_ref,
                  st_s, c_s, n_s, m_s, cb_s, b_s):
    blk = pl.program_id(1)
    n_blk = pl.num_programs(1)

    @pl.when(blk == 0)
    def _():
        for h in range(N_HEADS):
            st_s[h] = s0_ref[0, h].T
            c_s[h] = c0_ref[0, h]
            n_s[h] = n0_ref[0, h:h + 1, :]
            m_s[h] = jnp.broadcast_to(m0_ref[0, :, h:h + 1], (1, LANES))
        cb_s[0:8, :] = jnp.zeros((8, QK_CH), F32)
        cb_s[8 - (CONV_W - 1):8, :] = cv0_ref[0]

    small = sm_ref[...]
    row_i = lax.broadcasted_iota(jnp.int32, (rows, rows), 0)
    col_i = lax.broadcasted_iota(jnp.int32, (rows, rows), 1)
    causal = col_i <= row_i

    sm_hi, sm_lo = _split2(small)
    z = _dot_hilo(sm_hi, sm_lo, wz_hi_ref[...], wz_lo_ref[...]) + bz_ref[...]
    log_a = _log_sigmoid(z) * (1.0 / GLA_GATE_NORM)
    tri = jnp.where(causal, 1.0, 0.0).astype(BF16)
    b_blk = _dot_exact_lhs(tri, log_a)
    b_last = b_blk[rows - 1:rows, :]
    factorable = jnp.min(b_last) >= -GLA_SAFE_DECAY

    def gla_finish(o, vc, gate):
        y = o * lax.rsqrt(jnp.mean(o * o, axis=-1, keepdims=True) + EPS) * g_gla_ref[:, vc]
        return (y * (gate * _sigmoid(gate))).astype(o_ref.dtype)

    @pl.when(factorable)
    def _():
        for h in range(N_HEADS):
            kc = slice(h * DK, (h + 1) * DK)
            vc = slice(h * DV, (h + 1) * DV)
            bh = b_blk[:, kc]
            qh = gq_ref[:, kc].astype(F32) * (DK ** -0.5)
            kh = gk_ref[:, kc].astype(F32)
            vh = gv_ref[:, vc]
            st = st_s[h]
            q_dec = (qh * jnp.exp(bh)).astype(BF16)
            k_inv = (kh * jnp.exp(-bh)).astype(BF16)
            a = jnp.where(causal, _dot_nt(q_dec, k_inv), 0.0)
            o = _dot(a.astype(BF16), vh) + _dot_nt(q_dec, st.astype(BF16))
            bh_end = b_last[:, kc]
            k_dec = (kh * jnp.exp(bh_end - bh)).astype(BF16)
            st_s[h] = st * jnp.exp(bh_end) + _dot_tn(vh, k_dec)
            o_ref[:, vc] = gla_finish(o, vc, gg_ref[:, vc].astype(F32))

    sub_r = lax.broadcasted_iota(jnp.int32, (GLA_SUB, LANES), 0)
    sub_l = lax.broadcasted_iota(jnp.int32, (GLA_SUB, LANES), 1)

    def gla_sub(c, carry):
        r0 = pl.multiple_of(c * GLA_SUB, GLA_SUB)
        rs = pl.ds(r0, GLA_SUB)
        for h in range(N_HEADS):
            kc = slice(h * DK, (h + 1) * DK)
            vc = slice(h * DV, (h + 1) * DV)
            bh = b_s[rs, kc]
            qh = gq_ref[rs, kc].astype(F32) * (DK ** -0.5)
            kh = gk_ref[rs, kc].astype(F32)
            vh = gv_ref[rs, vc]
            st = st_s[h]
            o = _dot_nt((qh * jnp.exp(bh)).astype(BF16), st.astype(BF16))
            a = jnp.zeros((GLA_SUB, LANES), F32)
            for s in range(GLA_SUB):
                e = jnp.exp(jnp.minimum(bh - bh[s:s + 1, :], 0.0))
                col = jnp.sum(qh * (kh[s:s + 1, :] * e), axis=1, keepdims=True)
                a = jnp.where((sub_l == s) & (sub_r >= s), col, a)
            o = o + _dot(a[:, :GLA_SUB].astype(BF16), vh)
            b_end = bh[GLA_SUB - 1:GLA_SUB, :]
            k_dec = kh * jnp.exp(b_end - bh)
            st_s[h] = st * jnp.exp(b_end) + _dot_tn(vh, k_dec.astype(BF16))
            o_ref[rs, vc] = gla_finish(o, vc, gg_ref[rs, vc].astype(F32))
        return carry

    @pl.when(jnp.logical_not(factorable))
    def _():
        same_sub = (row_i // GLA_SUB) == (col_i // GLA_SUB)
        blk_tri = jnp.where(causal & same_sub, 1.0, 0.0).astype(BF16)
        b_s[...] = _dot_exact_lhs(blk_tri, log_a)
        lax.fori_loop(0, rows // GLA_SUB, gla_sub, 0)

    cb_s[8:8 + rows, :] = mqk_ref[...].astype(F32)
    conv = bc_ref[...]
    for j in range(CONV_W):
        conv = conv + cb_s[8 - (CONV_W - 1) + j:8 - (CONV_W - 1) + j + rows, :] * wc_ref[j:j + 1, :]
    cb_s[0:8, :] = cb_s[rows:rows + 8, :]
    qk = conv * _sigmoid(conv)

    pre = small + bsm_ref[...]
    log_f = _log_sigmoid(pre)
    f_cum =_dot_exact_lhs(tri, log_f)
    eye = jnp.where(row_i == col_i, 1.0, 0.0).astype(BF16)
    f_cum_t = sum(_dot_tn(p, eye) for p in _split3(f_cum))
    pre_t = sum(_dot_tn(p, eye) for p in _split3(pre))
    lane_1 = lax.broadcasted_iota(jnp.int32, (1, LANES), 1)
    m_new = jnp.zeros((1, LANES), F32)

    for h in range(N_HEADS):
        kc = slice(h * DK, (h + 1) * DK)
        vc = slice(h * DV, (h + 1) * DV)
        f_col = f_cum[:, LANE_F + h:LANE_F + h + 1]
        i_col = pre[:, LANE_I + h:LANE_I + h + 1]
        f_row = f_cum_t[LANE_F + h:LANE_F + h + 1, :]
        i_row = pre_t[LANE_I + h:LANE_I + h + 1, :]
        dm = jnp.where(causal, f_col - f_row + i_row, -jnp.inf)
        m_prev = m_s[h][:, 0:1]
        inter = f_col + m_prev
        m_t = jnp.maximum(inter, jnp.max(dm, axis=1, keepdims=True))
        w = jnp.exp(dm - m_t)
        w_inter = jnp.exp(inter - m_t)
        q = qk[:, kc]
        k = qk[:, N_HEADS * DK + h * DK:N_HEADS * DK + (h + 1) * DK] * (DK ** -0.5)
        v = mv_ref[:, vc]
        q_b = q.astype(BF16)
        c_prev = c_s[h]
        n_prev = n_s[h]
        s_qk = _dot_nt(q_b, k.astype(BF16)) * w
        num = _dot(s_qk.astype(BF16), v) + w_inter * _dot(q_b, c_prev.astype(BF16))
        den = (jnp.sum(s_qk, axis=1, keepdims=True)
               + w_inter * jnp.sum(q * n_prev, axis=1, keepdims=True))
        hh = num / jnp.maximum(jnp.abs(den), jnp.exp(-m_t))
        m_end = m_t[rows - 1:rows, :]
        w_s = jnp.exp(f_col[rows - 1:rows, :] - f_col + i_col - m_end)
        dec = w_inter[rows - 1:rows, :]
        k_w = k * w_s
        c_s[h] = dec * c_prev + _dot_tn(k_w.astype(BF16), v)
        n_s[h] = dec * n_prev + jnp.sum(k_w, axis=0, keepdims=True)
        m_s[h] = jnp.broadcast_to(m_end, (1, LANES))
        m_new = jnp.where(lane_1 == h, m_end, m_new)
        y = hh * lax.rsqrt(jnp.mean(hh * hh, axis=-1, keepdims=True) + EPS) * g_ml_ref[:, vc]
        o_ref[:, N_HEADS * DV + h * DV:N_HEADS * DV + (h + 1) * DV] = (
            y * _sigmoid(mo_ref[:, vc].astype(F32))).astype(o_ref.dtype)

    @pl.when(blk == n_blk - 1)
    def _():
        for h in range(N_HEADS):
            s_out_ref[0, h] = st_s[h].T
            c_out_ref[0, h] = c_s[h]
            n_out_ref[0, h:h + 1, :] = n_s[h]
        m_out_ref[0] = m_new
        cv_out_ref[0] = cb_s[8 - (CONV_W - 1):8, :]


def _mixers(p_big, p_small, row0, n_streams, t_len, rows, s0, c0, n0, m0, cv0, wts):
    assert t_len % rows == 0 and row0 % rows == 0 and rows % GLA_SUB == 0
    n_blk = t_len // rows
    b0 = row0 // rows

    def rmap(col):
        return lambda s, b: (b0 + s * n_blk + b, col)

    def smap(*zeros):
        return lambda s, b: (s,) + zeros

    def wmap(s, b):
        return (0, 0)

    wz_hi, wz_lo, bz, g_gla, wc, bc, bsm, g_ml = wts
    in_specs = [
        pl.BlockSpec((rows, N_HEADS * DK), rmap(0)),
        pl.BlockSpec((rows, N_HEADS * DK), rmap(1)),
        pl.BlockSpec((rows, N_HEADS * DV), rmap(1)),
        pl.BlockSpec((rows, N_HEADS * DV), rmap(2)),
        pl.BlockSpec((rows, QK_CH), rmap(3)),
        pl.BlockSpec((rows, N_HEADS * DV), rmap(4)),
        pl.BlockSpec((rows, N_HEADS * DV), rmap(5)),
        pl.BlockSpec((rows, LANES), rmap(0)),
        pl.BlockSpec((1, N_HEADS, DK, DV), smap(0, 0, 0)),
        pl.BlockSpec((1, N_HEADS, DK, DV), smap(0, 0, 0)),
        pl.BlockSpec((1, N_HEADS, DK), smap(0, 0)),
        pl.BlockSpec((1, 1, N_HEADS), smap(0, 0)),
        pl.BlockSpec((1, CONV_W - 1, QK_CH), smap(0, 0)),
        pl.BlockSpec(wz_hi.shape, wmap), pl.BlockSpec(wz_lo.shape, wmap),
        pl.BlockSpec(bz.shape, wmap), pl.BlockSpec(g_gla.shape, wmap),
        pl.BlockSpec(wc.shape, wmap), pl.BlockSpec(bc.shape, wmap),
        pl.BlockSpec(bsm.shape, wmap), pl.BlockSpec(g_ml.shape, wmap),
    ]
    n_rows = n_streams * t_len
    out_shape = [
        jax.ShapeDtypeStruct((n_rows, D_MODEL), BF16),
        jax.ShapeDtypeStruct((n_streams, N_HEADS, DK, DV), F32),
        jax.ShapeDtypeStruct((n_streams, N_HEADS, DK, DV), F32),
        jax.ShapeDtypeStruct((n_streams, N_HEADS, DK), F32),
        jax.ShapeDtypeStruct((n_streams, 1, LANES), F32),
        jax.ShapeDtypeStruct((n_streams, CONV_W - 1, QK_CH), F32),
    ]
    out_specs = [
        pl.BlockSpec((rows, D_MODEL), lambda s, b: (s * n_blk + b, 0)),
        pl.BlockSpec((1, N_HEADS, DK, DV), smap(0, 0, 0)),
        pl.BlockSpec((1, N_HEADS, DK, DV), smap(0, 0, 0)),
        pl.BlockSpec((1, N_HEADS, DK), smap(0, 0)),
        pl.BlockSpec((1, 1, LANES), smap(0, 0)),
        pl.BlockSpec((1, CONV_W - 1, QK_CH), smap(0, 0)),
    ]
    scratch = [
        pltpu.VMEM((N_HEADS, DV, DK), F32),
        pltpu.VMEM((N_HEADS, DK, DV), F32),
        pltpu.VMEM((N_HEADS, 1, DK), F32),
        pltpu.VMEM((N_HEADS, 1, LANES), F32),
        pltpu.VMEM((rows + 8, QK_CH), F32),
        pltpu.VMEM((rows, N_HEADS * DK), F32),
    ]
    return pl.pallas_call(
        functools.partial(_mixer_kernel, rows),
        grid=(n_streams, n_blk),
        in_specs=in_specs, out_specs=out_specs, out_shape=out_shape,
        scratch_shapes=scratch,
        compiler_params=pltpu.CompilerParams(
            dimension_semantics=("arbitrary", "arbitrary"), vmem_limit_bytes=VMEM_LIMIT),
        name=f"mixers_r{rows}",
    )(p_big, p_big, p_big, p_big, p_big, p_big, p_big, p_small, s0, c0, n0, m0, cv0, *wts)


def _pack_rows(ref, val):
    word = pltpu.pack_elementwise([val[:, :HALF], val[:, HALF:]], packed_dtype=BF16)
    rows = val.shape[0]
    for j in range(REC_ROWS):
        ref[pl.ds(j, rows, stride=REC_ROWS), :] = word[:, j * LANES:(j + 1) * LANES]


def _unpack_rows(ref, rows):
    word = jnp.concatenate(
        [ref[pl.ds(j, rows, stride=REC_ROWS), :] for j in range(REC_ROWS)], axis=1)
    lo = pltpu.unpack_elementwise(word, index=0, packed_dtype=BF16, unpacked_dtype=F32)
    hi = pltpu.unpack_elementwise(word, index=1, packed_dtype=BF16, unpacked_dtype=F32)
    return lo, hi


def _outproj_kernel(n_prompt_tiles, op_ref, os_ref, xp_ref, xs_ref, w_ref, g_ref, wr_ref,
                    x1_ref, xn_ref, route_ref, code_ref, cnt_ref, cnt_s):
    i = pl.program_id(0)

    @pl.when(i == 0)
    def _():
        cnt_s[...] = jnp.zeros_like(cnt_s)

    def body(o_ref, x_ref):
        x1 = x_ref[...] + _dot(o_ref[...], w_ref[...])
        x1_ref[...] = x1
        xn = _rmsnorm(x1, g_ref[...])
        _pack_rows(xn_ref, xn)
        xn_hi, xn_lo = _split2(xn)
        lg = _dot_hilo_cols(xn_hi, xn_lo, wr_ref)
        lane = lax.broadcasted_iota(jnp.int32, lg.shape, 1).astype(F32)
        neg = -jnp.inf
        lgm = jnp.where(lane < N_GROUPS, lg, neg)
        mg = jnp.max(lgm, axis=1, keepdims=True)
        g_idx = jnp.min(jnp.where(lgm == mg, lane, float(LANES)), axis=1, keepdims=True)
        g_val = 1.0 / jnp.sum(jnp.where(lane < N_GROUPS, jnp.exp(lg - mg), 0.0), axis=1, keepdims=True)
        e0 = ROUTER_E0 + EXPERTS_PER_GROUP * g_idx
        le = jnp.where((lane >= e0) & (lane < e0 + EXPERTS_PER_GROUP), lg, neg)
        v1 = jnp.max(le, axis=1, keepdims=True)
        i1 = jnp.min(jnp.where(le == v1, lane, float(LANES)), axis=1, keepdims=True)
        le2 = jnp.where(lane == i1, neg, le)
        v2 = jnp.max(le2, axis=1, keepdims=True)
        i2 = jnp.min(jnp.where(le2 == v2, lane, float(LANES)), axis=1, keepdims=True)
        t = jnp.exp(v2 - v1)
        w1 = g_val / (1.0 + t)
        w2 = g_val * t / (1.0 + t)
        oh1 = lane == i1
        oh2 = lane == i2
        hot = jnp.where(oh1 | oh2, 1.0, 0.0)
        r_i = lax.broadcasted_iota(jnp.int32, (ROW_TILE, ROW_TILE), 0)
        c_i = lax.broadcasted_iota(jnp.int32, (ROW_TILE, ROW_TILE), 1)
        before = jnp.where(c_i < r_i, 1.0, 0.0).astype(BF16)
        seen = _dot(before, hot.astype(BF16)) + cnt_s[...]
        rank1 = jnp.sum(jnp.where(oh1, seen, 0.0), axis=1, keepdims=True)
        rank2 = jnp.sum(jnp.where(oh2, seen, 0.0), axis=1, keepdims=True)
        cnt_s[...] += jnp.sum(hot, axis=0, keepdims=True)
        code1 = (i1 - ROUTER_E0) * float(SLOT_CODE) + rank1
        code2 = (i2 - ROUTER_E0) * float(SLOT_CODE) + rank2
        route = jnp.zeros_like(lg)
        for k, col in ((0, code1), (1, code2), (4, w1), (5, w2)):
            route = jnp.where(lane == k, col, route)
        route_ref[...] = route
        code_ref[0] = route.T[0:8, :].astype(jnp.int32)

    @pl.when(i < n_prompt_tiles)
    def _():
        body(op_ref, xp_ref)

    @pl.when(i >= n_prompt_tiles)
    def _():
        body(os_ref, xs_ref)

    cnt_ref[...] = cnt_s[...]


def _outproj(o_p, o_s, xp, xs, w_out, g, wr):
    n_p, n_s = xp.shape[0], xs.shape[0]
    npt = n_p // ROW_TILE
    n = n_p + n_s

    def pmap(i):
        return (jnp.minimum(i, npt - 1), 0)

    def cmap(i):
        return (0, 0)

    def omap(i):
        return (i, 0)

    return pl.pallas_call(
        functools.partial(_outproj_kernel, npt),
        grid=(npt + 1,),
        in_specs=[
            pl.BlockSpec((ROW_TILE, D_MODEL), pmap), pl.BlockSpec((ROW_TILE, D_MODEL), cmap),
            pl.BlockSpec((ROW_TILE, D_MODEL), pmap), pl.BlockSpec((ROW_TILE, D_MODEL), cmap),
            pl.BlockSpec((D_MODEL, D_MODEL), cmap), pl.BlockSpec((1, D_MODEL), cmap),
            pl.BlockSpec((D_MODEL, 2 * LANES), cmap),
        ],
        out_specs=[
            pl.BlockSpec((ROW_TILE, D_MODEL), omap), pl.BlockSpec((ROW_TILE * REC_ROWS, LANES), omap),
            pl.BlockSpec((ROW_TILE, LANES), omap), pl.BlockSpec((1, 8, ROW_TILE), lambda i: (i, 0, 0)),
            pl.BlockSpec((1, LANES), cmap),
        ],
        out_shape=[
            jax.ShapeDtypeStruct((n, D_MODEL), F32),
            jax.ShapeDtypeStruct((n * REC_ROWS, LANES), jnp.uint32),
            jax.ShapeDtypeStruct((n, LANES), F32),
            jax.ShapeDtypeStruct((npt + 1, 8, ROW_TILE), jnp.int32),
            jax.ShapeDtypeStruct((1, LANES), F32),
        ],
        scratch_shapes=[pltpu.VMEM((1, LANES), F32)],
        compiler_params=pltpu.CompilerParams(
            dimension_semantics=("arbitrary",), vmem_limit_bytes=VMEM_LIMIT),
        name="outproj_router",
    )(o_p, o_s, xp, xs, w_out, g, wr)


def _rec(ref, idx):
    return ref.at[pl.ds(pl.multiple_of(idx * REC_ROWS, REC_ROWS), REC_ROWS)]


def _zero_records_kernel(o_ref):
    z = jnp.zeros(o_ref.shape, F32)
    o_ref[...] = pltpu.pack_elementwise([z, z], packed_dtype=BF16)


def _slot_row(code, row0_ref):
    return row0_ref[lax.shift_right_logical(code, SLOT_CODE_BITS)] + (code & (SLOT_CODE - 1))


def _dispatch_kernel(t_max, c1_ref, c2_ref, row0_ref, padlo_ref, padn_ref, nt_ref, xn_ref, xs_ref,
                     zero_s, sem, zsem):
    step = pl.program_id(0)
    base = step * ROW_TILE

    def pad_copy(row):
        return pltpu.make_async_copy(_rec(zero_s, 0), _rec(xs_ref, row), zsem)

    def tile_copy(t):
        rows = EXP_TILE * REC_ROWS
        return pltpu.make_async_copy(zero_s, xs_ref.at[pl.ds(pl.multiple_of(t * rows, rows), rows)], zsem)

    def for_pad_rows(fn):
        def per_expert(e, c):
            lo = padlo_ref[e]
            lax.fori_loop(0, padn_ref[e], lambda r, cc: (fn(pad_copy(lo + r)), cc)[1], 0)
            return c
        lax.fori_loop(0, N_EXPERTS, per_expert, 0)
        lax.fori_loop(nt_ref[0], t_max, lambda t, c: (fn(tile_copy(t)), c)[1], 0)

    @pl.when(step == 0)
    def _():
        _zero_records_kernel(zero_s)
        for_pad_rows(lambda cp: cp.start(priority=1))

    def copies(r):
        src = _rec(xn_ref, r)
        return (pltpu.make_async_copy(src, _rec(xs_ref, _slot_row(c1_ref[base + r], row0_ref)), sem),
                pltpu.make_async_copy(src, _rec(xs_ref, _slot_row(c2_ref[base + r], row0_ref)), sem))

    def start(g, c):
        for u in range(DMA_UNROLL):
            for prio, cp in enumerate(copies(g * DMA_UNROLL + u)):
                cp.start(priority=prio)
        return c

    def wait(g, c):
        for u in range(DMA_UNROLL):
            for cp in copies(g * DMA_UNROLL + u):
                cp.wait()
        return c

    lax.fori_loop(0, ROW_TILE // DMA_UNROLL, start, 0)
    lax.fori_loop(0, ROW_TILE // DMA_UNROLL, wait, 0)

    @pl.when(step == 0)
    def _():
        for_pad_rows(lambda cp: cp.wait())


def _dispatch(code1, code2, row0, pad_lo, pad_n, n_tiles, xn_rec, t_max):
    n = code1.shape[0]
    return pl.pallas_call(
        functools.partial(_dispatch_kernel, t_max),
        grid_spec=pltpu.PrefetchScalarGridSpec(
            num_scalar_prefetch=6,
            grid=(n // ROW_TILE,),
            in_specs=[pl.BlockSpec((ROW_TILE * REC_ROWS, LANES), lambda i, *_: (i, 0))],
            out_specs=pl.BlockSpec(memory_space=pl.ANY),
            scratch_shapes=[pltpu.VMEM((EXP_TILE * REC_ROWS, LANES), jnp.uint32),
                            pltpu.SemaphoreType.DMA, pltpu.SemaphoreType.DMA],
        ),
        out_shape=jax.ShapeDtypeStruct((t_max * EXP_TILE * REC_ROWS, LANES), jnp.uint32),
        compiler_params=pltpu.CompilerParams(
            dimension_semantics=("arbitrary",), vmem_limit_bytes=VMEM_LIMIT),
        name="moe_dispatch",
    )(code1, code2, row0, pad_lo, pad_n, n_tiles, xn_rec)


def _experts_kernel(te_ref, nt_ref, xs_ref, wg_ref, wu_ref, wd_ref, ys_ref, wg_s, wu_s, wd_s):
    t = pl.program_id(0)
    prev = te_ref[jnp.maximum(t - 1, 0)]

    @pl.when((t == 0) | (te_ref[t] != prev))
    def _():
        wg_s[...] = wg_ref[0].astype(BF16)
        wu_s[...] = wu_ref[0].astype(BF16)
        wd_s[...] = wd_ref[0].astype(BF16)

    @pl.when(t < nt_ref[0])
    def _():
        lo, hi = _unpack_rows(xs_ref, EXP_TILE)
        x = jnp.concatenate([lo.astype(BF16), hi.astype(BF16)], axis=1)
        hg = _dot(x, wg_s[...])
        hu = _dot(x, wu_s[...])
        act = hg * _sigmoid(hg) * hu
        _pack_rows(ys_ref, _dot(act.astype(BF16), wd_s[...]))

    @pl.when(t >= nt_ref[0])
    def _():
        _zero_records_kernel(ys_ref)


def _experts(tile_expert, n_tiles, xs, wg, wu, wd):
    t_max = tile_expert.shape[0]

    def tmap(t, te, nt):
        return (jnp.minimum(t, nt[0] - 1), 0)

    def wmap(t, te, nt):
        return (te[t], 0, 0)

    return pl.pallas_call(
        _experts_kernel,
        grid_spec=pltpu.PrefetchScalarGridSpec(
            num_scalar_prefetch=2,
            grid=(t_max,),
            in_specs=[pl.BlockSpec((EXP_TILE * REC_ROWS, LANES), tmap),
                      pl.BlockSpec((1, D_MODEL, D_EXPERT), wmap),
                      pl.BlockSpec((1, D_MODEL, D_EXPERT), wmap),
                      pl.BlockSpec((1, D_EXPERT, D_MODEL), wmap)],
            out_specs=pl.BlockSpec((EXP_TILE * REC_ROWS, LANES), lambda t, te, nt: (t, 0)),
            scratch_shapes=[pltpu.VMEM((D_MODEL, D_EXPERT), BF16), pltpu.VMEM((D_MODEL, D_EXPERT), BF16),
                            pltpu.VMEM((D_EXPERT, D_MODEL), BF16)],
        ),
        out_shape=jax.ShapeDtypeStruct(xs.shape, xs.dtype),
        compiler_params=pltpu.CompilerParams(
            dimension_semantics=("arbitrary",), vmem_limit_bytes=VMEM_LIMIT),
        name="moe_experts",
    )(tile_expert, n_tiles, xs, wg, wu, wd)


def _combine_kernel(n_prompt_tiles, c1_ref, c2_ref, row0_ref, x1_ref, route_ref, g_ref, ys_ref,
                    yp_ref, ysm_ref, a_s, b_s, sem):
    i = pl.program_id(0)
    n_steps = pl.num_programs(0)

    def copies(step, slot, r):
        tok = step * ROW_TILE + r
        return (pltpu.make_async_copy(_rec(ys_ref, _slot_row(c1_ref[tok], row0_ref)),
                                      _rec(a_s.at[slot], r), sem.at[slot]),
                pltpu.make_async_copy(_rec(ys_ref, _slot_row(c2_ref[tok], row0_ref)),
                                      _rec(b_s.at[slot], r), sem.at[slot]))

    def start_all(step, slot):
        def start(g, c):
            for u in range(DMA_UNROLL):
                for prio, cp in enumerate(copies(step, slot, g * DMA_UNROLL + u)):
                    cp.start(priority=prio)
            return c
        lax.fori_loop(0, ROW_TILE // DMA_UNROLL, start, 0)

    def wait_all(step, slot):
        def wait(g, c):
            for u in range(DMA_UNROLL):
                for cp in copies(step, slot, g * DMA_UNROLL + u):
                    cp.wait()
            return c
        lax.fori_loop(0, ROW_TILE // DMA_UNROLL, wait, 0)

    slot = i % 2

    @pl.when(i == 0)
    def _():
        start_all(0, 0)

    @pl.when(i + 1 < n_steps)
    def _():
        start_all(i + 1, 1 - slot)

    wait_all(i, slot)
    a_lo, a_hi = _unpack_rows(a_s.at[slot], ROW_TILE)
    b_lo, b_hi = _unpack_rows(b_s.at[slot], ROW_TILE)
    route = route_ref[...]
    w1 = route[:, 4:5]
    w2 = route[:, 5:6]
    moe = jnp.concatenate([w1 * a_lo + w2 * b_lo, w1 * a_hi + w2 * b_hi], axis=1)
    y = _rmsnorm(x1_ref[...] + moe, g_ref[...])

    @pl.when(i < n_prompt_tiles)
    def _():
        yp_ref[...] = y

    @pl.when(i >= n_prompt_tiles)
    def _():
        ysm_ref[...] = y


def _combine(code1, code2, row0, x1, route, g, ys, n_p):
    n = x1.shape[0]
    npt = n_p // ROW_TILE

    def omap(i, *_):
        return (i, 0)

    return pl.pallas_call(
        functools.partial(_combine_kernel, npt),
        grid_spec=pltpu.PrefetchScalarGridSpec(
            num_scalar_prefetch=3,
            grid=(n // ROW_TILE,),
            in_specs=[pl.BlockSpec((ROW_TILE, D_MODEL), omap),
                      pl.BlockSpec((ROW_TILE, LANES), omap),
                      pl.BlockSpec((1, D_MODEL), lambda i, *_: (0, 0)),
                      pl.BlockSpec(memory_space=pl.ANY)],
            out_specs=[pl.BlockSpec((ROW_TILE, D_MODEL), lambda i, *_: (jnp.minimum(i, npt - 1), 0)),
                       pl.BlockSpec((ROW_TILE, D_MODEL), lambda i, *_: (jnp.maximum(i - npt, 0), 0))],
            scratch_shapes=[pltpu.VMEM((2, ROW_TILE * REC_ROWS, LANES), jnp.uint32),
                            pltpu.VMEM((2, ROW_TILE * REC_ROWS, LANES), jnp.uint32),
                            pltpu.SemaphoreType.DMA((2,))],
        ),
        out_shape=[jax.ShapeDtypeStruct((n_p, D_MODEL), F32),
                   jax.ShapeDtypeStruct((n - n_p, D_MODEL), F32)],
        compiler_params=pltpu.CompilerParams(
            dimension_semantics=("arbitrary",), vmem_limit_bytes=VMEM_LIMIT),
        name="moe_combine",
    )(code1, code2, row0, x1, route, g, ys)


def _pad_lanes(w):
    return jnp.pad(w, ((0, 0), (0, LANES - w.shape[1])))


def _moe_plan(counts, n_tokens):
    cnt = counts[0, ROUTER_E0:ROUTER_E0 + N_EXPERTS].astype(jnp.int32)
    tiles = (cnt + EXP_TILE - 1) // EXP_TILE
    tile_end = jnp.cumsum(tiles)
    row0 = (tile_end - tiles) * EXP_TILE
    t_max = 2 * n_tokens // EXP_TILE + N_EXPERTS
    tile_ids = jnp.arange(t_max, dtype=jnp.int32)
    tile_expert = jnp.minimum(
        jnp.sum((tile_ids[:, None] >= tile_end[None, :]).astype(jnp.int32), axis=1), N_EXPERTS - 1)
    return row0, row0 + cnt, tiles * EXP_TILE - cnt, tile_expert, tile_end[-1:].astype(jnp.int32), t_max


def kernel(x_prompt, x_sample, state_gla_S, state_mlstm_C, state_mlstm_n, state_mlstm_m, cache_mlstm_conv, g_mix_norm, w_in, w_gla_gate_up, b_gla_gate_up, g_gla_out, w_mlstm_conv, b_mlstm_conv, b_mlstm_i, b_mlstm_f, g_mlstm_out, w_out, g_ffn_norm, w_router_group, w_router_expert, w_exp_gate, w_exp_up, w_exp_down, g_final):
    depth = w_in.shape[0]
    assert depth == 1
    bp, t_p, _ = x_prompt.shape
    bs, t_s, _ = x_sample.shape
    assert bp == 1
    xp = x_prompt.reshape(bp * t_p, D_MODEL)
    xs = x_sample.reshape(bs * t_s, D_MODEL)
    n_p = xp.shape[0]

    w = w_in[0]
    c_gz = 3072
    c_mqk = c_gz + GLA_GATE_RANK
    c_mi = c_mqk + 3072
    w_big = _pack_inproj_weight(w, c_gz, c_mqk, c_mi)
    ws = _hilo_cols(_pad_lanes(jnp.concatenate([w[:, c_gz:c_mqk], w[:, c_mi:]], axis=1)))
    wz = jnp.pad(w_gla_gate_up[0], ((0, LANES - GLA_GATE_RANK), (0, 0)))
    wz_hi, wz_lo = _split2(wz)
    bsm = _pad_lanes(jnp.concatenate(
        [jnp.zeros((1, GLA_GATE_RANK), F32), b_mlstm_i[0][None], b_mlstm_f[0][None]], axis=1))
    mix_w = (wz_hi, wz_lo, b_gla_gate_up[0][None], g_gla_out[0][None], w_mlstm_conv[0],
             b_mlstm_conv[0][None], bsm, g_mlstm_out[0][None])
    wr = _hilo_cols(_pad_lanes(jnp.concatenate([w_router_group[0], w_router_expert[0]], axis=1)))

    p_big, p_small = _inproj(xp, xs, g_mix_norm[0][None], w_big, ws)

    dt = x_prompt.dtype
    z_s = jnp.zeros((bp, N_HEADS, DK, DV), dt)
    z_n = jnp.zeros((bp, N_HEADS, DK), dt)
    z_m = jnp.zeros((bp, 1, N_HEADS), dt)
    z_cv = jnp.zeros((bp, CONV_W - 1, QK_CH), dt)
    o_p, p_S, p_C, p_n, p_m, p_cv = _mixers(
        p_big, p_small, 0, bp, t_p, MIX_ROWS, z_s, z_s, z_n, z_m, z_cv, mix_w)
    o_s, s_S, s_C, s_n, s_m, s_cv = _mixers(
        p_big, p_small, n_p, bs, t_s, t_s, state_gla_S[0], state_mlstm_C[0], state_mlstm_n[0],
        state_mlstm_m[0][:, None, :], cache_mlstm_conv[0], mix_w)

    x1, xn_rec, route, codes, counts = _outproj(
        o_p, o_s, xp, xs, w_out[0].astype(BF16), g_ffn_norm[0][None], wr)
    code1 = codes[:, 0, :].reshape(-1)
    code2 = codes[:, 1, :].reshape(-1)
    row0, pad_lo, pad_n, tile_expert, n_tiles, t_max = _moe_plan(counts, x1.shape[0])
    xs_rec = _dispatch(code1, code2, row0, pad_lo, pad_n, n_tiles, xn_rec, t_max)
    ys_rec = _experts(tile_expert, n_tiles, xs_rec, w_exp_gate[0], w_exp_up[0], w_exp_down[0])
    y_p, y_s = _combine(code1, code2, row0, x1, route, g_final[None], ys_rec, n_p)

    return (y_p.reshape(x_prompt.shape), y_s.reshape(x_sample.shape),
            p_S[None], p_C[None], p_n[None], p_m[:, 0, :N_HEADS][None], p_cv[None],
            s_S[None], s_C[None], s_n[None], s_m[:, 0, :N_HEADS][None], s_cv[None])
```

```python
import functools

import jax
import jax.numpy as jnp
from jax import lax
from jax.experimental import pallas as pl
from jax.experimental.pallas import tpu as pltpu

F32 = jnp.float32
BF16 = jnp.bfloat16

D_MODEL = 2048
N_HEADS = 4
DK = 128
DV = 256
GLA_GATE_RANK = 16
GLA_GATE_NORM = 16.0
CONV_W = 4
QK_CH = 2 * N_HEADS * DK
N_GROUPS = 4
EXPERTS_PER_GROUP = 8
N_EXPERTS = N_GROUPS * EXPERTS_PER_GROUP
D_EXPERT = 256
EPS = 1e-6

LANES = 128
GLA_SUB = 16
GLA_SAFE_DECAY = 60.0
MIX_ROWS = 256
ROW_TILE = 512
PROJ_BIG = 6 * 1024
PROJ_CHUNK = 1024
PACK_COLS = 256
LANE_I = GLA_GATE_RANK
LANE_F = GLA_GATE_RANK + N_HEADS
ROUTER_E0 = N_GROUPS
HALF = D_MODEL // 2
REC_ROWS = HALF // LANES
EXP_TILE = 256
DMA_UNROLL = 8
SLOT_CODE_BITS = 16
SLOT_CODE = 1 << SLOT_CODE_BITS

VMEM_LIMIT = 56 * 1024 * 1024


def _dot(a, b):
    return jnp.dot(a, b, preferred_element_type=F32)


def _dot_nt(a, b):
    return lax.dot_general(a, b, (((1,), (1,)), ((), ())), preferred_element_type=F32)


def _dot_tn(a, b):
    return lax.dot_general(a, b, (((0,), (0,)), ((), ())), preferred_element_type=F32)


def _split2(x):
    hi = x.astype(BF16)
    lo = (x - hi.astype(F32)).astype(BF16)
    return hi, lo


def _split3(x):
    hi = x.astype(BF16)
    r = x - hi.astype(F32)
    mid = r.astype(BF16)
    lo = (r - mid.astype(F32)).astype(BF16)
    return hi, mid, lo


def _dot_exact_lhs(m_bf16, x):
    hi, mid, lo = _split3(x)
    return _dot(m_bf16, hi) + _dot(m_bf16, mid) + _dot(m_bf16, lo)


def _dot_hilo(a_hi, a_lo, b_hi, b_lo):
    return _dot(a_hi, b_hi) + _dot(a_lo, b_hi) + _dot(a_hi, b_lo)


def _hilo_cols(w):
    return jnp.concatenate(_split2(w), axis=1)


def _dot_hilo_cols(a_hi, a_lo, b_ref):
    p = _dot(a_hi, b_ref[...])
    return (p[:, LANES:] + _dot(a_lo, b_ref[:, :LANES])) + p[:, :LANES]


def _log_sigmoid(z):
    return jnp.minimum(z, 0.0) - jnp.log1p(jnp.exp(-jnp.abs(z)))


def _sigmoid(z):
    return 0.5 * jnp.tanh(0.5 * z) + 0.5


def _rmsnorm(x, g):
    return x * lax.rsqrt(jnp.mean(x * x, axis=-1, keepdims=True) + EPS) * g


def _pack_inproj_weight_kernel(c_gz, c_mqk, wt_ref, o_ref, buf, sem):
    j = pl.program_id(0)
    n_j = pl.num_programs(0)

    def copy(jj):
        col0 = jj * PACK_COLS
        row = pl.multiple_of(col0 + jnp.where(col0 >= c_gz, c_mqk - c_gz, 0), 16)
        return pltpu.make_async_copy(wt_ref.at[pl.ds(row, PACK_COLS)], buf.at[jj % 2], sem.at[jj % 2])

    @pl.when(j == 0)
    def _():
        copy(0).start()

    @pl.when(j + 1 < n_j)
    def _():
        copy(j + 1).start()

    copy(j).wait()
    r_i = lax.broadcasted_iota(jnp.int32, (PACK_COLS, PACK_COLS), 0)
    c_i = lax.broadcasted_iota(jnp.int32, (PACK_COLS, PACK_COLS), 1)
    eye = jnp.where(r_i == c_i, 1.0, 0.0).astype(BF16)
    o_ref[...] = _dot_tn(buf[j % 2].astype(BF16), eye).astype(o_ref.dtype)


def _pack_inproj_weight(wt, c_gz, c_mqk):
    assert c_gz % PACK_COLS == 0 and (c_mqk - c_gz) % 16 == 0
    return pl.pallas_call(
        functools.partial(_pack_inproj_weight_kernel, c_gz, c_mqk),
        grid=(PROJ_BIG // PACK_COLS,),
        in_specs=[pl.BlockSpec(memory_space=pl.ANY)],
        out_specs=pl.BlockSpec((wt.shape[1], PACK_COLS), lambda j: (0, j)),
        out_shape=jax.ShapeDtypeStruct((wt.shape[1], PROJ_BIG), BF16),
        scratch_shapes=[pltpu.VMEM((2, PACK_COLS, wt.shape[1]), F32), pltpu.SemaphoreType.DMA((2,))],
        compiler_params=pltpu.CompilerParams(dimension_semantics=("arbitrary",), vmem_limit_bytes=VMEM_LIMIT),
        name="pack_inproj_weight",
    )(wt)


def _inproj_kernel(n_prompt_tiles, xp_ref, xs_ref, g_ref, w_ref, ws_ref,
                   p_ref, ps_ref):
    i = pl.program_id(0)

    def body(x_ref):
        y = _rmsnorm(x_ref[...], g_ref[...])
        y_hi, y_lo = _split2(y)
        ps_ref[...] = _dot_hilo_cols(y_hi, y_lo, ws_ref)
        for c in range(PROJ_BIG // PROJ_CHUNK):
            cols = slice(c * PROJ_CHUNK, (c + 1) * PROJ_CHUNK)
            p_ref[:, cols] = _dot(y_hi, w_ref[:, cols]).astype(p_ref.dtype)

    @pl.when(i < n_prompt_tiles)
    def _():
        body(xp_ref)

    @pl.when(i >= n_prompt_tiles)
    def _():
        body(xs_ref)


def _inproj(xp, xs, g, w_big, ws):
    n_p, n_s = xp.shape[0], xs.shape[0]
    assert n_p % ROW_TILE == 0 and n_s == ROW_TILE
    npt = n_p // ROW_TILE
    n = n_p + n_s
    once = pl.Buffered(1)
    return pl.pallas_call(
        functools.partial(_inproj_kernel, npt),
        grid=(npt + 1,),
        in_specs=[
            pl.BlockSpec((ROW_TILE, D_MODEL), lambda i: (jnp.minimum(i, npt - 1), 0)),
            pl.BlockSpec((ROW_TILE, D_MODEL), lambda i: (0, 0)),
            pl.BlockSpec((1, D_MODEL), lambda i: (0, 0)),
            pl.BlockSpec((D_MODEL, PROJ_BIG), lambda i: (0, 0), pipeline_mode=once),
            pl.BlockSpec((D_MODEL, 2 * LANES), lambda i: (0, 0), pipeline_mode=once),
        ],
        out_specs=[
            pl.BlockSpec((ROW_TILE, PROJ_BIG), lambda i: (i, 0)),
            pl.BlockSpec((ROW_TILE, LANES), lambda i: (i, 0)),
        ],
        out_shape=[
            jax.ShapeDtypeStruct((n, PROJ_BIG), BF16),
            jax.ShapeDtypeStruct((n, LANES), F32),
        ],
        compiler_params=pltpu.CompilerParams(
            dimension_semantics=("arbitrary",), vmem_limit_bytes=VMEM_LIMIT),
        name="inproj",
    )(xp, xs, g, w_big, ws)


def _mixer_kernel(rows, gq_ref, gk_ref, gv_ref, gg_ref, mqk_ref, mv_ref, mo_ref, sm_ref,
                  s0_ref, c0_ref, n0_ref, m0_ref, cv0_ref,
                  wz_hi_ref, wz_lo_ref, bz_ref, g_gla_ref, wc_ref, bc_ref, bsm_ref, g_ml_ref,
                  o_ref, s_out_ref, c_out_ref, n_out_ref, m_out_ref, cv_out_ref,
                  st_s, c_s, n_s, m_s, cb_s, b_s):
    blk = pl.program_id(1)
    n_blk = pl.num_programs(1)

    @pl.when(blk == 0)
    def _():
        for h in range(N_HEADS):
            st_s[h] = s0_ref[0, h].T
            c_s[h] = c0_ref[0, h]
            n_s[h] = n0_ref[0, h:h + 1, :]
            m_s[h] = jnp.broadcast_to(m0_ref[0, :, h:h + 1], (1, LANES))
        cb_s[0:8, :] = jnp.zeros((8, QK_CH), F32)
        cb_s[8 - (CONV_W - 1):8, :] = cv0_ref[0]

    small = sm_ref[...]
    row_i = lax.broadcasted_iota(jnp.int32, (rows, rows), 0)
    col_i = lax.broadcasted_iota(jnp.int32, (rows, rows), 1)
    causal = col_i <= row_i

    sm_hi, sm_lo = _split2(small)
    z = _dot_hilo(sm_hi, sm_lo, wz_hi_ref[...], wz_lo_ref[...]) + bz_ref[...]
    log_a = _log_sigmoid(z) * (1.0 / GLA_GATE_NORM)
    tri = jnp.where(causal, 1.0, 0.0).astype(BF16)
    b_blk = _dot_exact_lhs(tri, log_a)
    b_last = b_blk[rows - 1:rows, :]
    factorable = jnp.min(b_last) >= -GLA_SAFE_DECAY

    def gla_finish(o, vc, gate):
        y = o * lax.rsqrt(jnp.mean(o * o, axis=-1, keepdims=True) + EPS) * g_gla_ref[:, vc]
        return (y * (gate * _sigmoid(gate))).astype(o_ref.dtype)

    @pl.when(factorable)
    def _():
        for h in range(N_HEADS):
            kc = slice(h * DK, (h + 1) * DK)
            vc = slice(h * DV, (h + 1) * DV)
            bh = b_blk[:, kc]
            qh = gq_ref[:, kc].astype(F32) * (DK ** -0.5)
            kh = gk_ref[:, kc].astype(F32)
            vh = gv_ref[:, vc]
            st = st_s[h]
            q_dec = (qh * jnp.exp(bh)).astype(BF16)
            k_inv = (kh * jnp.exp(-bh)).astype(BF16)
            a = jnp.where(causal, _dot_nt(q_dec, k_inv), 0.0)
            o = _dot(a.astype(BF16), vh) + _dot_nt(q_dec, st.astype(BF16))
            bh_end = b_last[:, kc]
            k_dec = (kh * jnp.exp(bh_end - bh)).astype(BF16)
            st_s[h] = st * jnp.exp(bh_end) + _dot_tn(vh, k_dec)
            o_ref[:, vc] = gla_finish(o, vc, gg_ref[:, vc].astype(F32))

    sub_r = lax.broadcasted_iota(jnp.int32, (GLA_SUB, LANES), 0)
    sub_l = lax.broadcasted_iota(jnp.int32, (GLA_SUB, LANES), 1)

    def gla_sub(c, carry):
        r0 = pl.multiple_of(c * GLA_SUB, GLA_SUB)
        rs = pl.ds(r0, GLA_SUB)
        for h in range(N_HEADS):
            kc = slice(h * DK, (h + 1) * DK)
            vc = slice(h * DV, (h + 1) * DV)
            bh = b_s[rs, kc]
            qh = gq_ref[rs, kc].astype(F32) * (DK ** -0.5)
            kh = gk_ref[rs, kc].astype(F32)
            vh = gv_ref[rs, vc]
            st = st_s[h]
            o = _dot_nt((qh * jnp.exp(bh)).astype(BF16), st.astype(BF16))
            a = jnp.zeros((GLA_SUB, LANES), F32)
            for s in range(GLA_SUB):
                e = jnp.exp(jnp.minimum(bh - bh[s:s + 1, :], 0.0))
                col = jnp.sum(qh * (kh[s:s + 1, :] * e), axis=1, keepdims=True)
                a = jnp.where((sub_l == s) & (sub_r >= s), col, a)
            o = o + _dot(a[:, :GLA_SUB].astype(BF16), vh)
            b_end = bh[GLA_SUB - 1:GLA_SUB, :]
            k_dec = kh * jnp.exp(b_end - bh)
            st_s[h] = st * jnp.exp(b_end) + _dot_tn(vh, k_dec.astype(BF16))
            o_ref[rs, vc] = gla_finish(o, vc, gg_ref[rs, vc].astype(F32))
        return carry

    @pl.when(jnp.logical_not(factorable))
    def _():
        same_sub = (row_i // GLA_SUB) == (col_i // GLA_SUB)
        blk_tri = jnp.where(causal & same_sub, 1.0, 0.0).astype(BF16)
        b_s[...] = _dot_exact_lhs(blk_tri, log_a)
        lax.fori_loop(0, rows // GLA_SUB, gla_sub, 0)

    cb_s[8:8 + rows, :] = mqk_ref[...].astype(F32)
    conv = bc_ref[...]
    for j in range(CONV_W):
        conv = conv + cb_s[8 - (CONV_W - 1) + j:8 - (CONV_W - 1) + j + rows, :] * wc_ref[j:j + 1, :]
    cb_s[0:8, :] = cb_s[rows:rows + 8, :]
    qk = conv * _sigmoid(conv)

    pre = small + bsm_ref[...]
    log_f = _log_sigmoid(pre)
    f_cum =_dot_exact_lhs(tri, log_f)
    eye = jnp.where(row_i == col_i, 1.0, 0.0).astype(BF16)
    f_cum_t = sum(_dot_tn(p, eye) for p in _split3(f_cum))
    pre_t = sum(_dot_tn(p, eye) for p in _split3(pre))
    lane_1 = lax.broadcasted_iota(jnp.int32, (1, LANES), 1)
    m_new = jnp.zeros((1, LANES), F32)

    for h in range(N_HEADS):
        kc = slice(h * DK, (h + 1) * DK)
        vc = slice(h * DV, (h + 1) * DV)
        f_col = f_cum[:, LANE_F + h:LANE_F + h + 1]
        i_col = pre[:, LANE_I + h:LANE_I + h + 1]
        f_row = f_cum_t[LANE_F + h:LANE_F + h + 1, :]
        i_row = pre_t[LANE_I + h:LANE_I + h + 1, :]
        dm = jnp.where(causal, f_col - f_row + i_row, -jnp.inf)
        m_prev = m_s[h][:, 0:1]
        inter = f_col + m_prev
        m_t = jnp.maximum(inter, jnp.max(dm, axis=1, keepdims=True))
        w = jnp.exp(dm - m_t)
        w_inter = jnp.exp(inter - m_t)
        q = qk[:, kc]
        k = qk[:, N_HEADS * DK + h * DK:N_HEADS * DK + (h + 1) * DK] * (DK ** -0.5)
        v = mv_ref[:, vc]
        q_b = q.astype(BF16)
        c_prev = c_s[h]
        n_prev = n_s[h]
        s_qk = _dot_nt(q_b, k.astype(BF16)) * w
        num = _dot(s_qk.astype(BF16), v) + w_inter * _dot(q_b, c_prev.astype(BF16))
        den = (jnp.sum(s_qk, axis=1, keepdims=True)
               + w_inter * jnp.sum(q * n_prev, axis=1, keepdims=True))
        hh = num / jnp.maximum(jnp.abs(den), jnp.exp(-m_t))
        m_end = m_t[rows - 1:rows, :]
        w_s = jnp.exp(f_col[rows - 1:rows, :] - f_col + i_col - m_end)
        dec = w_inter[rows - 1:rows, :]
        k_w = k * w_s
        c_s[h] = dec * c_prev + _dot_tn(k_w.astype(BF16), v)
        n_s[h] = dec * n_prev + jnp.sum(k_w, axis=0, keepdims=True)
        m_s[h] = jnp.broadcast_to(m_end, (1, LANES))
        m_new = jnp.where(lane_1 == h, m_end, m_new)
        y = hh * lax.rsqrt(jnp.mean(hh * hh, axis=-1, keepdims=True) + EPS) * g_ml_ref[:, vc]
        o_ref[:, N_HEADS * DV + h * DV:N_HEADS * DV + (h + 1) * DV] = (
            y * _sigmoid(mo_ref[:, vc].astype(F32))).astype(o_ref.dtype)

    @pl.when(blk == n_blk - 1)
    def _():
        for h in range(N_HEADS):
            s_out_ref[0, h] = st_s[h].T
            c_out_ref[0, h] = c_s[h]
            n_out_ref[0, h:h + 1, :] = n_s[h]
        m_out_ref[0] = m_new
        cv_out_ref[0] = cb_s[8 - (CONV_W - 1):8, :]


def _mixers(p_big, p_small, row0, n_streams, t_len, rows, s0, c0, n0, m0, cv0, wts):
    assert t_len % rows == 0 and row0 % rows == 0 and rows % GLA_SUB == 0
    n_blk = t_len // rows
    b0 = row0 // rows

    def rmap(col):
        return lambda s, b: (b0 + s * n_blk + b, col)

    def smap(*zeros):
        return lambda s, b: (s,) + zeros

    def wmap(s, b):
        return (0, 0)

    wz_hi, wz_lo, bz, g_gla, wc, bc, bsm, g_ml = wts
    in_specs = [
        pl.BlockSpec((rows, N_HEADS * DK), rmap(0)),
        pl.BlockSpec((rows, N_HEADS * DK), rmap(1)),
        pl.BlockSpec((rows, N_HEADS * DV), rmap(1)),
        pl.BlockSpec((rows, N_HEADS * DV), rmap(2)),
        pl.BlockSpec((rows, QK_CH), rmap(3)),
        pl.BlockSpec((rows, N_HEADS * DV), rmap(4)),
        pl.BlockSpec((rows, N_HEADS * DV), rmap(5)),
        pl.BlockSpec((rows, LANES), rmap(0)),
        pl.BlockSpec((1, N_HEADS, DK, DV), smap(0, 0, 0)),
        pl.BlockSpec((1, N_HEADS, DK, DV), smap(0, 0, 0)),
        pl.BlockSpec((1, N_HEADS, DK), smap(0, 0)),
        pl.BlockSpec((1, 1, N_HEADS), smap(0, 0)),
        pl.BlockSpec((1, CONV_W - 1, QK_CH), smap(0, 0)),
        pl.BlockSpec(wz_hi.shape, wmap), pl.BlockSpec(wz_lo.shape, wmap),
        pl.BlockSpec(bz.shape, wmap), pl.BlockSpec(g_gla.shape, wmap),
        pl.BlockSpec(wc.shape, wmap), pl.BlockSpec(bc.shape, wmap),
        pl.BlockSpec(bsm.shape, wmap), pl.BlockSpec(g_ml.shape, wmap),
    ]
    n_rows = n_streams * t_len
    out_shape = [
        jax.ShapeDtypeStruct((n_rows, D_MODEL), BF16),
        jax.ShapeDtypeStruct((n_streams, N_HEADS, DK, DV), F32),
        jax.ShapeDtypeStruct((n_streams, N_HEADS, DK, DV), F32),
        jax.ShapeDtypeStruct((n_streams, N_HEADS, DK), F32),
        jax.ShapeDtypeStruct((n_streams, 1, LANES), F32),
        jax.ShapeDtypeStruct((n_streams, CONV_W - 1, QK_CH), F32),
    ]
    out_specs = [
        pl.BlockSpec((rows, D_MODEL), lambda s, b: (s * n_blk + b, 0)),
        pl.BlockSpec((1, N_HEADS, DK, DV), smap(0, 0, 0)),
        pl.BlockSpec((1, N_HEADS, DK, DV), smap(0, 0, 0)),
        pl.BlockSpec((1, N_HEADS, DK), smap(0, 0)),
        pl.BlockSpec((1, 1, LANES), smap(0, 0)),
        pl.BlockSpec((1, CONV_W - 1, QK_CH), smap(0, 0)),
    ]
    scratch = [
        pltpu.VMEM((N_HEADS, DV, DK), F32),
        pltpu.VMEM((N_HEADS, DK, DV), F32),
        pltpu.VMEM((N_HEADS, 1, DK), F32),
        pltpu.VMEM((N_HEADS, 1, LANES), F32),
        pltpu.VMEM((rows + 8, QK_CH), F32),
        pltpu.VMEM((rows, N_HEADS * DK), F32),
    ]
    return pl.pallas_call(
        functools.partial(_mixer_kernel, rows),
        grid=(n_streams, n_blk),
        in_specs=in_specs, out_specs=out_specs, out_shape=out_shape,
        scratch_shapes=scratch,
        compiler_params=pltpu.CompilerParams(
            dimension_semantics=("arbitrary", "arbitrary"), vmem_limit_bytes=VMEM_LIMIT),
        name=f"mixers_r{rows}",
    )(p_big, p_big, p_big, p_big, p_big, p_big, p_big, p_small, s0, c0, n0, m0, cv0, *wts)


def _pack_rows(ref, val):
    word = pltpu.pack_elementwise([val[:, :HALF], val[:, HALF:]], packed_dtype=BF16)
    rows = val.shape[0]
    for j in range(REC_ROWS):
        ref[pl.ds(j, rows, stride=REC_ROWS), :] = word[:, j * LANES:(j + 1) * LANES]


def _unpack_rows(ref, rows):
    word = jnp.concatenate(
        [ref[pl.ds(j, rows, stride=REC_ROWS), :] for j in range(REC_ROWS)], axis=1)
    lo = pltpu.unpack_elementwise(word, index=0, packed_dtype=BF16, unpacked_dtype=F32)
    hi = pltpu.unpack_elementwise(word, index=1, packed_dtype=BF16, unpacked_dtype=F32)
    return lo, hi


def _outproj_kernel(n_prompt_tiles, op_ref, os_ref, xp_ref, xs_ref, w_ref, g_ref, wr_ref,
                    x1_ref, xn_ref, route_ref, code_ref, cnt_ref, cnt_s):
    i = pl.program_id(0)

    @pl.when(i == 0)
    def _():
        cnt_s[...] = jnp.zeros_like(cnt_s)

    def body(o_ref, x_ref):
        x1 = x_ref[...] + _dot(o_ref[...], w_ref[...])
        x1_ref[...] = x1
        xn = _rmsnorm(x1, g_ref[...])
        _pack_rows(xn_ref, xn)
        xn_hi, xn_lo = _split2(xn)
        lg = _dot_hilo_cols(xn_hi, xn_lo, wr_ref)
        lane = lax.broadcasted_iota(jnp.int32, lg.shape, 1).astype(F32)
        neg = -jnp.inf
        lgm = jnp.where(lane < N_GROUPS, lg, neg)
        mg = jnp.max(lgm, axis=1, keepdims=True)
        g_idx = jnp.min(jnp.where(lgm == mg, lane, float(LANES)), axis=1, keepdims=True)
        g_val = 1.0 / jnp.sum(jnp.where(lane < N_GROUPS, jnp.exp(lg - mg), 0.0), axis=1, keepdims=True)
        e0 = ROUTER_E0 + EXPERTS_PER_GROUP * g_idx
        le = jnp.where((lane >= e0) & (lane < e0 + EXPERTS_PER_GROUP), lg, neg)
        v1 = jnp.max(le, axis=1, keepdims=True)
        i1 = jnp.min(jnp.where(le == v1, lane, float(LANES)), axis=1, keepdims=True)
        le2 = jnp.where(lane == i1, neg, le)
        v2 = jnp.max(le2, axis=1, keepdims=True)
        i2 = jnp.min(jnp.where(le2 == v2, lane, float(LANES)), axis=1, keepdims=True)
        t = jnp.exp(v2 - v1)
        w1 = g_val / (1.0 + t)
        w2 = g_val * t / (1.0 + t)
        oh1 = lane == i1
        oh2 = lane == i2
        hot = jnp.where(oh1 | oh2, 1.0, 0.0)
        r_i = lax.broadcasted_iota(jnp.int32, (ROW_TILE, ROW_TILE), 0)
        c_i = lax.broadcasted_iota(jnp.int32, (ROW_TILE, ROW_TILE), 1)
        before = jnp.where(c_i < r_i, 1.0, 0.0).astype(BF16)
        seen = _dot(before, hot.astype(BF16)) + cnt_s[...]
        rank1 = jnp.sum(jnp.where(oh1, seen, 0.0), axis=1, keepdims=True)
        rank2 = jnp.sum(jnp.where(oh2, seen, 0.0), axis=1, keepdims=True)
        cnt_s[...] += jnp.sum(hot, axis=0, keepdims=True)
        code1 = (i1 - ROUTER_E0) * float(SLOT_CODE) + rank1
        code2 = (i2 - ROUTER_E0) * float(SLOT_CODE) + rank2
        route = jnp.zeros_like(lg)
        for k, col in ((0, code1), (1, code2), (4, w1), (5, w2)):
            route = jnp.where(lane == k, col, route)
        route_ref[...] = route
        code_ref[0] = route.T[0:8, :].astype(jnp.int32)

    @pl.when(i < n_prompt_tiles)
    def _():
        body(op_ref, xp_ref)

    @pl.when(i >= n_prompt_tiles)
    def _():
        body(os_ref, xs_ref)

    cnt_ref[...] = cnt_s[...]


def _outproj(o_p, o_s, xp, xs, w_out, g, wr):
    n_p, n_s = xp.shape[0], xs.shape[0]
    npt = n_p // ROW_TILE
    n = n_p + n_s

    def pmap(i):
        return (jnp.minimum(i, npt - 1), 0)

    def cmap(i):
        return (0, 0)

    def omap(i):
        return (i, 0)

    return pl.pallas_call(
        functools.partial(_outproj_kernel, npt),
        grid=(npt + 1,),
        in_specs=[
            pl.BlockSpec((ROW_TILE, D_MODEL), pmap), pl.BlockSpec((ROW_TILE, D_MODEL), cmap),
            pl.BlockSpec((ROW_TILE, D_MODEL), pmap), pl.BlockSpec((ROW_TILE, D_MODEL), cmap),
            pl.BlockSpec((D_MODEL, D_MODEL), cmap), pl.BlockSpec((1, D_MODEL), cmap),
            pl.BlockSpec((D_MODEL, 2 * LANES), cmap),
        ],
        out_specs=[
            pl.BlockSpec((ROW_TILE, D_MODEL), omap), pl.BlockSpec((ROW_TILE * REC_ROWS, LANES), omap),
            pl.BlockSpec((ROW_TILE, LANES), omap), pl.BlockSpec((1, 8, ROW_TILE), lambda i: (i, 0, 0)),
            pl.BlockSpec((1, LANES), cmap),
        ],
        out_shape=[
            jax.ShapeDtypeStruct((n, D_MODEL), F32),
            jax.ShapeDtypeStruct((n * REC_ROWS, LANES), jnp.uint32),
            jax.ShapeDtypeStruct((n, LANES), F32),
            jax.ShapeDtypeStruct((npt + 1, 8, ROW_TILE), jnp.int32),
            jax.ShapeDtypeStruct((1, LANES), F32),
        ],
        scratch_shapes=[pltpu.VMEM((1, LANES), F32)],
        compiler_params=pltpu.CompilerParams(
            dimension_semantics=("arbitrary",), vmem_limit_bytes=VMEM_LIMIT),
        name="outproj_router",
    )(o_p, o_s, xp, xs, w_out, g, wr)


def _rec(ref, idx):
    return ref.at[pl.ds(pl.multiple_of(idx * REC_ROWS, REC_ROWS), REC_ROWS)]


def _zero_records_kernel(o_ref):
    z = jnp.zeros(o_ref.shape, F32)
    o_ref[...] = pltpu.pack_elementwise([z, z], packed_dtype=BF16)


def _slot_row(code, row0_ref):
    return row0_ref[lax.shift_right_logical(code, SLOT_CODE_BITS)] + (code & (SLOT_CODE - 1))


def _dispatch_kernel(t_max, c1_ref, c2_ref, row0_ref, padlo_ref, padn_ref, nt_ref, xn_ref, xs_ref,
                     zero_s, sem, zsem):
    step = pl.program_id(0)
    base = step * ROW_TILE

    def pad_copy(row):
        return pltpu.make_async_copy(_rec(zero_s, 0), _rec(xs_ref, row), zsem)

    def tile_copy(t):
        rows = EXP_TILE * REC_ROWS
        return pltpu.make_async_copy(zero_s, xs_ref.at[pl.ds(pl.multiple_of(t * rows, rows), rows)], zsem)

    def for_pad_rows(fn):
        def per_expert(e, c):
            lo = padlo_ref[e]
            lax.fori_loop(0, padn_ref[e], lambda r, cc: (fn(pad_copy(lo + r)), cc)[1], 0)
            return c
        lax.fori_loop(0, N_EXPERTS, per_expert, 0)
        lax.fori_loop(nt_ref[0], t_max, lambda t, c: (fn(tile_copy(t)), c)[1], 0)

    @pl.when(step == 0)
    def _():
        _zero_records_kernel(zero_s)
        for_pad_rows(lambda cp: cp.start(priority=1))

    def copies(r):
        src = _rec(xn_ref, r)
        return (pltpu.make_async_copy(src, _rec(xs_ref, _slot_row(c1_ref[base + r], row0_ref)), sem),
                pltpu.make_async_copy(src, _rec(xs_ref, _slot_row(c2_ref[base + r], row0_ref)), sem))

    def start(g, c):
        for u in range(DMA_UNROLL):
            for prio, cp in enumerate(copies(g * DMA_UNROLL + u)):
                cp.start(priority=prio)
        return c

    def wait(g, c):
        for u in range(DMA_UNROLL):
            for cp in copies(g * DMA_UNROLL + u):
                cp.wait()
        return c

    lax.fori_loop(0, ROW_TILE // DMA_UNROLL, start, 0)
    lax.fori_loop(0, ROW_TILE // DMA_UNROLL, wait, 0)

    @pl.when(step == 0)
    def _():
        for_pad_rows(lambda cp: cp.wait())


def _dispatch(code1, code2, row0, pad_lo, pad_n, n_tiles, xn_rec, t_max):
    n = code1.shape[0]
    return pl.pallas_call(
        functools.partial(_dispatch_kernel, t_max),
        grid_spec=pltpu.PrefetchScalarGridSpec(
            num_scalar_prefetch=6,
            grid=(n // ROW_TILE,),
            in_specs=[pl.BlockSpec((ROW_TILE * REC_ROWS, LANES), lambda i, *_: (i, 0))],
            out_specs=pl.BlockSpec(memory_space=pl.ANY),
            scratch_shapes=[pltpu.VMEM((EXP_TILE * REC_ROWS, LANES), jnp.uint32),
                            pltpu.SemaphoreType.DMA, pltpu.SemaphoreType.DMA],
        ),
        out_shape=jax.ShapeDtypeStruct((t_max * EXP_TILE * REC_ROWS, LANES), jnp.uint32),
        compiler_params=pltpu.CompilerParams(
            dimension_semantics=("arbitrary",), vmem_limit_bytes=VMEM_LIMIT),
        name="moe_dispatch",
    )(code1, code2, row0, pad_lo, pad_n, n_tiles, xn_rec)


def _experts_kernel(te_ref, nt_ref, xs_ref, wg_ref, wu_ref, wd_ref, ys_ref, wg_s, wu_s, wd_s):
    t = pl.program_id(0)
    prev = te_ref[jnp.maximum(t - 1, 0)]

    @pl.when((t == 0) | (te_ref[t] != prev))
    def _():
        wg_s[...] = wg_ref[0].astype(BF16)
        wu_s[...] = wu_ref[0].astype(BF16)
        wd_s[...] = wd_ref[0].astype(BF16)

    @pl.when(t < nt_ref[0])
    def _():
        lo, hi = _unpack_rows(xs_ref, EXP_TILE)
        x = jnp.concatenate([lo.astype(BF16), hi.astype(BF16)], axis=1)
        hg = _dot(x, wg_s[...])
        hu = _dot(x, wu_s[...])
        act = hg * _sigmoid(hg) * hu
        _pack_rows(ys_ref, _dot(act.astype(BF16), wd_s[...]))

    @pl.when(t >= nt_ref[0])
    def _():
        _zero_records_kernel(ys_ref)


def _experts(tile_expert, n_tiles, xs, wg, wu, wd):
    t_max = tile_expert.shape[0]

    def tmap(t, te, nt):
        return (jnp.minimum(t, nt[0] - 1), 0)

    def wmap(t, te, nt):
        return (te[t], 0, 0)

    return pl.pallas_call(
        _experts_kernel,
        grid_spec=pltpu.PrefetchScalarGridSpec(
            num_scalar_prefetch=2,
            grid=(t_max,),
            in_specs=[pl.BlockSpec((EXP_TILE * REC_ROWS, LANES), tmap),
                      pl.BlockSpec((1, D_MODEL, D_EXPERT), wmap),
                      pl.BlockSpec((1, D_MODEL, D_EXPERT), wmap),
                      pl.BlockSpec((1, D_EXPERT, D_MODEL), wmap)],
            out_specs=pl.BlockSpec((EXP_TILE * REC_ROWS, LANES), lambda t, te, nt: (t, 0)),
            scratch_shapes=[pltpu.VMEM((D_MODEL, D_EXPERT), BF16), pltpu.VMEM((D_MODEL, D_EXPERT), BF16),
                            pltpu.VMEM((D_EXPERT, D_MODEL), BF16)],
        ),
        out_shape=jax.ShapeDtypeStruct(xs.shape, xs.dtype),
        compiler_params=pltpu.CompilerParams(
            dimension_semantics=("arbitrary",), vmem_limit_bytes=VMEM_LIMIT),
        name="moe_experts",
    )(tile_expert, n_tiles, xs, wg, wu, wd)


def _combine_kernel(n_prompt_tiles, c1_ref, c2_ref, row0_ref, x1_ref, route_ref, g_ref, ys_ref,
                    yp_ref, ysm_ref, a_s, b_s, sem):
    i = pl.program_id(0)
    n_steps = pl.num_programs(0)

    def copies(step, slot, r):
        tok = step * ROW_TILE + r
        return (pltpu.make_async_copy(_rec(ys_ref, _slot_row(c1_ref[tok], row0_ref)),
                                      _rec(a_s.at[slot], r), sem.at[slot]),
                pltpu.make_async_copy(_rec(ys_ref, _slot_row(c2_ref[tok], row0_ref)),
                                      _rec(b_s.at[slot], r), sem.at[slot]))

    def start_all(step, slot):
        def start(g, c):
            for u in range(DMA_UNROLL):
                for prio, cp in enumerate(copies(step, slot, g * DMA_UNROLL + u)):
                    cp.start(priority=prio)
            return c
        lax.fori_loop(0, ROW_TILE // DMA_UNROLL, start, 0)

    def wait_all(step, slot):
        def wait(g, c):
            for u in range(DMA_UNROLL):
                for cp in copies(step, slot, g * DMA_UNROLL + u):
                    cp.wait()
            return c
        lax.fori_loop(0, ROW_TILE // DMA_UNROLL, wait, 0)

    slot = i % 2

    @pl.when(i == 0)
    def _():
        start_all(0, 0)

    @pl.when(i + 1 < n_steps)
    def _():
        start_all(i + 1, 1 - slot)

    wait_all(i, slot)
    a_lo, a_hi = _unpack_rows(a_s.at[slot], ROW_TILE)
    b_lo, b_hi = _unpack_rows(b_s.at[slot], ROW_TILE)
    route = route_ref[...]
    w1 = route[:, 4:5]
    w2 = route[:, 5:6]
    moe = jnp.concatenate([w1 * a_lo + w2 * b_lo, w1 * a_hi + w2 * b_hi], axis=1)
    y = _rmsnorm(x1_ref[...] + moe, g_ref[...])

    @pl.when(i < n_prompt_tiles)
    def _():
        yp_ref[...] = y

    @pl.when(i >= n_prompt_tiles)
    def _():
        ysm_ref[...] = y


def _combine(code1, code2, row0, x1, route, g, ys, n_p):
    n = x1.shape[0]
    npt = n_p // ROW_TILE

    def omap(i, *_):
        return (i, 0)

    return pl.pallas_call(
        functools.partial(_combine_kernel, npt),
        grid_spec=pltpu.PrefetchScalarGridSpec(
            num_scalar_prefetch=3,
            grid=(n // ROW_TILE,),
            in_specs=[pl.BlockSpec((ROW_TILE, D_MODEL), omap),
                      pl.BlockSpec((ROW_TILE, LANES), omap),
                      pl.BlockSpec((1, D_MODEL), lambda i, *_: (0, 0)),
                      pl.BlockSpec(memory_space=pl.ANY)],
            out_specs=[pl.BlockSpec((ROW_TILE, D_MODEL), lambda i, *_: (jnp.minimum(i, npt - 1), 0)),
                       pl.BlockSpec((ROW_TILE, D_MODEL), lambda i, *_: (jnp.maximum(i - npt, 0), 0))],
            scratch_shapes=[pltpu.VMEM((2, ROW_TILE * REC_ROWS, LANES), jnp.uint32),
                            pltpu.VMEM((2, ROW_TILE * REC_ROWS, LANES), jnp.uint32),
                            pltpu.SemaphoreType.DMA((2,))],
        ),
        out_shape=[jax.ShapeDtypeStruct((n_p, D_MODEL), F32),
                   jax.ShapeDtypeStruct((n - n_p, D_MODEL), F32)],
        compiler_params=pltpu.CompilerParams(
            dimension_semantics=("arbitrary",), vmem_limit_bytes=VMEM_LIMIT),
        name="moe_combine",
    )(code1, code2, row0, x1, route, g, ys)


def _pad_lanes(w):
    return jnp.pad(w, ((0, 0), (0, LANES - w.shape[1])))


def _moe_plan(counts, n_tokens):
    cnt = counts[0, ROUTER_E0:ROUTER_E0 + N_EXPERTS].astype(jnp.int32)
    tiles = (cnt + EXP_TILE - 1) // EXP_TILE
    tile_end = jnp.cumsum(tiles)
    row0 = (tile_end - tiles) * EXP_TILE
    t_max = 2 * n_tokens // EXP_TILE + N_EXPERTS
    tile_ids = jnp.arange(t_max, dtype=jnp.int32)
    tile_expert = jnp.minimum(
        jnp.sum((tile_ids[:, None] >= tile_end[None, :]).astype(jnp.int32), axis=1), N_EXPERTS - 1)
    return row0, row0 + cnt, tiles * EXP_TILE - cnt, tile_expert, tile_end[-1:].astype(jnp.int32), t_max


def kernel(x_prompt, x_sample, state_gla_S, state_mlstm_C, state_mlstm_n, state_mlstm_m, cache_mlstm_conv, g_mix_norm, w_in, w_gla_gate_up, b_gla_gate_up, g_gla_out, w_mlstm_conv, b_mlstm_conv, b_mlstm_i, b_mlstm_f, g_mlstm_out, w_out, g_ffn_norm, w_router_group, w_router_expert, w_exp_gate, w_exp_up, w_exp_down, g_final):
    depth = w_in.shape[0]
    assert depth == 1
    bp, t_p, _ = x_prompt.shape
    bs, t_s, _ = x_sample.shape
    assert bp == 1
    xp = x_prompt.reshape(bp * t_p, D_MODEL)
    xs = x_sample.reshape(bs * t_s, D_MODEL)
    n_p = xp.shape[0]

    wt = jnp.transpose(w_in[0])
    c_gz = 3072
    c_mqk = c_gz + GLA_GATE_RANK
    c_mi = c_mqk + 3072
    w_big = _pack_inproj_weight(wt, c_gz, c_mqk)
    ws = _hilo_cols(_pad_lanes(jnp.concatenate([wt[c_gz:c_mqk], wt[c_mi:]], axis=0).T))
    wz = jnp.pad(w_gla_gate_up[0], ((0, LANES - GLA_GATE_RANK), (0, 0)))
    wz_hi, wz_lo = _split2(wz)
    bsm = _pad_lanes(jnp.concatenate(
        [jnp.zeros((1, GLA_GATE_RANK), F32), b_mlstm_i[0][None], b_mlstm_f[0][None]], axis=1))
    mix_w = (wz_hi, wz_lo, b_gla_gate_up[0][None], g_gla_out[0][None], w_mlstm_conv[0],
             b_mlstm_conv[0][None], bsm, g_mlstm_out[0][None])
    wr = _hilo_cols(_pad_lanes(jnp.concatenate([w_router_group[0], w_router_expert[0]], axis=1)))

    p_big, p_small = _inproj(xp, xs, g_mix_norm[0][None], w_big, ws)

    dt = x_prompt.dtype
    z_s = jnp.zeros((bp, N_HEADS, DK, DV), dt)
    z_n = jnp.zeros((bp, N_HEADS, DK), dt)
    z_m = jnp.zeros((bp, 1, N_HEADS), dt)
    z_cv = jnp.zeros((bp, CONV_W - 1, QK_CH), dt)
    o_p, p_S, p_C, p_n, p_m, p_cv = _mixers(
        p_big, p_small, 0, bp, t_p, MIX_ROWS, z_s, z_s, z_n, z_m, z_cv, mix_w)
    o_s, s_S, s_C, s_n, s_m, s_cv = _mixers(
        p_big, p_small, n_p, bs, t_s, t_s, state_gla_S[0], state_mlstm_C[0], state_mlstm_n[0],
        state_mlstm_m[0][:, None, :], cache_mlstm_conv[0], mix_w)

    x1, xn_rec, route, codes, counts = _outproj(
        o_p, o_s, xp, xs, w_out[0].astype(BF16), g_ffn_norm[0][None], wr)
    code1 = codes[:, 0, :].reshape(-1)
    code2 = codes[:, 1, :].reshape(-1)
    row0, pad_lo, pad_n, tile_expert, n_tiles, t_max = _moe_plan(counts, x1.shape[0])
    xs_rec = _dispatch(code1, code2, row0, pad_lo, pad_n, n_tiles, xn_rec, t_max)
    ys_rec = _experts(tile_expert, n_tiles, xs_rec, w_exp_gate[0], w_exp_up[0], w_exp_down[0])
    y_p, y_s = _combine(code1, code2, row0, x1, route, g_final[None], ys_rec, n_p)

    return (y_p.reshape(x_prompt.shape), y_s.reshape(x_sample.shape),
            p_S[None], p_C[None], p_n[None], p_m[:, 0, :N_HEADS][None], p_cv[None],
            s_S[None], s_C[None], s_n[None], s_m[:, 0, :N_HEADS][None], s_cv[None])
```

```python
import functools

import jax
import jax.numpy as jnp
from jax import lax
from jax.experimental import pallas as pl
from jax.experimental.pallas import tpu as pltpu

F32 = jnp.float32
BF16 = jnp.bfloat16

D_MODEL = 2048
N_HEADS = 4
DK = 128
DV = 256
GLA_GATE_RANK = 16
GLA_GATE_NORM = 16.0
CONV_W = 4
QK_CH = 2 * N_HEADS * DK
N_GROUPS = 4
EXPERTS_PER_GROUP = 8
N_EXPERTS = N_GROUPS * EXPERTS_PER_GROUP
D_EXPERT = 256
EPS = 1e-6

LANES = 128
GLA_SUB = 16
GLA_SAFE_DECAY = 60.0
MIX_ROWS = 256
ROW_TILE = 512
PROJ_BIG = 6 * 1024
PROJ_CHUNK = 1024
PACK_COLS = 256
LANE_I = GLA_GATE_RANK
LANE_F = GLA_GATE_RANK + N_HEADS
ROUTER_E0 = N_GROUPS
HALF = D_MODEL // 2
REC_ROWS = HALF // LANES
EXP_TILE = 256
DMA_UNROLL = 8
SLOT_CODE_BITS = 16
SLOT_CODE = 1 << SLOT_CODE_BITS

VMEM_LIMIT = 56 * 1024 * 1024


def _dot(a, b):
    return jnp.dot(a, b, preferred_element_type=F32)


def _dot_nt(a, b):
    return lax.dot_general(a, b, (((1,), (1,)), ((), ())), preferred_element_type=F32)


def _dot_tn(a, b):
    return lax.dot_general(a, b, (((0,), (0,)), ((), ())), preferred_element_type=F32)


def _split2(x):
    hi = x.astype(BF16)
    lo = (x - hi.astype(F32)).astype(BF16)
    return hi, lo


def _split3(x):
    hi = x.astype(BF16)
    r = x - hi.astype(F32)
    mid = r.astype(BF16)
    lo = (r - mid.astype(F32)).astype(BF16)
    return hi, mid, lo


def _dot_exact_lhs(m_bf16, x):
    hi, mid, lo = _split3(x)
    return _dot(m_bf16, hi) + _dot(m_bf16, mid) + _dot(m_bf16, lo)


def _dot_hilo(a_hi, a_lo, b_hi, b_lo):
    return _dot(a_hi, b_hi) + _dot(a_lo, b_hi) + _dot(a_hi, b_lo)


def _hilo_cols(w):
    return jnp.concatenate(_split2(w), axis=1)


def _dot_hilo_cols(a_hi, a_lo, b_ref):
    p = _dot(a_hi, b_ref[...])
    return (p[:, LANES:] + _dot(a_lo, b_ref[:, :LANES])) + p[:, :LANES]


def _log_sigmoid(z):
    return jnp.minimum(z, 0.0) - jnp.log1p(jnp.exp(-jnp.abs(z)))


def _sigmoid(z):
    return 0.5 * jnp.tanh(0.5 * z) + 0.5


def _rmsnorm(x, g):
    return x * lax.rsqrt(jnp.mean(x * x, axis=-1, keepdims=True) + EPS) * g


def _pack_inproj_weight_kernel(c_gz, c_mqk, wt_ref, o_ref, buf, sem):
    j = pl.program_id(0)
    n_j = pl.num_programs(0)

    def copy(jj):
        col0 = jj * PACK_COLS
        row = pl.multiple_of(col0 + jnp.where(col0 >= c_gz, c_mqk - c_gz, 0), 16)
        return pltpu.make_async_copy(wt_ref.at[pl.ds(row, PACK_COLS)], buf.at[jj % 2], sem.at[jj % 2])

    @pl.when(j == 0)
    def _():
        copy(0).start()

    @pl.when(j + 1 < n_j)
    def _():
        copy(j + 1).start()

    copy(j).wait()
    r_i = lax.broadcasted_iota(jnp.int32, (PACK_COLS, PACK_COLS), 0)
    c_i = lax.broadcasted_iota(jnp.int32, (PACK_COLS, PACK_COLS), 1)
    eye = jnp.where(r_i == c_i, 1.0, 0.0).astype(BF16)
    o_ref[...] = _dot_tn(buf[j % 2].astype(BF16), eye).astype(o_ref.dtype)


def _pack_inproj_weight(wt, c_gz, c_mqk):
    assert c_gz % PACK_COLS == 0 and (c_mqk - c_gz) % 16 == 0
    return pl.pallas_call(
        functools.partial(_pack_inproj_weight_kernel, c_gz, c_mqk),
        grid=(PROJ_BIG // PACK_COLS,),
        in_specs=[pl.BlockSpec(memory_space=pl.ANY)],
        out_specs=pl.BlockSpec((wt.shape[1], PACK_COLS), lambda j: (0, j)),
        out_shape=jax.ShapeDtypeStruct((wt.shape[1], PROJ_BIG), BF16),
        scratch_shapes=[pltpu.VMEM((2, PACK_COLS, wt.shape[1]), F32), pltpu.SemaphoreType.DMA((2,))],
        compiler_params=pltpu.CompilerParams(dimension_semantics=("arbitrary",), vmem_limit_bytes=VMEM_LIMIT),
        name="pack_inproj_weight",
    )(wt)


def _inproj_kernel(n_prompt_tiles, xp_ref, xs_ref, g_ref, w_ref, ws_ref,
                   p_ref, ps_ref):
    i = pl.program_id(0)

    def body(x_ref):
        y = _rmsnorm(x_ref[...], g_ref[...])
        y_hi, y_lo = _split2(y)
        ps_ref[...] = _dot_hilo_cols(y_hi, y_lo, ws_ref)
        for c in range(PROJ_BIG // PROJ_CHUNK):
            cols = slice(c * PROJ_CHUNK, (c + 1) * PROJ_CHUNK)
            p_ref[:, cols] = _dot(y_hi, w_ref[:, cols]).astype(p_ref.dtype)

    @pl.when(i < n_prompt_tiles)
    def _():
        body(xp_ref)

    @pl.when(i >= n_prompt_tiles)
    def _():
        body(xs_ref)


def _inproj(xp, xs, g, w_big, ws):
    n_p, n_s = xp.shape[0], xs.shape[0]
    assert n_p % ROW_TILE == 0 and n_s == ROW_TILE
    npt = n_p // ROW_TILE
    n = n_p + n_s
    once = pl.Buffered(1)
    return pl.pallas_call(
        functools.partial(_inproj_kernel, npt),
        grid=(npt + 1,),
        in_specs=[
            pl.BlockSpec((ROW_TILE, D_MODEL), lambda i: (jnp.minimum(i, npt - 1), 0)),
            pl.BlockSpec((ROW_TILE, D_MODEL), lambda i: (0, 0)),
            pl.BlockSpec((1, D_MODEL), lambda i: (0, 0)),
            pl.BlockSpec((D_MODEL, PROJ_BIG), lambda i: (0, 0), pipeline_mode=once),
            pl.BlockSpec((D_MODEL, 2 * LANES), lambda i: (0, 0), pipeline_mode=once),
        ],
        out_specs=[
            pl.BlockSpec((ROW_TILE, PROJ_BIG), lambda i: (i, 0)),
            pl.BlockSpec((ROW_TILE, LANES), lambda i: (i, 0)),
        ],
        out_shape=[
            jax.ShapeDtypeStruct((n, PROJ_BIG), BF16),
            jax.ShapeDtypeStruct((n, LANES), F32),
        ],
        compiler_params=pltpu.CompilerParams(
            dimension_semantics=("arbitrary",), vmem_limit_bytes=VMEM_LIMIT),
        name="inproj",
    )(xp, xs, g, w_big, ws)


def _mixer_kernel(rows, gq_ref, gk_ref, gv_ref, gg_ref, mqk_ref, mv_ref, mo_ref, sm_ref,
                  s0_ref, c0_ref, n0_ref, m0_ref, cv0_ref,
                  wz_hi_ref, wz_lo_ref, bz_ref, g_gla_ref, wc_ref, bc_ref, bsm_ref, g_ml_ref,
                  o_ref, s_out_ref, c_out_ref, n_out_ref, m_out_ref, cv_out_ref,
                  st_s, c_s, n_s, m_s, cb_s, b_s):
    blk = pl.program_id(1)
    n_blk = pl.num_programs(1)

    @pl.when(blk == 0)
    def _():
        for h in range(N_HEADS):
            st_s[h] = s0_ref[0, h].T
            c_s[h] = c0_ref[0, h]
            n_s[h] = n0_ref[0, h:h + 1, :]
            m_s[h] = jnp.broadcast_to(m0_ref[0, :, h:h + 1], (1, LANES))
        cb_s[0:8, :] = jnp.zeros((8, QK_CH), F32)
        cb_s[8 - (CONV_W - 1):8, :] = cv0_ref[0]

    small = sm_ref[...]
    row_i = lax.broadcasted_iota(jnp.int32, (rows, rows), 0)
    col_i = lax.broadcasted_iota(jnp.int32, (rows, rows), 1)
    causal = col_i <= row_i

    sm_hi, sm_lo = _split2(small)
    z = _dot_hilo(sm_hi, sm_lo, wz_hi_ref[...], wz_lo_ref[...]) + bz_ref[...]
    log_a = _log_sigmoid(z) * (1.0 / GLA_GATE_NORM)
    tri = jnp.where(causal, 1.0, 0.0).astype(BF16)
    b_blk = _dot_exact_lhs(tri, log_a)
    b_last = b_blk[rows - 1:rows, :]
    factorable = jnp.min(b_last) >= -GLA_SAFE_DECAY

    def gla_finish(o, vc, gate):
        y = o * lax.rsqrt(jnp.mean(o * o, axis=-1, keepdims=True) + EPS) * g_gla_ref[:, vc]
        return (y * (gate * _sigmoid(gate))).astype(o_ref.dtype)

    @pl.when(factorable)
    def _():
        for h in range(N_HEADS):
            kc = slice(h * DK, (h + 1) * DK)
            vc = slice(h * DV, (h + 1) * DV)
            bh = b_blk[:, kc]
            qh = gq_ref[:, kc].astype(F32) * (DK ** -0.5)
            kh = gk_ref[:, kc].astype(F32)
            vh = gv_ref[:, vc]
            st = st_s[h]
            q_dec = (qh * jnp.exp(bh)).astype(BF16)
            k_inv = (kh * jnp.exp(-bh)).astype(BF16)
            a = jnp.where(causal, _dot_nt(q_dec, k_inv), 0.0)
            o = _dot(a.astype(BF16), vh) + _dot_nt(q_dec, st.astype(BF16))
            bh_end = b_last[:, kc]
            k_dec = (kh * jnp.exp(bh_end - bh)).astype(BF16)
            st_s[h] = st * jnp.exp(bh_end) + _dot_tn(vh, k_dec)
            o_ref[:, vc] = gla_finish(o, vc, gg_ref[:, vc].astype(F32))

    sub_r = lax.broadcasted_iota(jnp.int32, (GLA_SUB, LANES), 0)
    sub_l = lax.broadcasted_iota(jnp.int32, (GLA_SUB, LANES), 1)

    def gla_sub(c, carry):
        r0 = pl.multiple_of(c * GLA_SUB, GLA_SUB)
        rs = pl.ds(r0, GLA_SUB)
        for h in range(N_HEADS):
            kc = slice(h * DK, (h + 1) * DK)
            vc = slice(h * DV, (h + 1) * DV)
            bh = b_s[rs, kc]
            qh = gq_ref[rs, kc].astype(F32) * (DK ** -0.5)
            kh = gk_ref[rs, kc].astype(F32)
            vh = gv_ref[rs, vc]
            st = st_s[h]
            o = _dot_nt((qh * jnp.exp(bh)).astype(BF16), st.astype(BF16))
            a = jnp.zeros((GLA_SUB, LANES), F32)
            for s in range(GLA_SUB):
                e = jnp.exp(jnp.minimum(bh - bh[s:s + 1, :], 0.0))
                col = jnp.sum(qh * (kh[s:s + 1, :] * e), axis=1, keepdims=True)
                a = jnp.where((sub_l == s) & (sub_r >= s), col, a)
            o = o + _dot(a[:, :GLA_SUB].astype(BF16), vh)
            b_end = bh[GLA_SUB - 1:GLA_SUB, :]
            k_dec = kh * jnp.exp(b_end - bh)
            st_s[h] = st * jnp.exp(b_end) + _dot_tn(vh, k_dec.astype(BF16))
            o_ref[rs, vc] = gla_finish(o, vc, gg_ref[rs, vc].astype(F32))
        return carry

    @pl.when(jnp.logical_not(factorable))
    def _():
        same_sub = (row_i // GLA_SUB) == (col_i // GLA_SUB)
        blk_tri = jnp.where(causal & same_sub, 1.0, 0.0).astype(BF16)
        b_s[...] = _dot_exact_lhs(blk_tri, log_a)
        lax.fori_loop(0, rows // GLA_SUB, gla_sub, 0)

    cb_s[8:8 + rows, :] = mqk_ref[...].astype(F32)
    conv = bc_ref[...]
    for j in range(CONV_W):
        conv = conv + cb_s[8 - (CONV_W - 1) + j:8 - (CONV_W - 1) + j + rows, :] * wc_ref[j:j + 1, :]
    cb_s[0:8, :] = cb_s[rows:rows + 8, :]
    qk = conv * _sigmoid(conv)

    pre = small + bsm_ref[...]
    log_f = _log_sigmoid(pre)
    f_cum =_dot_exact_lhs(tri, log_f)
    eye = jnp.where(row_i == col_i, 1.0, 0.0).astype(BF16)
    f_cum_t = sum(_dot_tn(p, eye) for p in _split3(f_cum))
    pre_t = sum(_dot_tn(p, eye) for p in _split3(pre))
    lane_1 = lax.broadcasted_iota(jnp.int32, (1, LANES), 1)
    m_new = jnp.zeros((1, LANES), F32)

    for h in range(N_HEADS):
        kc = slice(h * DK, (h + 1) * DK)
        vc = slice(h * DV, (h + 1) * DV)
        f_col = f_cum[:, LANE_F + h:LANE_F + h + 1]
        i_col = pre[:, LANE_I + h:LANE_I + h + 1]
        f_row = f_cum_t[LANE_F + h:LANE_F + h + 1, :]
        i_row = pre_t[LANE_I + h:LANE_I + h + 1, :]
        dm = jnp.where(causal, f_col - f_row + i_row, -jnp.inf)
        m_prev = m_s[h][:, 0:1]
        inter = f_col + m_prev
        m_t = jnp.maximum(inter, jnp.max(dm, axis=1, keepdims=True))
        w = jnp.exp(dm - m_t)
        w_inter = jnp.exp(inter - m_t)
        q = qk[:, kc]
        k = qk[:, N_HEADS * DK + h * DK:N_HEADS * DK + (h + 1) * DK] * (DK ** -0.5)
        v = mv_ref[:, vc]
        q_b = q.astype(BF16)
        c_prev = c_s[h]
        n_prev = n_s[h]
        s_qk = _dot_nt(q_b, k.astype(BF16)) * w
        num = _dot(s_qk.astype(BF16), v) + w_inter * _dot(q_b, c_prev.astype(BF16))
        den = (jnp.sum(s_qk, axis=1, keepdims=True)
               + w_inter * jnp.sum(q * n_prev, axis=1, keepdims=True))
        hh = num / jnp.maximum(jnp.abs(den), jnp.exp(-m_t))
        m_end = m_t[rows - 1:rows, :]
        w_s = jnp.exp(f_col[rows - 1:rows, :] - f_col + i_col - m_end)
        dec = w_inter[rows - 1:rows, :]
        k_w = k * w_s
        c_s[h] = dec * c_prev + _dot_tn(k_w.astype(BF16), v)
        n_s[h] = dec * n_prev + jnp.sum(k_w, axis=0, keepdims=True)
        m_s[h] = jnp.broadcast_to(m_end, (1, LANES))
        m_new = jnp.where(lane_1 == h, m_end, m_new)
        y = hh * lax.rsqrt(jnp.mean(hh * hh, axis=-1, keepdims=True) + EPS) * g_ml_ref[:, vc]
        o_ref[:, N_HEADS * DV + h * DV:N_HEADS * DV + (h + 1) * DV] = (
            y * _sigmoid(mo_ref[:, vc].astype(F32))).astype(o_ref.dtype)

    @pl.when(blk == n_blk - 1)
    def _():
        for h in range(N_HEADS):
            s_out_ref[0, h] = st_s[h].T
            c_out_ref[0, h] = c_s[h]
            n_out_ref[0, h:h + 1, :] = n_s[h]
        m_out_ref[0] = m_new
        cv_out_ref[0] = cb_s[8 - (CONV_W - 1):8, :]


def _mixers(p_big, p_small, row0, n_streams, t_len, rows, s0, c0, n0, m0, cv0, wts):
    assert t_len % rows == 0 and row0 % rows == 0 and rows % GLA_SUB == 0
    n_blk = t_len // rows
    b0 = row0 // rows

    def rmap(col):
        return lambda s, b: (b0 + s * n_blk + b, col)

    def smap(*zeros):
        return lambda s, b: (s,) + zeros

    def wmap(s, b):
        return (0, 0)

    wz_hi, wz_lo, bz, g_gla, wc, bc, bsm, g_ml = wts
    in_specs = [
        pl.BlockSpec((rows, N_HEADS * DK), rmap(0)),
        pl.BlockSpec((rows, N_HEADS * DK), rmap(1)),
        pl.BlockSpec((rows, N_HEADS * DV), rmap(1)),
        pl.BlockSpec((rows, N_HEADS * DV), rmap(2)),
        pl.BlockSpec((rows, QK_CH), rmap(3)),
        pl.BlockSpec((rows, N_HEADS * DV), rmap(4)),
        pl.BlockSpec((rows, N_HEADS * DV), rmap(5)),
        pl.BlockSpec((rows, LANES), rmap(0)),
        pl.BlockSpec((1, N_HEADS, DK, DV), smap(0, 0, 0)),
        pl.BlockSpec((1, N_HEADS, DK, DV), smap(0, 0, 0)),
        pl.BlockSpec((1, N_HEADS, DK), smap(0, 0)),
        pl.BlockSpec((1, 1, N_HEADS), smap(0, 0)),
        pl.BlockSpec((1, CONV_W - 1, QK_CH), smap(0, 0)),
        pl.BlockSpec(wz_hi.shape, wmap), pl.BlockSpec(wz_lo.shape, wmap),
        pl.BlockSpec(bz.shape, wmap), pl.BlockSpec(g_gla.shape, wmap),
        pl.BlockSpec(wc.shape, wmap), pl.BlockSpec(bc.shape, wmap),
        pl.BlockSpec(bsm.shape, wmap), pl.BlockSpec(g_ml.shape, wmap),
    ]
    n_rows = n_streams * t_len
    out_shape = [
        jax.ShapeDtypeStruct((n_rows, D_MODEL), BF16),
        jax.ShapeDtypeStruct((n_streams, N_HEADS, DK, DV), F32),
        jax.ShapeDtypeStruct((n_streams, N_HEADS, DK, DV), F32),
        jax.ShapeDtypeStruct((n_streams, N_HEADS, DK), F32),
        jax.ShapeDtypeStruct((n_streams, 1, LANES), F32),
        jax.ShapeDtypeStruct((n_streams, CONV_W - 1, QK_CH), F32),
    ]
    out_specs = [
        pl.BlockSpec((rows, D_MODEL), lambda s, b: (s * n_blk + b, 0)),
        pl.BlockSpec((1, N_HEADS, DK, DV), smap(0, 0, 0)),
        pl.BlockSpec((1, N_HEADS, DK, DV), smap(0, 0, 0)),
        pl.BlockSpec((1, N_HEADS, DK), smap(0, 0)),
        pl.BlockSpec((1, 1, LANES), smap(0, 0)),
        pl.BlockSpec((1, CONV_W - 1, QK_CH), smap(0, 0)),
    ]
    scratch = [
        pltpu.VMEM((N_HEADS, DV, DK), F32),
        pltpu.VMEM((N_HEADS, DK, DV), F32),
        pltpu.VMEM((N_HEADS, 1, DK), F32),
        pltpu.VMEM((N_HEADS, 1, LANES), F32),
        pltpu.VMEM((rows + 8, QK_CH), F32),
        pltpu.VMEM((rows, N_HEADS * DK), F32),
    ]
    return pl.pallas_call(
        functools.partial(_mixer_kernel, rows),
        grid=(n_streams, n_blk),
        in_specs=in_specs, out_specs=out_specs, out_shape=out_shape,
        scratch_shapes=scratch,
        compiler_params=pltpu.CompilerParams(
            dimension_semantics=("arbitrary", "arbitrary"), vmem_limit_bytes=VMEM_LIMIT),
        name=f"mixers_r{rows}",
    )(p_big, p_big, p_big, p_big, p_big, p_big, p_big, p_small, s0, c0, n0, m0, cv0, *wts)


def _pack_rows(ref, val):
    word = pltpu.pack_elementwise([val[:, :HALF], val[:, HALF:]], packed_dtype=BF16)
    rows = val.shape[0]
    for j in range(REC_ROWS):
        ref[pl.ds(j, rows, stride=REC_ROWS), :] = word[:, j * LANES:(j + 1) * LANES]


def _unpack_rows(ref, rows):
    word = jnp.concatenate(
        [ref[pl.ds(j, rows, stride=REC_ROWS), :] for j in range(REC_ROWS)], axis=1)
    lo = pltpu.unpack_elementwise(word, index=0, packed_dtype=BF16, unpacked_dtype=F32)
    hi = pltpu.unpack_elementwise(word, index=1, packed_dtype=BF16, unpacked_dtype=F32)
    return lo, hi


def _outproj_kernel(n_prompt_tiles, op_ref, os_ref, xp_ref, xs_ref, w_ref, g_ref, wr_ref,
                    x1_ref, xn_ref, route_ref, code_ref, cnt_ref, cnt_s):
    i = pl.program_id(0)

    @pl.when(i == 0)
    def _():
        cnt_s[...] = jnp.zeros_like(cnt_s)

    def body(o_ref, x_ref):
        x1 = x_ref[...] + _dot(o_ref[...], w_ref[...])
        x1_ref[...] = x1
        xn = _rmsnorm(x1, g_ref[...])
        _pack_rows(xn_ref, xn)
        xn_hi, xn_lo = _split2(xn)
        lg = _dot_hilo_cols(xn_hi, xn_lo, wr_ref)
        lane = lax.broadcasted_iota(jnp.int32, lg.shape, 1).astype(F32)
        neg = -jnp.inf
        lgm = jnp.where(lane < N_GROUPS, lg, neg)
        mg = jnp.max(lgm, axis=1, keepdims=True)
        g_idx = jnp.min(jnp.where(lgm == mg, lane, float(LANES)), axis=1, keepdims=True)
        g_val = 1.0 / jnp.sum(jnp.where(lane < N_GROUPS, jnp.exp(lg - mg), 0.0), axis=1, keepdims=True)
        e0 = ROUTER_E0 + EXPERTS_PER_GROUP * g_idx
        le = jnp.where((lane >= e0) & (lane < e0 + EXPERTS_PER_GROUP), lg, neg)
        v1 = jnp.max(le, axis=1, keepdims=True)
        i1 = jnp.min(jnp.where(le == v1, lane, float(LANES)), axis=1, keepdims=True)
        le2 = jnp.where(lane == i1, neg, le)
        v2 = jnp.max(le2, axis=1, keepdims=True)
        i2 = jnp.min(jnp.where(le2 == v2, lane, float(LANES)), axis=1, keepdims=True)
        t = jnp.exp(v2 - v1)
        w1 = g_val / (1.0 + t)
        w2 = g_val * t / (1.0 + t)
        oh1 = lane == i1
        oh2 = lane == i2
        hot = jnp.where(oh1 | oh2, 1.0, 0.0)
        r_i = lax.broadcasted_iota(jnp.int32, (ROW_TILE, ROW_TILE), 0)
        c_i = lax.broadcasted_iota(jnp.int32, (ROW_TILE, ROW_TILE), 1)
        before = jnp.where(c_i < r_i, 1.0, 0.0).astype(BF16)
        seen = _dot(before, hot.astype(BF16)) + cnt_s[...]
        rank1 = jnp.sum(jnp.where(oh1, seen, 0.0), axis=1, keepdims=True)
        rank2 = jnp.sum(jnp.where(oh2, seen, 0.0), axis=1, keepdims=True)
        cnt_s[...] += jnp.sum(hot, axis=0, keepdims=True)
        code1 = (i1 - ROUTER_E0) * float(SLOT_CODE) + rank1
        code2 = (i2 - ROUTER_E0) * float(SLOT_CODE) + rank2
        route = jnp.zeros_like(lg)
        for k, col in ((0, code1), (1, code2), (4, w1), (5, w2)):
            route = jnp.where(lane == k, col, route)
        route_ref[...] = route
        code_ref[0] = route.T[0:8, :].astype(jnp.int32)

    @pl.when(i < n_prompt_tiles)
    def _():
        body(op_ref, xp_ref)

    @pl.when(i >= n_prompt_tiles)
    def _():
        body(os_ref, xs_ref)

    cnt_ref[...] = cnt_s[...]


def _outproj(o_p, o_s, xp, xs, w_out, g, wr):
    n_p, n_s = xp.shape[0], xs.shape[0]
    npt = n_p // ROW_TILE
    n = n_p + n_s

    def pmap(i):
        return (jnp.minimum(i, npt - 1), 0)

    def cmap(i):
        return (0, 0)

    def omap(i):
        return (i, 0)

    return pl.pallas_call(
        functools.partial(_outproj_kernel, npt),
        grid=(npt + 1,),
        in_specs=[
            pl.BlockSpec((ROW_TILE, D_MODEL), pmap), pl.BlockSpec((ROW_TILE, D_MODEL), cmap),
            pl.BlockSpec((ROW_TILE, D_MODEL), pmap), pl.BlockSpec((ROW_TILE, D_MODEL), cmap),
            pl.BlockSpec((D_MODEL, D_MODEL), cmap), pl.BlockSpec((1, D_MODEL), cmap),
            pl.BlockSpec((D_MODEL, 2 * LANES), cmap),
        ],
        out_specs=[
            pl.BlockSpec((ROW_TILE, D_MODEL), omap), pl.BlockSpec((ROW_TILE * REC_ROWS, LANES), omap),
            pl.BlockSpec((ROW_TILE, LANES), omap), pl.BlockSpec((1, 8, ROW_TILE), lambda i: (i, 0, 0)),
            pl.BlockSpec((1, LANES), cmap),
        ],
        out_shape=[
            jax.ShapeDtypeStruct((n, D_MODEL), F32),
            jax.ShapeDtypeStruct((n * REC_ROWS, LANES), jnp.uint32),
            jax.ShapeDtypeStruct((n, LANES), F32),
            jax.ShapeDtypeStruct((npt + 1, 8, ROW_TILE), jnp.int32),
            jax.ShapeDtypeStruct((1, LANES), F32),
        ],
        scratch_shapes=[pltpu.VMEM((1, LANES), F32)],
        compiler_params=pltpu.CompilerParams(
            dimension_semantics=("arbitrary",), vmem_limit_bytes=VMEM_LIMIT),
        name="outproj_router",
    )(o_p, o_s, xp, xs, w_out, g, wr)


def _rec(ref, idx):
    return ref.at[pl.ds(pl.multiple_of(idx * REC_ROWS, REC_ROWS), REC_ROWS)]


def _zero_records_kernel(o_ref):
    z = jnp.zeros(o_ref.shape, F32)
    o_ref[...] = pltpu.pack_elementwise([z, z], packed_dtype=BF16)


def _slot_rows_kernel(row0_ref, code_ref, pos_ref):
    code = code_ref[...]
    expert = lax.shift_right_logical(code, SLOT_CODE_BITS)
    first = jnp.zeros_like(code)
    for e in range(N_EXPERTS):
        first = jnp.where(expert == e, row0_ref[e], first)
    pos_ref[...] = first + (code & (SLOT_CODE - 1))


def _slot_rows(row0, codes):
    return pl.pallas_call(
        _slot_rows_kernel,
        grid_spec=pltpu.PrefetchScalarGridSpec(
            num_scalar_prefetch=1,
            grid=(1,),
            in_specs=[pl.BlockSpec(codes.shape, lambda i, r: (0, 0, 0))],
            out_specs=pl.BlockSpec(codes.shape, lambda i, r: (0, 0, 0)),
        ),
        out_shape=jax.ShapeDtypeStruct(codes.shape, codes.dtype),
        name="moe_slot_rows",
    )(row0, codes)


def _dispatch_kernel(t_max, p1_ref, p2_ref, padlo_ref, padn_ref, nt_ref, xn_ref, xs_ref,
                     zero_s, sem, zsem):
    step = pl.program_id(0)
    base = step * ROW_TILE

    def pad_copy(row):
        return pltpu.make_async_copy(_rec(zero_s, 0), _rec(xs_ref, row), zsem)

    def tile_copy(t):
        rows = EXP_TILE * REC_ROWS
        return pltpu.make_async_copy(zero_s, xs_ref.at[pl.ds(pl.multiple_of(t * rows, rows), rows)], zsem)

    def for_pad_rows(fn):
        def per_expert(e, c):
            lo = padlo_ref[e]
            lax.fori_loop(0, padn_ref[e], lambda r, cc: (fn(pad_copy(lo + r)), cc)[1], 0)
            return c
        lax.fori_loop(0, N_EXPERTS, per_expert, 0)
        lax.fori_loop(nt_ref[0], t_max, lambda t, c: (fn(tile_copy(t)), c)[1], 0)

    @pl.when(step == 0)
    def _():
        _zero_records_kernel(zero_s)
        for_pad_rows(lambda cp: cp.start(priority=1))

    def copies(r):
        src = _rec(xn_ref, r)
        return (pltpu.make_async_copy(src, _rec(xs_ref, p1_ref[base + r]), sem),
                pltpu.make_async_copy(src, _rec(xs_ref, p2_ref[base + r]), sem))

    def start(g, c):
        for u in range(DMA_UNROLL):
            for prio, cp in enumerate(copies(g * DMA_UNROLL + u)):
                cp.start(priority=prio)
        return c

    def wait(g, c):
        for u in range(DMA_UNROLL):
            for cp in copies(g * DMA_UNROLL + u):
                cp.wait()
        return c

    lax.fori_loop(0, ROW_TILE // DMA_UNROLL, start, 0)
    lax.fori_loop(0, ROW_TILE // DMA_UNROLL, wait, 0)

    @pl.when(step == 0)
    def _():
        for_pad_rows(lambda cp: cp.wait())


def _dispatch(pos1, pos2, pad_lo, pad_n, n_tiles, xn_rec, t_max):
    n = pos1.shape[0]
    return pl.pallas_call(
        functools.partial(_dispatch_kernel, t_max),
        grid_spec=pltpu.PrefetchScalarGridSpec(
            num_scalar_prefetch=5,
            grid=(n // ROW_TILE,),
            in_specs=[pl.BlockSpec((ROW_TILE * REC_ROWS, LANES), lambda i, *_: (i, 0))],
            out_specs=pl.BlockSpec(memory_space=pl.ANY),
            scratch_shapes=[pltpu.VMEM((EXP_TILE * REC_ROWS, LANES), jnp.uint32),
                            pltpu.SemaphoreType.DMA, pltpu.SemaphoreType.DMA],
        ),
        out_shape=jax.ShapeDtypeStruct((t_max * EXP_TILE * REC_ROWS, LANES), jnp.uint32),
        compiler_params=pltpu.CompilerParams(
            dimension_semantics=("arbitrary",), vmem_limit_bytes=VMEM_LIMIT),
        name="moe_dispatch",
    )(pos1, pos2, pad_lo, pad_n, n_tiles, xn_rec)


def _experts_kernel(te_ref, nt_ref, xs_ref, wg_ref, wu_ref, wd_ref, ys_ref, wg_s, wu_s, wd_s):
    t = pl.program_id(0)
    prev = te_ref[jnp.maximum(t - 1, 0)]

    @pl.when((t == 0) | (te_ref[t] != prev))
    def _():
        wg_s[...] = wg_ref[0].astype(BF16)
        wu_s[...] = wu_ref[0].astype(BF16)
        wd_s[...] = wd_ref[0].astype(BF16)

    @pl.when(t < nt_ref[0])
    def _():
        lo, hi = _unpack_rows(xs_ref, EXP_TILE)
        x = jnp.concatenate([lo.astype(BF16), hi.astype(BF16)], axis=1)
        hg = _dot(x, wg_s[...])
        hu = _dot(x, wu_s[...])
        act = hg * _sigmoid(hg) * hu
        _pack_rows(ys_ref, _dot(act.astype(BF16), wd_s[...]))

    @pl.when(t >= nt_ref[0])
    def _():
        _zero_records_kernel(ys_ref)


def _experts(tile_expert, n_tiles, xs, wg, wu, wd):
    t_max = tile_expert.shape[0]

    def tmap(t, te, nt):
        return (jnp.minimum(t, nt[0] - 1), 0)

    def wmap(t, te, nt):
        return (te[t], 0, 0)

    return pl.pallas_call(
        _experts_kernel,
        grid_spec=pltpu.PrefetchScalarGridSpec(
            num_scalar_prefetch=2,
            grid=(t_max,),
            in_specs=[pl.BlockSpec((EXP_TILE * REC_ROWS, LANES), tmap),
                      pl.BlockSpec((1, D_MODEL, D_EXPERT), wmap),
                      pl.BlockSpec((1, D_MODEL, D_EXPERT), wmap),
                      pl.BlockSpec((1, D_EXPERT, D_MODEL), wmap)],
            out_specs=pl.BlockSpec((EXP_TILE * REC_ROWS, LANES), lambda t, te, nt: (t, 0)),
            scratch_shapes=[pltpu.VMEM((D_MODEL, D_EXPERT), BF16), pltpu.VMEM((D_MODEL, D_EXPERT), BF16),
                            pltpu.VMEM((D_EXPERT, D_MODEL), BF16)],
        ),
        out_shape=jax.ShapeDtypeStruct(xs.shape, xs.dtype),
        compiler_params=pltpu.CompilerParams(
            dimension_semantics=("arbitrary",), vmem_limit_bytes=VMEM_LIMIT),
        name="moe_experts",
    )(tile_expert, n_tiles, xs, wg, wu, wd)


def _combine_kernel(n_prompt_tiles, p1_ref, p2_ref, x1_ref, route_ref, g_ref, ys_ref,
                    yp_ref, ysm_ref, a_s, b_s, sem):
    i = pl.program_id(0)
    n_steps = pl.num_programs(0)

    def copies(step, slot, r):
        tok = step * ROW_TILE + r
        return (pltpu.make_async_copy(_rec(ys_ref, p1_ref[tok]), _rec(a_s.at[slot], r), sem.at[slot]),
                pltpu.make_async_copy(_rec(ys_ref, p2_ref[tok]), _rec(b_s.at[slot], r), sem.at[slot]))

    def start_all(step, slot):
        def start(g, c):
            for u in range(DMA_UNROLL):
                for prio, cp in enumerate(copies(step, slot, g * DMA_UNROLL + u)):
                    cp.start(priority=prio)
            return c
        lax.fori_loop(0, ROW_TILE // DMA_UNROLL, start, 0)

    def wait_all(step, slot):
        def wait(g, c):
            for u in range(DMA_UNROLL):
                for cp in copies(step, slot, g * DMA_UNROLL + u):
                    cp.wait()
            return c
        lax.fori_loop(0, ROW_TILE // DMA_UNROLL, wait, 0)

    slot = i % 2

    @pl.when(i == 0)
    def _():
        start_all(0, 0)

    @pl.when(i + 1 < n_steps)
    def _():
        start_all(i + 1, 1 - slot)

    wait_all(i, slot)
    a_lo, a_hi = _unpack_rows(a_s.at[slot], ROW_TILE)
    b_lo, b_hi = _unpack_rows(b_s.at[slot], ROW_TILE)
    route = route_ref[...]
    w1 = route[:, 4:5]
    w2 = route[:, 5:6]
    moe = jnp.concatenate([w1 * a_lo + w2 * b_lo, w1 * a_hi + w2 * b_hi], axis=1)
    y = _rmsnorm(x1_ref[...] + moe, g_ref[...])

    @pl.when(i < n_prompt_tiles)
    def _():
        yp_ref[...] = y

    @pl.when(i >= n_prompt_tiles)
    def _():
        ysm_ref[...] = y


def _combine(pos1, pos2, x1, route, g, ys, n_p):
    n = x1.shape[0]
    npt = n_p // ROW_TILE

    def omap(i, *_):
        return (i, 0)

    return pl.pallas_call(
        functools.partial(_combine_kernel, npt),
        grid_spec=pltpu.PrefetchScalarGridSpec(
            num_scalar_prefetch=2,
            grid=(n // ROW_TILE,),
            in_specs=[pl.BlockSpec((ROW_TILE, D_MODEL), omap),
                      pl.BlockSpec((ROW_TILE, LANES), omap),
                      pl.BlockSpec((1, D_MODEL), lambda i, *_: (0, 0)),
                      pl.BlockSpec(memory_space=pl.ANY)],
            out_specs=[pl.BlockSpec((ROW_TILE, D_MODEL), lambda i, *_: (jnp.minimum(i, npt - 1), 0)),
                       pl.BlockSpec((ROW_TILE, D_MODEL), lambda i, *_: (jnp.maximum(i - npt, 0), 0))],
            scratch_shapes=[pltpu.VMEM((2, ROW_TILE * REC_ROWS, LANES), jnp.uint32),
                            pltpu.VMEM((2, ROW_TILE * REC_ROWS, LANES), jnp.uint32),
                            pltpu.SemaphoreType.DMA((2,))],
        ),
        out_shape=[jax.ShapeDtypeStruct((n_p, D_MODEL), F32),
                   jax.ShapeDtypeStruct((n - n_p, D_MODEL), F32)],
        compiler_params=pltpu.CompilerParams(
            dimension_semantics=("arbitrary",), vmem_limit_bytes=VMEM_LIMIT),
        name="moe_combine",
    )(pos1, pos2, x1, route, g, ys)


def _pad_lanes(w):
    return jnp.pad(w, ((0, 0), (0, LANES - w.shape[1])))


def _moe_plan(counts, n_tokens):
    cnt = counts[0, ROUTER_E0:ROUTER_E0 + N_EXPERTS].astype(jnp.int32)
    tiles = (cnt + EXP_TILE - 1) // EXP_TILE
    tile_end = jnp.cumsum(tiles)
    row0 = (tile_end - tiles) * EXP_TILE
    t_max = 2 * n_tokens // EXP_TILE + N_EXPERTS
    tile_ids = jnp.arange(t_max, dtype=jnp.int32)
    tile_expert = jnp.minimum(
        jnp.sum((tile_ids[:, None] >= tile_end[None, :]).astype(jnp.int32), axis=1), N_EXPERTS - 1)
    return row0, row0 + cnt, tiles * EXP_TILE - cnt, tile_expert, tile_end[-1:].astype(jnp.int32), t_max


def kernel(x_prompt, x_sample, state_gla_S, state_mlstm_C, state_mlstm_n, state_mlstm_m, cache_mlstm_conv, g_mix_norm, w_in, w_gla_gate_up, b_gla_gate_up, g_gla_out, w_mlstm_conv, b_mlstm_conv, b_mlstm_i, b_mlstm_f, g_mlstm_out, w_out, g_ffn_norm, w_router_group, w_router_expert, w_exp_gate, w_exp_up, w_exp_down, g_final):
    depth = w_in.shape[0]
    assert depth == 1
    bp, t_p, _ = x_prompt.shape
    bs, t_s, _ = x_sample.shape
    assert bp == 1
    xp = x_prompt.reshape(bp * t_p, D_MODEL)
    xs = x_sample.reshape(bs * t_s, D_MODEL)
    n_p = xp.shape[0]

    wt = jnp.transpose(w_in[0])
    c_gz = 3072
    c_mqk = c_gz + GLA_GATE_RANK
    c_mi = c_mqk + 3072
    w_big = _pack_inproj_weight(wt, c_gz, c_mqk)
    ws = _hilo_cols(_pad_lanes(jnp.concatenate([wt[c_gz:c_mqk], wt[c_mi:]], axis=0).T))
    wz = jnp.pad(w_gla_gate_up[0], ((0, LANES - GLA_GATE_RANK), (0, 0)))
    wz_hi, wz_lo = _split2(wz)
    bsm = _pad_lanes(jnp.concatenate(
        [jnp.zeros((1, GLA_GATE_RANK), F32), b_mlstm_i[0][None], b_mlstm_f[0][None]], axis=1))
    mix_w = (wz_hi, wz_lo, b_gla_gate_up[0][None], g_gla_out[0][None], w_mlstm_conv[0],
             b_mlstm_conv[0][None], bsm, g_mlstm_out[0][None])
    wr = _hilo_cols(_pad_lanes(jnp.concatenate([w_router_group[0], w_router_expert[0]], axis=1)))

    p_big, p_small = _inproj(xp, xs, g_mix_norm[0][None], w_big, ws)

    dt = x_prompt.dtype
    z_s = jnp.zeros((bp, N_HEADS, DK, DV), dt)
    z_n = jnp.zeros((bp, N_HEADS, DK), dt)
    z_m = jnp.zeros((bp, 1, N_HEADS), dt)
    z_cv = jnp.zeros((bp, CONV_W - 1, QK_CH), dt)
    o_p, p_S, p_C, p_n, p_m, p_cv = _mixers(
        p_big, p_small, 0, bp, t_p, MIX_ROWS, z_s, z_s, z_n, z_m, z_cv, mix_w)
    o_s, s_S, s_C, s_n, s_m, s_cv = _mixers(
        p_big, p_small, n_p, bs, t_s, t_s, state_gla_S[0], state_mlstm_C[0], state_mlstm_n[0],
        state_mlstm_m[0][:, None, :], cache_mlstm_conv[0], mix_w)

    x1, xn_rec, route, codes, counts = _outproj(
        o_p, o_s, xp, xs, w_out[0].astype(BF16), g_ffn_norm[0][None], wr)
    row0, pad_lo, pad_n, tile_expert, n_tiles, t_max = _moe_plan(counts, x1.shape[0])
    pos = _slot_rows(row0, codes)
    pos1 = pos[:, 0, :].reshape(-1)
    pos2 = pos[:, 1, :].reshape(-1)
    xs_rec = _dispatch(pos1, pos2, pad_lo, pad_n, n_tiles, xn_rec, t_max)
    ys_rec = _experts(tile_expert, n_tiles, xs_rec, w_exp_gate[0], w_exp_up[0], w_exp_down[0])
    y_p, y_s = _combine(pos1, pos2, x1, route, g_final[None], ys_rec, n_p)

    return (y_p.reshape(x_prompt.shape), y_s.reshape(x_sample.shape),
            p_S[None], p_C[None], p_n[None], p_m[:, 0, :N_HEADS][None], p_cv[None],
            s_S[None], s_C[None], s_n[None], s_m[:, 0, :N_HEADS][None], s_cv[None])
```

```python
import functools

import jax
import jax.numpy as jnp
from jax import lax
from jax.experimental import pallas as pl
from jax.experimental.pallas import tpu as pltpu

F32 = jnp.float32
BF16 = jnp.bfloat16

D_MODEL = 2048
N_HEADS = 4
DK = 128
DV = 256
GLA_GATE_RANK = 16
GLA_GATE_NORM = 16.0
CONV_W = 4
QK_CH = 2 * N_HEADS * DK
N_GROUPS = 4
EXPERTS_PER_GROUP = 8
N_EXPERTS = N_GROUPS * EXPERTS_PER_GROUP
D_EXPERT = 256
EPS = 1e-6

LANES = 128
GLA_SUB = 16
GLA_SAFE_DECAY = 60.0
MIX_ROWS = 256
ROW_TILE = 512
PROJ_BIG = 6 * 1024
PROJ_CHUNK = 1024
PACK_COLS = 256
LANE_I = GLA_GATE_RANK
LANE_F = GLA_GATE_RANK + N_HEADS
ROUTER_E0 = N_GROUPS
HALF = D_MODEL // 2
REC_ROWS = HALF // LANES
EXP_TILE = 256
DMA_UNROLL = 8
SLOT_CODE_BITS = 16
SLOT_CODE = 1 << SLOT_CODE_BITS

VMEM_LIMIT = 56 * 1024 * 1024
VMEM_LIMIT_FUSED = 62 * 1024 * 1024


def _dot(a, b):
    return jnp.dot(a, b, preferred_element_type=F32)


def _dot_nt(a, b):
    return lax.dot_general(a, b, (((1,), (1,)), ((), ())), preferred_element_type=F32)


def _dot_tn(a, b):
    return lax.dot_general(a, b, (((0,), (0,)), ((), ())), preferred_element_type=F32)


def _split2(x):
    hi = x.astype(BF16)
    lo = (x - hi.astype(F32)).astype(BF16)
    return hi, lo


def _split3(x):
    hi = x.astype(BF16)
    r = x - hi.astype(F32)
    mid = r.astype(BF16)
    lo = (r - mid.astype(F32)).astype(BF16)
    return hi, mid, lo


def _dot_exact_lhs(m_bf16, x):
    hi, mid, lo = _split3(x)
    return _dot(m_bf16, hi) + _dot(m_bf16, mid) + _dot(m_bf16, lo)


def _dot_hilo(a_hi, a_lo, b_hi, b_lo):
    return _dot(a_hi, b_hi) + _dot(a_lo, b_hi) + _dot(a_hi, b_lo)


def _hilo_cols(w):
    return jnp.concatenate(_split2(w), axis=1)


def _dot_hilo_cols(a_hi, a_lo, b_ref):
    p = _dot(a_hi, b_ref[...])
    return (p[:, LANES:] + _dot(a_lo, b_ref[:, :LANES])) + p[:, :LANES]


def _log_sigmoid(z):
    return jnp.minimum(z, 0.0) - jnp.log1p(jnp.exp(-jnp.abs(z)))


def _sigmoid(z):
    return 0.5 * jnp.tanh(0.5 * z) + 0.5


def _rmsnorm(x, g):
    return x * lax.rsqrt(jnp.mean(x * x, axis=-1, keepdims=True) + EPS) * g


def _pack_inproj_weight_kernel(c_gz, c_mqk, wt_ref, o_ref, buf, sem):
    j = pl.program_id(0)
    n_j = pl.num_programs(0)

    def copy(jj):
        col0 = jj * PACK_COLS
        row = pl.multiple_of(col0 + jnp.where(col0 >= c_gz, c_mqk - c_gz, 0), 16)
        return pltpu.make_async_copy(wt_ref.at[pl.ds(row, PACK_COLS)], buf.at[jj % 2], sem.at[jj % 2])

    @pl.when(j == 0)
    def _():
        copy(0).start()

    @pl.when(j + 1 < n_j)
    def _():
        copy(j + 1).start()

    copy(j).wait()
    r_i = lax.broadcasted_iota(jnp.int32, (PACK_COLS, PACK_COLS), 0)
    c_i = lax.broadcasted_iota(jnp.int32, (PACK_COLS, PACK_COLS), 1)
    eye = jnp.where(r_i == c_i, 1.0, 0.0).astype(BF16)
    o_ref[...] = _dot_tn(buf[j % 2].astype(BF16), eye).astype(o_ref.dtype)


def _pack_inproj_weight(wt, c_gz, c_mqk):
    assert c_gz % PACK_COLS == 0 and (c_mqk - c_gz) % 16 == 0
    return pl.pallas_call(
        functools.partial(_pack_inproj_weight_kernel, c_gz, c_mqk),
        grid=(PROJ_BIG // PACK_COLS,),
        in_specs=[pl.BlockSpec(memory_space=pl.ANY)],
        out_specs=pl.BlockSpec((wt.shape[1], PACK_COLS), lambda j: (0, j)),
        out_shape=jax.ShapeDtypeStruct((wt.shape[1], PROJ_BIG), BF16),
        scratch_shapes=[pltpu.VMEM((2, PACK_COLS, wt.shape[1]), F32), pltpu.SemaphoreType.DMA((2,))],
        compiler_params=pltpu.CompilerParams(dimension_semantics=("arbitrary",), vmem_limit_bytes=VMEM_LIMIT),
        name="pack_inproj_weight",
    )(wt)


def _project(x_ref, g_ref, w_ref, ws_ref, p_ref, ps_ref, h_ref, chunks, with_norm):
    if with_norm:
        y_hi, y_lo = _split2(_rmsnorm(x_ref[...], g_ref[...]))
        h_ref[...] = y_hi
        ps_ref[...] = _dot_hilo_cols(y_hi, y_lo, ws_ref)
    else:
        y_hi = h_ref[...]
    for c in chunks:
        cols = slice(c * PROJ_CHUNK, (c + 1) * PROJ_CHUNK)
        p_ref[:, cols] = _dot(y_hi, w_ref[:, cols]).astype(p_ref.dtype)


PROJ_CHUNKS = tuple(range(PROJ_BIG // PROJ_CHUNK))


def _inproj_kernel(x_ref, g_ref, w_ref, ws_ref, p_ref, ps_ref, h_s):
    _project(x_ref, g_ref, w_ref, ws_ref, p_ref, ps_ref, h_s, PROJ_CHUNKS, True)


def _inproj(x, g, w_big, ws):
    n = x.shape[0]
    assert n % ROW_TILE == 0
    once = pl.Buffered(1)
    return pl.pallas_call(
        _inproj_kernel,
        grid=(n // ROW_TILE,),
        in_specs=[
            pl.BlockSpec((ROW_TILE, D_MODEL), lambda i: (i, 0)),
            pl.BlockSpec((1, D_MODEL), lambda i: (0, 0)),
            pl.BlockSpec((D_MODEL, PROJ_BIG), lambda i: (0, 0), pipeline_mode=once),
            pl.BlockSpec((D_MODEL, 2 * LANES), lambda i: (0, 0), pipeline_mode=once),
        ],
        out_specs=[
            pl.BlockSpec((ROW_TILE, PROJ_BIG), lambda i: (i, 0)),
            pl.BlockSpec((ROW_TILE, LANES), lambda i: (i, 0)),
        ],
        out_shape=[
            jax.ShapeDtypeStruct((n, PROJ_BIG), BF16),
            jax.ShapeDtypeStruct((n, LANES), F32),
        ],
        scratch_shapes=[pltpu.VMEM((ROW_TILE, D_MODEL), BF16)],
        compiler_params=pltpu.CompilerParams(
            dimension_semantics=("arbitrary",), vmem_limit_bytes=VMEM_LIMIT),
        name="inproj",
    )(x, g, w_big, ws)


def _mixer_kernel(rows, side_work, gq_ref, gk_ref, gv_ref, gg_ref, mqk_ref, mv_ref, mo_ref, sm_ref,
                  s0_ref, c0_ref, n0_ref, m0_ref, cv0_ref,
                  wz_hi_ref, wz_lo_ref, bz_ref, g_gla_ref, wc_ref, bc_ref, bsm_ref, g_ml_ref,
                  o_ref, s_out_ref, c_out_ref, n_out_ref, m_out_ref, cv_out_ref,
                  st_s, c_s, n_s, m_s, cb_s, b_s):
    blk = pl.program_id(1)
    n_blk = pl.num_programs(1)

    @pl.when(blk == 0)
    def _():
        for h in range(N_HEADS):
            st_s[h] = s0_ref[0, h].T
            c_s[h] = c0_ref[0, h]
            n_s[h] = n0_ref[0, h:h + 1, :]
            m_s[h] = jnp.broadcast_to(m0_ref[0, :, h:h + 1], (1, LANES))
        cb_s[0:8, :] = jnp.zeros((8, QK_CH), F32)
        cb_s[8 - (CONV_W - 1):8, :] = cv0_ref[0]

    small = sm_ref[...]
    row_i = lax.broadcasted_iota(jnp.int32, (rows, rows), 0)
    col_i = lax.broadcasted_iota(jnp.int32, (rows, rows), 1)
    causal = col_i <= row_i

    sm_hi, sm_lo = _split2(small)
    z = _dot_hilo(sm_hi, sm_lo, wz_hi_ref[...], wz_lo_ref[...]) + bz_ref[...]
    log_a = _log_sigmoid(z) * (1.0 / GLA_GATE_NORM)
    tri = jnp.where(causal, 1.0, 0.0).astype(BF16)
    b_blk = _dot_exact_lhs(tri, log_a)
    b_last = b_blk[rows - 1:rows, :]
    factorable = jnp.min(b_last) >= -GLA_SAFE_DECAY
    side_work[0]()

    def gla_finish(o, vc, gate):
        y = o * lax.rsqrt(jnp.mean(o * o, axis=-1, keepdims=True) + EPS) * g_gla_ref[:, vc]
        return (y * (gate * _sigmoid(gate))).astype(o_ref.dtype)

    @pl.when(factorable)
    def _():
        for h in range(N_HEADS):
            kc = slice(h * DK, (h + 1) * DK)
            vc = slice(h * DV, (h + 1) * DV)
            bh = b_blk[:, kc]
            qh = gq_ref[:, kc].astype(F32) * (DK ** -0.5)
            kh = gk_ref[:, kc].astype(F32)
            vh = gv_ref[:, vc]
            st = st_s[h]
            q_dec = (qh * jnp.exp(bh)).astype(BF16)
            k_inv = (kh * jnp.exp(-bh)).astype(BF16)
            a = jnp.where(causal, _dot_nt(q_dec, k_inv), 0.0)
            o = _dot(a.astype(BF16), vh) + _dot_nt(q_dec, st.astype(BF16))
            bh_end = b_last[:, kc]
            k_dec = (kh * jnp.exp(bh_end - bh)).astype(BF16)
            st_s[h] = st * jnp.exp(bh_end) + _dot_tn(vh, k_dec)
            o_ref[:, vc] = gla_finish(o, vc, gg_ref[:, vc].astype(F32))

    sub_r = lax.broadcasted_iota(jnp.int32, (GLA_SUB, LANES), 0)
    sub_l = lax.broadcasted_iota(jnp.int32, (GLA_SUB, LANES), 1)

    def gla_sub(c, carry):
        r0 = pl.multiple_of(c * GLA_SUB, GLA_SUB)
        rs = pl.ds(r0, GLA_SUB)
        for h in range(N_HEADS):
            kc = slice(h * DK, (h + 1) * DK)
            vc = slice(h * DV, (h + 1) * DV)
            bh = b_s[rs, kc]
            qh = gq_ref[rs, kc].astype(F32) * (DK ** -0.5)
            kh = gk_ref[rs, kc].astype(F32)
            vh = gv_ref[rs, vc]
            st = st_s[h]
            o = _dot_nt((qh * jnp.exp(bh)).astype(BF16), st.astype(BF16))
            a = jnp.zeros((GLA_SUB, LANES), F32)
            for s in range(GLA_SUB):
                e = jnp.exp(jnp.minimum(bh - bh[s:s + 1, :], 0.0))
                col = jnp.sum(qh * (kh[s:s + 1, :] * e), axis=1, keepdims=True)
                a = jnp.where((sub_l == s) & (sub_r >= s), col, a)
            o = o + _dot(a[:, :GLA_SUB].astype(BF16), vh)
            b_end = bh[GLA_SUB - 1:GLA_SUB, :]
            k_dec = kh * jnp.exp(b_end - bh)
            st_s[h] = st * jnp.exp(b_end) + _dot_tn(vh, k_dec.astype(BF16))
            o_ref[rs, vc] = gla_finish(o, vc, gg_ref[rs, vc].astype(F32))
        return carry

    @pl.when(jnp.logical_not(factorable))
    def _():
        same_sub = (row_i // GLA_SUB) == (col_i // GLA_SUB)
        blk_tri = jnp.where(causal & same_sub, 1.0, 0.0).astype(BF16)
        b_s[...] = _dot_exact_lhs(blk_tri, log_a)
        lax.fori_loop(0, rows // GLA_SUB, gla_sub, 0)

    cb_s[8:8 + rows, :] = mqk_ref[...].astype(F32)
    conv = bc_ref[...]
    for j in range(CONV_W):
        conv = conv + cb_s[8 - (CONV_W - 1) + j:8 - (CONV_W - 1) + j + rows, :] * wc_ref[j:j + 1, :]
    cb_s[0:8, :] = cb_s[rows:rows + 8, :]
    qk = conv * _sigmoid(conv)

    pre = small + bsm_ref[...]
    log_f = _log_sigmoid(pre)
    f_cum =_dot_exact_lhs(tri, log_f)
    eye = jnp.where(row_i == col_i, 1.0, 0.0).astype(BF16)
    f_cum_t = sum(_dot_tn(p, eye) for p in _split3(f_cum))
    pre_t = sum(_dot_tn(p, eye) for p in _split3(pre))
    side_work[1]()
    lane_1 = lax.broadcasted_iota(jnp.int32, (1, LANES), 1)
    m_new = jnp.zeros((1, LANES), F32)

    for h in range(N_HEADS):
        kc = slice(h * DK, (h + 1) * DK)
        vc = slice(h * DV, (h + 1) * DV)
        f_col = f_cum[:, LANE_F + h:LANE_F + h + 1]
        i_col = pre[:, LANE_I + h:LANE_I + h + 1]
        f_row = f_cum_t[LANE_F + h:LANE_F + h + 1, :]
        i_row = pre_t[LANE_I + h:LANE_I + h + 1, :]
        dm = jnp.where(causal, f_col - f_row + i_row, -jnp.inf)
        m_prev = m_s[h][:, 0:1]
        inter = f_col + m_prev
        m_t = jnp.maximum(inter, jnp.max(dm, axis=1, keepdims=True))
        w = jnp.exp(dm - m_t)
        w_inter = jnp.exp(inter - m_t)
        q = qk[:, kc]
        k = qk[:, N_HEADS * DK + h * DK:N_HEADS * DK + (h + 1) * DK] * (DK ** -0.5)
        v = mv_ref[:, vc]
        q_b = q.astype(BF16)
        c_prev = c_s[h]
        n_prev = n_s[h]
        s_qk = _dot_nt(q_b, k.astype(BF16)) * w
        num = _dot(s_qk.astype(BF16), v) + w_inter * _dot(q_b, c_prev.astype(BF16))
        den = (jnp.sum(s_qk, axis=1, keepdims=True)
               + w_inter * jnp.sum(q * n_prev, axis=1, keepdims=True))
        hh = num / jnp.maximum(jnp.abs(den), jnp.exp(-m_t))
        m_end = m_t[rows - 1:rows, :]
        w_s = jnp.exp(f_col[rows - 1:rows, :] - f_col + i_col - m_end)
        dec = w_inter[rows - 1:rows, :]
        k_w = k * w_s
        c_s[h] = dec * c_prev + _dot_tn(k_w.astype(BF16), v)
        side_work[2 + h]()
        n_s[h] = dec * n_prev + jnp.sum(k_w, axis=0, keepdims=True)
        m_s[h] = jnp.broadcast_to(m_end, (1, LANES))
        m_new = jnp.where(lane_1 == h, m_end, m_new)
        y = hh * lax.rsqrt(jnp.mean(hh * hh, axis=-1, keepdims=True) + EPS) * g_ml_ref[:, vc]
        o_ref[:, N_HEADS * DV + h * DV:N_HEADS * DV + (h + 1) * DV] = (
            y * _sigmoid(mo_ref[:, vc].astype(F32))).astype(o_ref.dtype)

    @pl.when(blk == n_blk - 1)
    def _():
        for h in range(N_HEADS):
            s_out_ref[0, h] = st_s[h].T
            c_out_ref[0, h] = c_s[h]
            n_out_ref[0, h:h + 1, :] = n_s[h]
        m_out_ref[0] = m_new
        cv_out_ref[0] = cb_s[8 - (CONV_W - 1):8, :]


def _mixers(p_big, p_small, row0, n_streams, t_len, rows, s0, c0, n0, m0, cv0, wts):
    assert t_len % rows == 0 and row0 % rows == 0 and rows % GLA_SUB == 0
    n_blk = t_len // rows
    b0 = row0 // rows

    def rmap(col):
        return lambda s, b: (b0 + s * n_blk + b, col)

    def smap(*zeros):
        return lambda s, b: (s,) + zeros

    def wmap(s, b):
        return (0, 0)

    wz_hi, wz_lo, bz, g_gla, wc, bc, bsm, g_ml = wts
    in_specs = [
        pl.BlockSpec((rows, N_HEADS * DK), rmap(0)),
        pl.BlockSpec((rows, N_HEADS * DK), rmap(1)),
        pl.BlockSpec((rows, N_HEADS * DV), rmap(1)),
        pl.BlockSpec((rows, N_HEADS * DV), rmap(2)),
        pl.BlockSpec((rows, QK_CH), rmap(3)),
        pl.BlockSpec((rows, N_HEADS * DV), rmap(4)),
        pl.BlockSpec((rows, N_HEADS * DV), rmap(5)),
        pl.BlockSpec((rows, LANES), rmap(0)),
        pl.BlockSpec((1, N_HEADS, DK, DV), smap(0, 0, 0)),
        pl.BlockSpec((1, N_HEADS, DK, DV), smap(0, 0, 0)),
        pl.BlockSpec((1, N_HEADS, DK), smap(0, 0)),
        pl.BlockSpec((1, 1, N_HEADS), smap(0, 0)),
        pl.BlockSpec((1, CONV_W - 1, QK_CH), smap(0, 0)),
        pl.BlockSpec(wz_hi.shape, wmap), pl.BlockSpec(wz_lo.shape, wmap),
        pl.BlockSpec(bz.shape, wmap), pl.BlockSpec(g_gla.shape, wmap),
        pl.BlockSpec(wc.shape, wmap), pl.BlockSpec(bc.shape, wmap),
        pl.BlockSpec(bsm.shape, wmap), pl.BlockSpec(g_ml.shape, wmap),
    ]
    n_rows = n_streams * t_len
    out_shape = [
        jax.ShapeDtypeStruct((n_rows, D_MODEL), BF16),
        jax.ShapeDtypeStruct((n_streams, N_HEADS, DK, DV), F32),
        jax.ShapeDtypeStruct((n_streams, N_HEADS, DK, DV), F32),
        jax.ShapeDtypeStruct((n_streams, N_HEADS, DK), F32),
        jax.ShapeDtypeStruct((n_streams, 1, LANES), F32),
        jax.ShapeDtypeStruct((n_streams, CONV_W - 1, QK_CH), F32),
    ]
    out_specs = [
        pl.BlockSpec((rows, D_MODEL), lambda s, b: (s * n_blk + b, 0)),
        pl.BlockSpec((1, N_HEADS, DK, DV), smap(0, 0, 0)),
        pl.BlockSpec((1, N_HEADS, DK, DV), smap(0, 0, 0)),
        pl.BlockSpec((1, N_HEADS, DK), smap(0, 0)),
        pl.BlockSpec((1, 1, LANES), smap(0, 0)),
        pl.BlockSpec((1, CONV_W - 1, QK_CH), smap(0, 0)),
    ]
    scratch = [
        pltpu.VMEM((N_HEADS, DV, DK), F32),
        pltpu.VMEM((N_HEADS, DK, DV), F32),
        pltpu.VMEM((N_HEADS, 1, DK), F32),
        pltpu.VMEM((N_HEADS, 1, LANES), F32),
        pltpu.VMEM((rows + 8, QK_CH), F32),
        pltpu.VMEM((rows, N_HEADS * DK), F32),
    ]
    return pl.pallas_call(
        functools.partial(_mixer_kernel, rows, (lambda: None,) * (2 + N_HEADS)),
        grid=(n_streams, n_blk),
        in_specs=in_specs, out_specs=out_specs, out_shape=out_shape,
        scratch_shapes=scratch,
        compiler_params=pltpu.CompilerParams(
            dimension_semantics=("arbitrary", "arbitrary"), vmem_limit_bytes=VMEM_LIMIT),
        name=f"mixers_r{rows}",
    )(p_big, p_big, p_big, p_big, p_big, p_big, p_big, p_small, s0, c0, n0, m0, cv0, *wts)


def _prompt_kernel(rows, x0_ref, xn_ref, g_ref, w_ref, ws_ref, *refs):
    mixer_refs, (p_buf, ps_buf, h_s) = refs[:-3], refs[-3:]
    blk = pl.program_id(1)
    cur = blk % 2

    def project(x_ref, slot, chunks, with_norm):
        _project(x_ref, g_ref, w_ref, ws_ref, p_buf.at[slot], ps_buf.at[slot], h_s, chunks, with_norm)

    @pl.when(blk == 0)
    def _():
        project(x0_ref, 0, PROJ_CHUNKS, True)

    p_cur = p_buf.at[cur]
    c0 = N_HEADS * DK
    views = [p_cur.at[:, lo:hi] for lo, hi in
             ((0, c0), (c0, 2 * c0), (1024, 2048), (2048, 3072), (3072, 4096), (4096, 5120), (5120, 6144))]
    side = ((lambda: project(xn_ref, 1 - cur, PROJ_CHUNKS[:2], True),)
            + tuple(functools.partial(project, xn_ref, 1 - cur, (c,), False) for c in PROJ_CHUNKS[2:])
            + (lambda: None,))
    assert len(side) == 2 + N_HEADS
    _mixer_kernel(rows, side, *views, ps_buf.at[cur], *mixer_refs)


def _prompt_mixers(xp, g, w_big, ws, rows, s0, c0, n0, m0, cv0, wts):
    t_len = xp.shape[0]
    assert t_len % rows == 0
    n_blk = t_len // rows
    once = pl.Buffered(1)

    def cmap(*zeros):
        return lambda s, b: zeros

    in_specs = [
        pl.BlockSpec((rows, D_MODEL), cmap(0, 0), pipeline_mode=once),
        pl.BlockSpec((rows, D_MODEL), lambda s, b: (jnp.minimum(b + 1, n_blk - 1), 0)),
        pl.BlockSpec((1, D_MODEL), cmap(0, 0)),
        pl.BlockSpec((D_MODEL, PROJ_BIG), cmap(0, 0), pipeline_mode=once),
        pl.BlockSpec((D_MODEL, 2 * LANES), cmap(0, 0), pipeline_mode=once),
        pl.BlockSpec((1, N_HEADS, DK, DV), cmap(0, 0, 0, 0)),
        pl.BlockSpec((1, N_HEADS, DK, DV), cmap(0, 0, 0, 0)),
        pl.BlockSpec((1, N_HEADS, DK), cmap(0, 0, 0)),
        pl.BlockSpec((1, 1, N_HEADS), cmap(0, 0, 0)),
        pl.BlockSpec((1, CONV_W - 1, QK_CH), cmap(0, 0, 0)),
    ] + [pl.BlockSpec(w.shape, cmap(0, 0)) for w in wts]
    out_shape = [
        jax.ShapeDtypeStruct((t_len, D_MODEL), BF16),
        jax.ShapeDtypeStruct((1, N_HEADS, DK, DV), F32),
        jax.ShapeDtypeStruct((1, N_HEADS, DK, DV), F32),
        jax.ShapeDtypeStruct((1, N_HEADS, DK), F32),
        jax.ShapeDtypeStruct((1, 1, LANES), F32),
        jax.ShapeDtypeStruct((1, CONV_W - 1, QK_CH), F32),
    ]
    out_specs = [
        pl.BlockSpec((rows, D_MODEL), lambda s, b: (b, 0)),
        pl.BlockSpec((1, N_HEADS, DK, DV), cmap(0, 0, 0, 0)),
        pl.BlockSpec((1, N_HEADS, DK, DV), cmap(0, 0, 0, 0)),
        pl.BlockSpec((1, N_HEADS, DK), cmap(0, 0, 0)),
        pl.BlockSpec((1, 1, LANES), cmap(0, 0, 0)),
        pl.BlockSpec((1, CONV_W - 1, QK_CH), cmap(0, 0, 0)),
    ]
    scratch = [
        pltpu.VMEM((N_HEADS, DV, DK), F32),
        pltpu.VMEM((N_HEADS, DK, DV), F32),
        pltpu.VMEM((N_HEADS, 1, DK), F32),
        pltpu.VMEM((N_HEADS, 1, LANES), F32),
        pltpu.VMEM((rows + 8, QK_CH), F32),
        pltpu.VMEM((rows, N_HEADS * DK), F32),
        pltpu.VMEM((2, rows, PROJ_BIG), BF16),
        pltpu.VMEM((2, rows, LANES), F32),
        pltpu.VMEM((rows, D_MODEL), BF16),
    ]
    return pl.pallas_call(
        functools.partial(_prompt_kernel, rows),
        grid=(1, n_blk),
        in_specs=in_specs, out_specs=out_specs, out_shape=out_shape,
        scratch_shapes=scratch,
        compiler_params=pltpu.CompilerParams(
            dimension_semantics=("arbitrary", "arbitrary"), vmem_limit_bytes=VMEM_LIMIT_FUSED),
        name="prompt_inproj_mixers",
    )(xp, xp, g, w_big, ws, s0, c0, n0, m0, cv0, *wts)


def _pack_rows(ref, val):
    word = pltpu.pack_elementwise([val[:, :HALF], val[:, HALF:]], packed_dtype=BF16)
    rows = val.shape[0]
    for j in range(REC_ROWS):
        ref[pl.ds(j, rows, stride=REC_ROWS), :] = word[:, j * LANES:(j + 1) * LANES]


def _unpack_rows(ref, rows):
    word = jnp.concatenate(
        [ref[pl.ds(j, rows, stride=REC_ROWS), :] for j in range(REC_ROWS)], axis=1)
    lo = pltpu.unpack_elementwise(word, index=0, packed_dtype=BF16, unpacked_dtype=F32)
    hi = pltpu.unpack_elementwise(word, index=1, packed_dtype=BF16, unpacked_dtype=F32)
    return lo, hi


def _outproj_kernel(n_prompt_tiles, op_ref, os_ref, xp_ref, xs_ref, w_ref, g_ref, wr_ref,
                    x1_ref, xn_ref, route_ref, code_ref, cnt_ref, cnt_s):
    i = pl.program_id(0)

    @pl.when(i == 0)
    def _():
        cnt_s[...] = jnp.zeros_like(cnt_s)

    def body(o_ref, x_ref):
        x1 = x_ref[...] + _dot(o_ref[...], w_ref[...])
        x1_ref[...] = x1
        xn = _rmsnorm(x1, g_ref[...])
        _pack_rows(xn_ref, xn)
        xn_hi, xn_lo = _split2(xn)
        lg = _dot_hilo_cols(xn_hi, xn_lo, wr_ref)
        lane = lax.broadcasted_iota(jnp.int32, lg.shape, 1).astype(F32)
        neg = -jnp.inf
        lgm = jnp.where(lane < N_GROUPS, lg, neg)
        mg = jnp.max(lgm, axis=1, keepdims=True)
        g_idx = jnp.min(jnp.where(lgm == mg, lane, float(LANES)), axis=1, keepdims=True)
        g_val = 1.0 / jnp.sum(jnp.where(lane < N_GROUPS, jnp.exp(lg - mg), 0.0), axis=1, keepdims=True)
        e0 = ROUTER_E0 + EXPERTS_PER_GROUP * g_idx
        le = jnp.where((lane >= e0) & (lane < e0 + EXPERTS_PER_GROUP), lg, neg)
        v1 = jnp.max(le, axis=1, keepdims=True)
        i1 = jnp.min(jnp.where(le == v1, lane, float(LANES)), axis=1, keepdims=True)
        le2 = jnp.where(lane == i1, neg, le)
        v2 = jnp.max(le2, axis=1, keepdims=True)
        i2 = jnp.min(jnp.where(le2 == v2, lane, float(LANES)), axis=1, keepdims=True)
        t = jnp.exp(v2 - v1)
        w1 = g_val / (1.0 + t)
        w2 = g_val * t / (1.0 + t)
        oh1 = lane == i1
        oh2 = lane == i2
        hot = jnp.where(oh1 | oh2, 1.0, 0.0)
        r_i = lax.broadcasted_iota(jnp.int32, (ROW_TILE, ROW_TILE), 0)
        c_i = lax.broadcasted_iota(jnp.int32, (ROW_TILE, ROW_TILE), 1)
        before = jnp.where(c_i < r_i, 1.0, 0.0).astype(BF16)
        seen = _dot(before, hot.astype(BF16)) + cnt_s[...]
        rank1 = jnp.sum(jnp.where(oh1, seen, 0.0), axis=1, keepdims=True)
        rank2 = jnp.sum(jnp.where(oh2, seen, 0.0), axis=1, keepdims=True)
        cnt_s[...] += jnp.sum(hot, axis=0, keepdims=True)
        code1 = (i1 - ROUTER_E0) * float(SLOT_CODE) + rank1
        code2 = (i2 - ROUTER_E0) * float(SLOT_CODE) + rank2
        route = jnp.zeros_like(lg)
        for k, col in ((0, code1), (1, code2), (4, w1), (5, w2)):
            route = jnp.where(lane == k, col, route)
        route_ref[...] = route
        code_ref[0] = route.T[0:8, :].astype(jnp.int32)

    @pl.when(i < n_prompt_tiles)
    def _():
        body(op_ref, xp_ref)

    @pl.when(i >= n_prompt_tiles)
    def _():
        body(os_ref, xs_ref)

    cnt_ref[...] = cnt_s[...]


def _outproj(o_p, o_s, xp, xs, w_out, g, wr):
    n_p, n_s = xp.shape[0], xs.shape[0]
    npt = n_p // ROW_TILE
    n = n_p + n_s

    def pmap(i):
        return (jnp.minimum(i, npt - 1), 0)

    def cmap(i):
        return (0, 0)

    def omap(i):
        return (i, 0)

    return pl.pallas_call(
        functools.partial(_outproj_kernel, npt),
        grid=(npt + 1,),
        in_specs=[
            pl.BlockSpec((ROW_TILE, D_MODEL), pmap), pl.BlockSpec((ROW_TILE, D_MODEL), cmap),
            pl.BlockSpec((ROW_TILE, D_MODEL), pmap), pl.BlockSpec((ROW_TILE, D_MODEL), cmap),
            pl.BlockSpec((D_MODEL, D_MODEL), cmap), pl.BlockSpec((1, D_MODEL), cmap),
            pl.BlockSpec((D_MODEL, 2 * LANES), cmap),
        ],
        out_specs=[
            pl.BlockSpec((ROW_TILE, D_MODEL), omap), pl.BlockSpec((ROW_TILE * REC_ROWS, LANES), omap),
            pl.BlockSpec((ROW_TILE, LANES), omap), pl.BlockSpec((1, 8, ROW_TILE), lambda i: (i, 0, 0)),
            pl.BlockSpec((1, LANES), cmap),
        ],
        out_shape=[
            jax.ShapeDtypeStruct((n, D_MODEL), F32),
            jax.ShapeDtypeStruct((n * REC_ROWS, LANES), jnp.uint32),
            jax.ShapeDtypeStruct((n, LANES), F32),
            jax.ShapeDtypeStruct((npt + 1, 8, ROW_TILE), jnp.int32),
            jax.ShapeDtypeStruct((1, LANES), F32),
        ],
        scratch_shapes=[pltpu.VMEM((1, LANES), F32)],
        compiler_params=pltpu.CompilerParams(
            dimension_semantics=("arbitrary",), vmem_limit_bytes=VMEM_LIMIT),
        name="outproj_router",
    )(o_p, o_s, xp, xs, w_out, g, wr)


def _rec(ref, idx):
    return ref.at[pl.ds(pl.multiple_of(idx * REC_ROWS, REC_ROWS), REC_ROWS)]


def _zero_records_kernel(o_ref):
    z = jnp.zeros(o_ref.shape, F32)
    o_ref[...] = pltpu.pack_elementwise([z, z], packed_dtype=BF16)


def _slot_rows_kernel(row0_ref, code_ref, pos_ref):
    code = code_ref[...]
    expert = lax.shift_right_logical(code, SLOT_CODE_BITS)
    first = jnp.zeros_like(code)
    for e in range(N_EXPERTS):
        first = jnp.where(expert == e, row0_ref[e], first)
    pos_ref[...] = first + (code & (SLOT_CODE - 1))


def _slot_rows(row0, codes):
    return pl.pallas_call(
        _slot_rows_kernel,
        grid_spec=pltpu.PrefetchScalarGridSpec(
            num_scalar_prefetch=1,
            grid=(1,),
            in_specs=[pl.BlockSpec(codes.shape, lambda i, r: (0, 0, 0))],
            out_specs=pl.BlockSpec(codes.shape, lambda i, r: (0, 0, 0)),
        ),
        out_shape=jax.ShapeDtypeStruct(codes.shape, codes.dtype),
        name="moe_slot_rows",
    )(row0, codes)


def _dispatch_kernel(t_max, p1_ref, p2_ref, padlo_ref, padn_ref, nt_ref, xn_ref, xs_ref,
                     zero_s, sem, zsem):
    step = pl.program_id(0)
    base = step * ROW_TILE

    def pad_copy(row):
        return pltpu.make_async_copy(_rec(zero_s, 0), _rec(xs_ref, row), zsem)

    def tile_copy(t):
        rows = EXP_TILE * REC_ROWS
        return pltpu.make_async_copy(zero_s, xs_ref.at[pl.ds(pl.multiple_of(t * rows, rows), rows)], zsem)

    def for_pad_rows(fn):
        def per_expert(e, c):
            lo = padlo_ref[e]
            lax.fori_loop(0, padn_ref[e], lambda r, cc: (fn(pad_copy(lo + r)), cc)[1], 0)
            return c
        lax.fori_loop(0, N_EXPERTS, per_expert, 0)
        lax.fori_loop(nt_ref[0], t_max, lambda t, c: (fn(tile_copy(t)), c)[1], 0)

    @pl.when(step == 0)
    def _():
        _zero_records_kernel(zero_s)
        for_pad_rows(lambda cp: cp.start(priority=1))

    def copies(r):
        src = _rec(xn_ref, r)
        return (pltpu.make_async_copy(src, _rec(xs_ref, p1_ref[base + r]), sem),
                pltpu.make_async_copy(src, _rec(xs_ref, p2_ref[base + r]), sem))

    def start(g, c):
        for u in range(DMA_UNROLL):
            for prio, cp in enumerate(copies(g * DMA_UNROLL + u)):
                cp.start(priority=prio)
        return c

    def wait(g, c):
        for u in range(DMA_UNROLL):
            for cp in copies(g * DMA_UNROLL + u):
                cp.wait()
        return c

    lax.fori_loop(0, ROW_TILE // DMA_UNROLL, start, 0)
    lax.fori_loop(0, ROW_TILE // DMA_UNROLL, wait, 0)

    @pl.when(step == 0)
    def _():
        for_pad_rows(lambda cp: cp.wait())


def _dispatch(pos1, pos2, pad_lo, pad_n, n_tiles, xn_rec, t_max):
    n = pos1.shape[0]
    return pl.pallas_call(
        functools.partial(_dispatch_kernel, t_max),
        grid_spec=pltpu.PrefetchScalarGridSpec(
            num_scalar_prefetch=5,
            grid=(n // ROW_TILE,),
            in_specs=[pl.BlockSpec((ROW_TILE * REC_ROWS, LANES), lambda i, *_: (i, 0))],
            out_specs=pl.BlockSpec(memory_space=pl.ANY),
            scratch_shapes=[pltpu.VMEM((EXP_TILE * REC_ROWS, LANES), jnp.uint32),
                            pltpu.SemaphoreType.DMA, pltpu.SemaphoreType.DMA],
        ),
        out_shape=jax.ShapeDtypeStruct((t_max * EXP_TILE * REC_ROWS, LANES), jnp.uint32),
        compiler_params=pltpu.CompilerParams(
            dimension_semantics=("arbitrary",), vmem_limit_bytes=VMEM_LIMIT),
        name="moe_dispatch",
    )(pos1, pos2, pad_lo, pad_n, n_tiles, xn_rec)


def _experts_kernel(te_ref, nt_ref, xs_ref, wg_ref, wu_ref, wd_ref, ys_ref, wg_s, wu_s, wd_s):
    t = pl.program_id(0)
    prev = te_ref[jnp.maximum(t - 1, 0)]

    @pl.when((t == 0) | (te_ref[t] != prev))
    def _():
        wg_s[...] = wg_ref[0].astype(BF16)
        wu_s[...] = wu_ref[0].astype(BF16)
        wd_s[...] = wd_ref[0].astype(BF16)

    @pl.when(t < nt_ref[0])
    def _():
        lo, hi = _unpack_rows(xs_ref, EXP_TILE)
        x = jnp.concatenate([lo.astype(BF16), hi.astype(BF16)], axis=1)
        hg = _dot(x, wg_s[...])
        hu = _dot(x, wu_s[...])
        act = hg * _sigmoid(hg) * hu
        _pack_rows(ys_ref, _dot(act.astype(BF16), wd_s[...]))

    @pl.when(t >= nt_ref[0])
    def _():
        _zero_records_kernel(ys_ref)


def _experts(tile_expert, n_tiles, xs, wg, wu, wd):
    t_max = tile_expert.shape[0]

    def tmap(t, te, nt):
        return (jnp.minimum(t, nt[0] - 1), 0)

    def wmap(t, te, nt):
        return (te[t], 0, 0)

    return pl.pallas_call(
        _experts_kernel,
        grid_spec=pltpu.PrefetchScalarGridSpec(
            num_scalar_prefetch=2,
            grid=(t_max,),
            in_specs=[pl.BlockSpec((EXP_TILE * REC_ROWS, LANES), tmap),
                      pl.BlockSpec((1, D_MODEL, D_EXPERT), wmap),
                      pl.BlockSpec((1, D_MODEL, D_EXPERT), wmap),
                      pl.BlockSpec((1, D_EXPERT, D_MODEL), wmap)],
            out_specs=pl.BlockSpec((EXP_TILE * REC_ROWS, LANES), lambda t, te, nt: (t, 0)),
            scratch_shapes=[pltpu.VMEM((D_MODEL, D_EXPERT), BF16), pltpu.VMEM((D_MODEL, D_EXPERT), BF16),
                            pltpu.VMEM((D_EXPERT, D_MODEL), BF16)],
        ),
        out_shape=jax.ShapeDtypeStruct(xs.shape, xs.dtype),
        compiler_params=pltpu.CompilerParams(
            dimension_semantics=("arbitrary",), vmem_limit_bytes=VMEM_LIMIT),
        name="moe_experts",
    )(tile_expert, n_tiles, xs, wg, wu, wd)


def _combine_kernel(n_prompt_tiles, p1_ref, p2_ref, x1_ref, route_ref, g_ref, ys_ref,
                    yp_ref, ysm_ref, a_s, b_s, sem):
    i = pl.program_id(0)
    n_steps = pl.num_programs(0)

    def copies(step, slot, r):
        tok = step * ROW_TILE + r
        return (pltpu.make_async_copy(_rec(ys_ref, p1_ref[tok]), _rec(a_s.at[slot], r), sem.at[slot]),
                pltpu.make_async_copy(_rec(ys_ref, p2_ref[tok]), _rec(b_s.at[slot], r), sem.at[slot]))

    def start_all(step, slot):
        def start(g, c):
            for u in range(DMA_UNROLL):
                for prio, cp in enumerate(copies(step, slot, g * DMA_UNROLL + u)):
                    cp.start(priority=prio)
            return c
        lax.fori_loop(0, ROW_TILE // DMA_UNROLL, start, 0)

    def wait_all(step, slot):
        def wait(g, c):
            for u in range(DMA_UNROLL):
                for cp in copies(step, slot, g * DMA_UNROLL + u):
                    cp.wait()
            return c
        lax.fori_loop(0, ROW_TILE // DMA_UNROLL, wait, 0)

    slot = i % 2

    @pl.when(i == 0)
    def _():
        start_all(0, 0)

    @pl.when(i + 1 < n_steps)
    def _():
        start_all(i + 1, 1 - slot)

    wait_all(i, slot)
    a_lo, a_hi = _unpack_rows(a_s.at[slot], ROW_TILE)
    b_lo, b_hi = _unpack_rows(b_s.at[slot], ROW_TILE)
    route = route_ref[...]
    w1 = route[:, 4:5]
    w2 = route[:, 5:6]
    moe = jnp.concatenate([w1 * a_lo + w2 * b_lo, w1 * a_hi + w2 * b_hi], axis=1)
    y = _rmsnorm(x1_ref[...] + moe, g_ref[...])

    @pl.when(i < n_prompt_tiles)
    def _():
        yp_ref[...] = y

    @pl.when(i >= n_prompt_tiles)
    def _():
        ysm_ref[...] = y


def _combine(pos1, pos2, x1, route, g, ys, n_p):
    n = x1.shape[0]
    npt = n_p // ROW_TILE

    def omap(i, *_):
        return (i, 0)

    return pl.pallas_call(
        functools.partial(_combine_kernel, npt),
        grid_spec=pltpu.PrefetchScalarGridSpec(
            num_scalar_prefetch=2,
            grid=(n // ROW_TILE,),
            in_specs=[pl.BlockSpec((ROW_TILE, D_MODEL), omap),
                      pl.BlockSpec((ROW_TILE, LANES), omap),
                      pl.BlockSpec((1, D_MODEL), lambda i, *_: (0, 0)),
                      pl.BlockSpec(memory_space=pl.ANY)],
            out_specs=[pl.BlockSpec((ROW_TILE, D_MODEL), lambda i, *_: (jnp.minimum(i, npt - 1), 0)),
                       pl.BlockSpec((ROW_TILE, D_MODEL), lambda i, *_: (jnp.maximum(i - npt, 0), 0))],
            scratch_shapes=[pltpu.VMEM((2, ROW_TILE * REC_ROWS, LANES), jnp.uint32),
                            pltpu.VMEM((2, ROW_TILE * REC_ROWS, LANES), jnp.uint32),
                            pltpu.SemaphoreType.DMA((2,))],
        ),
        out_shape=[jax.ShapeDtypeStruct((n_p, D_MODEL), F32),
                   jax.ShapeDtypeStruct((n - n_p, D_MODEL), F32)],
        compiler_params=pltpu.CompilerParams(
            dimension_semantics=("arbitrary",), vmem_limit_bytes=VMEM_LIMIT),
        name="moe_combine",
    )(pos1, pos2, x1, route, g, ys)


def _pad_lanes(w):
    return jnp.pad(w, ((0, 0), (0, LANES - w.shape[1])))


def _moe_plan(counts, n_tokens):
    cnt = counts[0, ROUTER_E0:ROUTER_E0 + N_EXPERTS].astype(jnp.int32)
    tiles = (cnt + EXP_TILE - 1) // EXP_TILE
    tile_end = jnp.cumsum(tiles)
    row0 = (tile_end - tiles) * EXP_TILE
    t_max = 2 * n_tokens // EXP_TILE + N_EXPERTS
    tile_ids = jnp.arange(t_max, dtype=jnp.int32)
    tile_expert = jnp.minimum(
        jnp.sum((tile_ids[:, None] >= tile_end[None, :]).astype(jnp.int32), axis=1), N_EXPERTS - 1)
    return row0, row0 + cnt, tiles * EXP_TILE - cnt, tile_expert, tile_end[-1:].astype(jnp.int32), t_max


def kernel(x_prompt, x_sample, state_gla_S, state_mlstm_C, state_mlstm_n, state_mlstm_m, cache_mlstm_conv, g_mix_norm, w_in, w_gla_gate_up, b_gla_gate_up, g_gla_out, w_mlstm_conv, b_mlstm_conv, b_mlstm_i, b_mlstm_f, g_mlstm_out, w_out, g_ffn_norm, w_router_group, w_router_expert, w_exp_gate, w_exp_up, w_exp_down, g_final):
    depth = w_in.shape[0]
    assert depth == 1
    bp, t_p, _ = x_prompt.shape
    bs, t_s, _ = x_sample.shape
    assert bp == 1
    xp = x_prompt.reshape(bp * t_p, D_MODEL)
    xs = x_sample.reshape(bs * t_s, D_MODEL)
    n_p = xp.shape[0]

    wt = jnp.transpose(w_in[0])
    c_gz = 3072
    c_mqk = c_gz + GLA_GATE_RANK
    c_mi = c_mqk + 3072
    w_big = _pack_inproj_weight(wt, c_gz, c_mqk)
    ws = _hilo_cols(_pad_lanes(jnp.concatenate([wt[c_gz:c_mqk], wt[c_mi:]], axis=0).T))
    wz = jnp.pad(w_gla_gate_up[0], ((0, LANES - GLA_GATE_RANK), (0, 0)))
    wz_hi, wz_lo = _split2(wz)
    bsm = _pad_lanes(jnp.concatenate(
        [jnp.zeros((1, GLA_GATE_RANK), F32), b_mlstm_i[0][None], b_mlstm_f[0][None]], axis=1))
    mix_w = (wz_hi, wz_lo, b_gla_gate_up[0][None], g_gla_out[0][None], w_mlstm_conv[0],
             b_mlstm_conv[0][None], bsm, g_mlstm_out[0][None])
    wr = _hilo_cols(_pad_lanes(jnp.concatenate([w_router_group[0], w_router_expert[0]], axis=1)))

    g_mix = g_mix_norm[0][None]
    p_big, p_small = _inproj(xs, g_mix, w_big, ws)

    dt = x_prompt.dtype
    z_s = jnp.zeros((bp, N_HEADS, DK, DV), dt)
    z_n = jnp.zeros((bp, N_HEADS, DK), dt)
    z_m = jnp.zeros((bp, 1, N_HEADS), dt)
    z_cv = jnp.zeros((bp, CONV_W - 1, QK_CH), dt)
    o_p, p_S, p_C, p_n, p_m, p_cv = _prompt_mixers(
        xp, g_mix, w_big, ws, MIX_ROWS, z_s, z_s, z_n, z_m, z_cv, mix_w)
    o_s, s_S, s_C, s_n, s_m, s_cv = _mixers(
        p_big, p_small, 0, bs, t_s, t_s, state_gla_S[0], state_mlstm_C[0], state_mlstm_n[0],
        state_mlstm_m[0][:, None, :], cache_mlstm_conv[0], mix_w)

    x1, xn_rec, route, codes, counts = _outproj(
        o_p, o_s, xp, xs, w_out[0].astype(BF16), g_ffn_norm[0][None], wr)
    row0, pad_lo, pad_n, tile_expert, n_tiles, t_max = _moe_plan(counts, x1.shape[0])
    pos = _slot_rows(row0, codes)
    pos1 = pos[:, 0, :].reshape(-1)
    pos2 = pos[:, 1, :].reshape(-1)
    xs_rec = _dispatch(pos1, pos2, pad_lo, pad_n, n_tiles, xn_rec, t_max)
    ys_rec = _experts(tile_expert, n_tiles, xs_rec, w_exp_gate[0], w_exp_up[0], w_exp_down[0])
    y_p, y_s = _combine(pos1, pos2, x1, route, g_final[None], ys_rec, n_p)

    return (y_p.reshape(x_prompt.shape), y_s.reshape(x_sample.shape),
            p_S[None], p_C[None], p_n[None], p_m[:, 0, :N_HEADS][None], p_cv[None],
            s_S[None], s_C[None], s_n[None], s_m[:, 0, :N_HEADS][None], s_cv[None])
```

```python
import functools

import jax
import jax.numpy as jnp
from jax import lax
from jax.experimental import pallas as pl
from jax.experimental.pallas import tpu as pltpu

F32 = jnp.float32
BF16 = jnp.bfloat16

D_MODEL = 2048
N_HEADS = 4
DK = 128
DV = 256
GLA_GATE_RANK = 16
GLA_GATE_NORM = 16.0
CONV_W = 4
QK_CH = 2 * N_HEADS * DK
N_GROUPS = 4
EXPERTS_PER_GROUP = 8
N_EXPERTS = N_GROUPS * EXPERTS_PER_GROUP
D_EXPERT = 256
EPS = 1e-6

LANES = 128
GLA_SUB = 16
GLA_SAFE_DECAY = 60.0
MIX_ROWS = 256
ROW_TILE = 512
PROJ_BIG = 6 * 1024
PROJ_CHUNK = 1024
PACK_COLS = 256
LANE_I = GLA_GATE_RANK
LANE_F = GLA_GATE_RANK + N_HEADS
ROUTER_E0 = N_GROUPS
HALF = D_MODEL // 2
REC_ROWS = HALF // LANES
EXP_TILE = 256
DMA_UNROLL = 8
SLOT_CODE_BITS = 16
SLOT_CODE = 1 << SLOT_CODE_BITS

VMEM_LIMIT = 56 * 1024 * 1024
VMEM_LIMIT_FUSED = 62 * 1024 * 1024


def _dot(a, b):
    return jnp.dot(a, b, preferred_element_type=F32)


def _dot_nt(a, b):
    return lax.dot_general(a, b, (((1,), (1,)), ((), ())), preferred_element_type=F32)


def _dot_tn(a, b):
    return lax.dot_general(a, b, (((0,), (0,)), ((), ())), preferred_element_type=F32)


def _split2(x):
    hi = x.astype(BF16)
    lo = (x - hi.astype(F32)).astype(BF16)
    return hi, lo


def _split3(x):
    hi = x.astype(BF16)
    r = x - hi.astype(F32)
    mid = r.astype(BF16)
    lo = (r - mid.astype(F32)).astype(BF16)
    return hi, mid, lo


def _dot_exact_lhs(m_bf16, x):
    hi, mid, lo = _split3(x)
    return _dot(m_bf16, hi) + _dot(m_bf16, mid) + _dot(m_bf16, lo)


def _dot_hilo(a_hi, a_lo, b_hi, b_lo):
    return _dot(a_hi, b_hi) + _dot(a_lo, b_hi) + _dot(a_hi, b_lo)


def _hilo_cols(w):
    return jnp.concatenate(_split2(w), axis=1)


def _dot_hilo_cols(a_hi, a_lo, b_ref):
    p = _dot(a_hi, b_ref[...])
    return (p[:, LANES:] + _dot(a_lo, b_ref[:, :LANES])) + p[:, :LANES]


def _log_sigmoid(z):
    return jnp.minimum(z, 0.0) - jnp.log1p(jnp.exp(-jnp.abs(z)))


def _sigmoid(z):
    return 0.5 * jnp.tanh(0.5 * z) + 0.5


def _rmsnorm(x, g):
    return x * lax.rsqrt(jnp.mean(x * x, axis=-1, keepdims=True) + EPS) * g


def _pack_inproj_weight_kernel(c_gz, c_mqk, wt_ref, o_ref, buf, sem):
    j = pl.program_id(0)
    n_j = pl.num_programs(0)

    def copy(jj):
        col0 = jj * PACK_COLS
        row = pl.multiple_of(col0 + jnp.where(col0 >= c_gz, c_mqk - c_gz, 0), 16)
        return pltpu.make_async_copy(wt_ref.at[pl.ds(row, PACK_COLS)], buf.at[jj % 2], sem.at[jj % 2])

    @pl.when(j == 0)
    def _():
        copy(0).start()

    @pl.when(j + 1 < n_j)
    def _():
        copy(j + 1).start()

    copy(j).wait()
    r_i = lax.broadcasted_iota(jnp.int32, (PACK_COLS, PACK_COLS), 0)
    c_i = lax.broadcasted_iota(jnp.int32, (PACK_COLS, PACK_COLS), 1)
    eye = jnp.where(r_i == c_i, 1.0, 0.0).astype(BF16)
    o_ref[...] = _dot_tn(buf[j % 2].astype(BF16), eye).astype(o_ref.dtype)


def _pack_inproj_weight(wt, c_gz, c_mqk):
    assert c_gz % PACK_COLS == 0 and (c_mqk - c_gz) % 16 == 0
    return pl.pallas_call(
        functools.partial(_pack_inproj_weight_kernel, c_gz, c_mqk),
        grid=(PROJ_BIG // PACK_COLS,),
        in_specs=[pl.BlockSpec(memory_space=pl.ANY)],
        out_specs=pl.BlockSpec((wt.shape[1], PACK_COLS), lambda j: (0, j)),
        out_shape=jax.ShapeDtypeStruct((wt.shape[1], PROJ_BIG), BF16),
        scratch_shapes=[pltpu.VMEM((2, PACK_COLS, wt.shape[1]), F32), pltpu.SemaphoreType.DMA((2,))],
        compiler_params=pltpu.CompilerParams(dimension_semantics=("arbitrary",), vmem_limit_bytes=VMEM_LIMIT),
        name="pack_inproj_weight",
    )(wt)


def _project(x_ref, g_ref, w_ref, ws_ref, p_ref, ps_ref, h_ref, chunks, with_norm):
    if with_norm:
        y_hi, y_lo = _split2(_rmsnorm(x_ref[...], g_ref[...]))
        h_ref[...] = y_hi
        ps_ref[...] = _dot_hilo_cols(y_hi, y_lo, ws_ref)
    else:
        y_hi = h_ref[...]
    for c in chunks:
        cols = slice(c * PROJ_CHUNK, (c + 1) * PROJ_CHUNK)
        p_ref[:, cols] = _dot(y_hi, w_ref[:, cols]).astype(p_ref.dtype)


PROJ_CHUNKS = tuple(range(PROJ_BIG // PROJ_CHUNK))


def _inproj_kernel(x_ref, g_ref, w_ref, ws_ref, p_ref, ps_ref, h_s):
    _project(x_ref, g_ref, w_ref, ws_ref, p_ref, ps_ref, h_s, PROJ_CHUNKS, True)


def _inproj(x, g, w_big, ws):
    n = x.shape[0]
    assert n % ROW_TILE == 0
    once = pl.Buffered(1)
    return pl.pallas_call(
        _inproj_kernel,
        grid=(n // ROW_TILE,),
        in_specs=[
            pl.BlockSpec((ROW_TILE, D_MODEL), lambda i: (i, 0)),
            pl.BlockSpec((1, D_MODEL), lambda i: (0, 0)),
            pl.BlockSpec((D_MODEL, PROJ_BIG), lambda i: (0, 0), pipeline_mode=once),
            pl.BlockSpec((D_MODEL, 2 * LANES), lambda i: (0, 0), pipeline_mode=once),
        ],
        out_specs=[
            pl.BlockSpec((ROW_TILE, PROJ_BIG), lambda i: (i, 0)),
            pl.BlockSpec((ROW_TILE, LANES), lambda i: (i, 0)),
        ],
        out_shape=[
            jax.ShapeDtypeStruct((n, PROJ_BIG), BF16),
            jax.ShapeDtypeStruct((n, LANES), F32),
        ],
        scratch_shapes=[pltpu.VMEM((ROW_TILE, D_MODEL), BF16)],
        compiler_params=pltpu.CompilerParams(
            dimension_semantics=("arbitrary",), vmem_limit_bytes=VMEM_LIMIT),
        name="inproj",
    )(x, g, w_big, ws)


def _mixer_kernel(rows, side_work, gq_ref, gk_ref, gv_ref, gg_ref, mqk_ref, mv_ref, mo_ref, sm_ref,
                  s0_ref, c0_ref, n0_ref, m0_ref, cv0_ref,
                  wz_hi_ref, wz_lo_ref, bz_ref, g_gla_ref, wc_ref, bc_ref, bsm_ref, g_ml_ref,
                  o_ref, s_out_ref, c_out_ref, n_out_ref, m_out_ref, cv_out_ref,
                  st_s, c_s, n_s, m_s, cb_s, b_s):
    blk = pl.program_id(1)
    n_blk = pl.num_programs(1)

    @pl.when(blk == 0)
    def _():
        for h in range(N_HEADS):
            st_s[h] = s0_ref[0, h].T
            c_s[h] = c0_ref[0, h]
            n_s[h] = n0_ref[0, h:h + 1, :]
            m_s[h] = jnp.broadcast_to(m0_ref[0, :, h:h + 1], (1, LANES))
        cb_s[0:8, :] = jnp.zeros((8, QK_CH), F32)
        cb_s[8 - (CONV_W - 1):8, :] = cv0_ref[0]

    small = sm_ref[...]
    row_i = lax.broadcasted_iota(jnp.int32, (rows, rows), 0)
    col_i = lax.broadcasted_iota(jnp.int32, (rows, rows), 1)
    causal = col_i <= row_i

    sm_hi, sm_lo = _split2(small)
    z = _dot_hilo(sm_hi, sm_lo, wz_hi_ref[...], wz_lo_ref[...]) + bz_ref[...]
    log_a = _log_sigmoid(z) * (1.0 / GLA_GATE_NORM)
    tri = jnp.where(causal, 1.0, 0.0).astype(BF16)
    b_blk = _dot_exact_lhs(tri, log_a)
    b_last = b_blk[rows - 1:rows, :]
    factorable = jnp.min(b_last) >= -GLA_SAFE_DECAY
    side_work[0]()

    def gla_finish(o, vc, gate):
        y = o * lax.rsqrt(jnp.mean(o * o, axis=-1, keepdims=True) + EPS) * g_gla_ref[:, vc]
        return (y * (gate * _sigmoid(gate))).astype(o_ref.dtype)

    @pl.when(factorable)
    def _():
        for h in range(N_HEADS):
            kc = slice(h * DK, (h + 1) * DK)
            vc = slice(h * DV, (h + 1) * DV)
            bh = b_blk[:, kc]
            qh = gq_ref[:, kc].astype(F32) * (DK ** -0.5)
            kh = gk_ref[:, kc].astype(F32)
            vh = gv_ref[:, vc]
            st = st_s[h]
            q_dec = (qh * jnp.exp(bh)).astype(BF16)
            k_inv = (kh * jnp.exp(-bh)).astype(BF16)
            a = jnp.where(causal, _dot_nt(q_dec, k_inv), 0.0)
            o = _dot(a.astype(BF16), vh) + _dot_nt(q_dec, st.astype(BF16))
            bh_end = b_last[:, kc]
            k_dec = (kh * jnp.exp(bh_end - bh)).astype(BF16)
            st_s[h] = st * jnp.exp(bh_end) + _dot_tn(vh, k_dec)
            o_ref[:, vc] = gla_finish(o, vc, gg_ref[:, vc].astype(F32))

    sub_r = lax.broadcasted_iota(jnp.int32, (GLA_SUB, LANES), 0)
    sub_l = lax.broadcasted_iota(jnp.int32, (GLA_SUB, LANES), 1)

    def gla_sub(c, carry):
        r0 = pl.multiple_of(c * GLA_SUB, GLA_SUB)
        rs = pl.ds(r0, GLA_SUB)
        for h in range(N_HEADS):
            kc = slice(h * DK, (h + 1) * DK)
            vc = slice(h * DV, (h + 1) * DV)
            bh = b_s[rs, kc]
            qh = gq_ref[rs, kc].astype(F32) * (DK ** -0.5)
            kh = gk_ref[rs, kc].astype(F32)
            vh = gv_ref[rs, vc]
            st = st_s[h]
            o = _dot_nt((qh * jnp.exp(bh)).astype(BF16), st.astype(BF16))
            a = jnp.zeros((GLA_SUB, LANES), F32)
            for s in range(GLA_SUB):
                e = jnp.exp(jnp.minimum(bh - bh[s:s + 1, :], 0.0))
                col = jnp.sum(qh * (kh[s:s + 1, :] * e), axis=1, keepdims=True)
                a = jnp.where((sub_l == s) & (sub_r >= s), col, a)
            o = o + _dot(a[:, :GLA_SUB].astype(BF16), vh)
            b_end = bh[GLA_SUB - 1:GLA_SUB, :]
            k_dec = kh * jnp.exp(b_end - bh)
            st_s[h] = st * jnp.exp(b_end) + _dot_tn(vh, k_dec.astype(BF16))
            o_ref[rs, vc] = gla_finish(o, vc, gg_ref[rs, vc].astype(F32))
        return carry

    @pl.when(jnp.logical_not(factorable))
    def _():
        same_sub = (row_i // GLA_SUB) == (col_i // GLA_SUB)
        blk_tri = jnp.where(causal & same_sub, 1.0, 0.0).astype(BF16)
        b_s[...] = _dot_exact_lhs(blk_tri, log_a)
        lax.fori_loop(0, rows // GLA_SUB, gla_sub, 0)

    cb_s[8:8 + rows, :] = mqk_ref[...].astype(F32)
    conv = bc_ref[...]
    for j in range(CONV_W):
        conv = conv + cb_s[8 - (CONV_W - 1) + j:8 - (CONV_W - 1) + j + rows, :] * wc_ref[j:j + 1, :]
    cb_s[0:8, :] = cb_s[rows:rows + 8, :]
    qk = conv * _sigmoid(conv)

    pre = small + bsm_ref[...]
    log_f = _log_sigmoid(pre)
    f_cum =_dot_exact_lhs(tri, log_f)
    eye = jnp.where(row_i == col_i, 1.0, 0.0).astype(BF16)
    f_cum_t = sum(_dot_tn(p, eye) for p in _split3(f_cum))
    pre_t = sum(_dot_tn(p, eye) for p in _split3(pre))
    side_work[1]()
    lane_1 = lax.broadcasted_iota(jnp.int32, (1, LANES), 1)
    m_new = jnp.zeros((1, LANES), F32)

    for h in range(N_HEADS):
        kc = slice(h * DK, (h + 1) * DK)
        vc = slice(h * DV, (h + 1) * DV)
        f_col = f_cum[:, LANE_F + h:LANE_F + h + 1]
        i_col = pre[:, LANE_I + h:LANE_I + h + 1]
        f_row = f_cum_t[LANE_F + h:LANE_F + h + 1, :]
        i_row = pre_t[LANE_I + h:LANE_I + h + 1, :]
        dm = jnp.where(causal, f_col - f_row + i_row, -jnp.inf)
        m_prev = m_s[h][:, 0:1]
        inter = f_col + m_prev
        m_t = jnp.maximum(inter, jnp.max(dm, axis=1, keepdims=True))
        w = jnp.exp(dm - m_t)
        w_inter = jnp.exp(inter - m_t)
        q = qk[:, kc]
        k = qk[:, N_HEADS * DK + h * DK:N_HEADS * DK + (h + 1) * DK] * (DK ** -0.5)
        v = mv_ref[:, vc]
        q_b = q.astype(BF16)
        c_prev = c_s[h]
        n_prev = n_s[h]
        s_qk = _dot_nt(q_b, k.astype(BF16)) * w
        num = _dot(s_qk.astype(BF16), v) + w_inter * _dot(q_b, c_prev.astype(BF16))
        den = (jnp.sum(s_qk, axis=1, keepdims=True)
               + w_inter * jnp.sum(q * n_prev, axis=1, keepdims=True))
        hh = num / jnp.maximum(jnp.abs(den), jnp.exp(-m_t))
        m_end = m_t[rows - 1:rows, :]
        w_s = jnp.exp(f_col[rows - 1:rows, :] - f_col + i_col - m_end)
        dec = w_inter[rows - 1:rows, :]
        k_w = k * w_s
        c_s[h] = dec * c_prev + _dot_tn(k_w.astype(BF16), v)
        side_work[2 + h]()
        n_s[h] = dec * n_prev + jnp.sum(k_w, axis=0, keepdims=True)
        m_s[h] = jnp.broadcast_to(m_end, (1, LANES))
        m_new = jnp.where(lane_1 == h, m_end, m_new)
        y = hh * lax.rsqrt(jnp.mean(hh * hh, axis=-1, keepdims=True) + EPS) * g_ml_ref[:, vc]
        o_ref[:, N_HEADS * DV + h * DV:N_HEADS * DV + (h + 1) * DV] = (
            y * _sigmoid(mo_ref[:, vc].astype(F32))).astype(o_ref.dtype)

    @pl.when(blk == n_blk - 1)
    def _():
        for h in range(N_HEADS):
            s_out_ref[0, h] = st_s[h].T
            c_out_ref[0, h] = c_s[h]
            n_out_ref[0, h:h + 1, :] = n_s[h]
        m_out_ref[0] = m_new
        cv_out_ref[0] = cb_s[8 - (CONV_W - 1):8, :]


def _mixers(p_big, p_small, row0, n_streams, t_len, rows, s0, c0, n0, m0, cv0, wts):
    assert t_len % rows == 0 and row0 % rows == 0 and rows % GLA_SUB == 0
    n_blk = t_len // rows
    b0 = row0 // rows

    def rmap(col):
        return lambda s, b: (b0 + s * n_blk + b, col)

    def smap(*zeros):
        return lambda s, b: (s,) + zeros

    def wmap(s, b):
        return (0, 0)

    wz_hi, wz_lo, bz, g_gla, wc, bc, bsm, g_ml = wts
    in_specs = [
        pl.BlockSpec((rows, N_HEADS * DK), rmap(0)),
        pl.BlockSpec((rows, N_HEADS * DK), rmap(1)),
        pl.BlockSpec((rows, N_HEADS * DV), rmap(1)),
        pl.BlockSpec((rows, N_HEADS * DV), rmap(2)),
        pl.BlockSpec((rows, QK_CH), rmap(3)),
        pl.BlockSpec((rows, N_HEADS * DV), rmap(4)),
        pl.BlockSpec((rows, N_HEADS * DV), rmap(5)),
        pl.BlockSpec((rows, LANES), rmap(0)),
        pl.BlockSpec((1, N_HEADS, DK, DV), smap(0, 0, 0)),
        pl.BlockSpec((1, N_HEADS, DK, DV), smap(0, 0, 0)),
        pl.BlockSpec((1, N_HEADS, DK), smap(0, 0)),
        pl.BlockSpec((1, 1, N_HEADS), smap(0, 0)),
        pl.BlockSpec((1, CONV_W - 1, QK_CH), smap(0, 0)),
        pl.BlockSpec(wz_hi.shape, wmap), pl.BlockSpec(wz_lo.shape, wmap),
        pl.BlockSpec(bz.shape, wmap), pl.BlockSpec(g_gla.shape, wmap),
        pl.BlockSpec(wc.shape, wmap), pl.BlockSpec(bc.shape, wmap),
        pl.BlockSpec(bsm.shape, wmap), pl.BlockSpec(g_ml.shape, wmap),
    ]
    n_rows = n_streams * t_len
    out_shape = [
        jax.ShapeDtypeStruct((n_rows, D_MODEL), BF16),
        jax.ShapeDtypeStruct((n_streams, N_HEADS, DK, DV), F32),
        jax.ShapeDtypeStruct((n_streams, N_HEADS, DK, DV), F32),
        jax.ShapeDtypeStruct((n_streams, N_HEADS, DK), F32),
        jax.ShapeDtypeStruct((n_streams, 1, LANES), F32),
        jax.ShapeDtypeStruct((n_streams, CONV_W - 1, QK_CH), F32),
    ]
    out_specs = [
        pl.BlockSpec((rows, D_MODEL), lambda s, b: (s * n_blk + b, 0)),
        pl.BlockSpec((1, N_HEADS, DK, DV), smap(0, 0, 0)),
        pl.BlockSpec((1, N_HEADS, DK, DV), smap(0, 0, 0)),
        pl.BlockSpec((1, N_HEADS, DK), smap(0, 0)),
        pl.BlockSpec((1, 1, LANES), smap(0, 0)),
        pl.BlockSpec((1, CONV_W - 1, QK_CH), smap(0, 0)),
    ]
    scratch = [
        pltpu.VMEM((N_HEADS, DV, DK), F32),
        pltpu.VMEM((N_HEADS, DK, DV), F32),
        pltpu.VMEM((N_HEADS, 1, DK), F32),
        pltpu.VMEM((N_HEADS, 1, LANES), F32),
        pltpu.VMEM((rows + 8, QK_CH), F32),
        pltpu.VMEM((rows, N_HEADS * DK), F32),
    ]
    return pl.pallas_call(
        functools.partial(_mixer_kernel, rows, (lambda: None,) * (2 + N_HEADS)),
        grid=(n_streams, n_blk),
        in_specs=in_specs, out_specs=out_specs, out_shape=out_shape,
        scratch_shapes=scratch,
        compiler_params=pltpu.CompilerParams(
            dimension_semantics=("arbitrary", "arbitrary"), vmem_limit_bytes=VMEM_LIMIT),
        name=f"mixers_r{rows}",
    )(p_big, p_big, p_big, p_big, p_big, p_big, p_big, p_small, s0, c0, n0, m0, cv0, *wts)


N_MIX_IN = 5 + 8
N_MIX_OUT = 6
N_MIX_SCRATCH = 6
N_CAST = 3


def _prompt_kernel(rows, n_blk, x0_ref, xn_ref, g_ref, w_ref, ws_ref, *refs):
    mix_in, ew_in = refs[:N_MIX_IN], refs[N_MIX_IN:N_MIX_IN + N_CAST]
    outs = refs[N_MIX_IN + N_CAST:]
    mix_out, ew_out, scr = outs[:N_MIX_OUT], outs[N_MIX_OUT:N_MIX_OUT + N_CAST], outs[N_MIX_OUT + N_CAST:]
    mix_scr, (p_buf, ps_buf, h_s) = scr[:N_MIX_SCRATCH], scr[N_MIX_SCRATCH:N_MIX_SCRATCH + 3]
    rest = scr[N_MIX_SCRATCH + 3:]
    stage_in, stage_out, (sem_in, sem_out) = rest[:N_CAST], rest[N_CAST:2 * N_CAST], rest[2 * N_CAST:]
    blk = pl.program_id(1)
    cur = blk % 2

    def in_copy(k, b):
        r = stage_out[k].shape[0]
        return pltpu.make_async_copy(ew_in[k].at[pl.ds(pl.multiple_of(b * r, r), r)],
                                     stage_in[k].at[b % 2], sem_in.at[k, b % 2])

    def out_copy(k, b):
        r = stage_out[k].shape[0]
        return pltpu.make_async_copy(stage_out[k], ew_out[k].at[pl.ds(pl.multiple_of(b * r, r), r)],
                                     sem_out.at[k])

    @pl.when(blk == 0)
    def _():
        for k in range(N_CAST):
            in_copy(k, 0).start()

    @pl.when(blk + 1 < n_blk)
    def _():
        for k in range(N_CAST):
            in_copy(k, blk + 1).start()

    for k in range(N_CAST):
        in_copy(k, blk).wait()

    @pl.when(blk > 0)
    def _():
        for k in range(N_CAST):
            out_copy(k, blk - 1).wait()

    def project(x_ref, slot, chunks, with_norm):
        _project(x_ref, g_ref, w_ref, ws_ref, p_buf.at[slot], ps_buf.at[slot], h_s, chunks, with_norm)

    @pl.when(blk == 0)
    def _():
        project(x0_ref, 0, PROJ_CHUNKS, True)

    def first_side():
        for k in range(N_CAST):
            stage_out[k][...] = stage_in[k][cur].astype(BF16)
        project(xn_ref, 1 - cur, PROJ_CHUNKS[:2], True)

    p_cur = p_buf.at[cur]
    c0 = N_HEADS * DK
    views = [p_cur.at[:, lo:hi] for lo, hi in
             ((0, c0), (c0, 2 * c0), (1024, 2048), (2048, 3072), (3072, 4096), (4096, 5120), (5120, 6144))]
    side = ((first_side,)
            + tuple(functools.partial(project, xn_ref, 1 - cur, (c,), False) for c in PROJ_CHUNKS[2:])
            + (lambda: None,))
    assert len(side) == 2 + N_HEADS
    _mixer_kernel(rows, side, *views, ps_buf.at[cur], *mix_in, *mix_out, *mix_scr)

    for k in range(N_CAST):
        out_copy(k, blk).start()

    @pl.when(blk == n_blk - 1)
    def _():
        for k in range(N_CAST):
            out_copy(k, blk).wait()


def _prompt_mixers(xp, g, w_big, ws, rows, s0, c0, n0, m0, cv0, wts, cast_in):
    t_len = xp.shape[0]
    assert t_len % rows == 0
    n_blk = t_len // rows
    assert len(cast_in) == N_CAST and all(a.shape[0] % (16 * n_blk) == 0 for a in cast_in)
    once = pl.Buffered(1)

    def cmap(*zeros):
        return lambda s, b: zeros

    in_specs = [
        pl.BlockSpec((rows, D_MODEL), cmap(0, 0), pipeline_mode=once),
        pl.BlockSpec((rows, D_MODEL), lambda s, b: (jnp.minimum(b + 1, n_blk - 1), 0)),
        pl.BlockSpec((1, D_MODEL), cmap(0, 0)),
        pl.BlockSpec((D_MODEL, PROJ_BIG), cmap(0, 0), pipeline_mode=once),
        pl.BlockSpec((D_MODEL, 2 * LANES), cmap(0, 0), pipeline_mode=once),
        pl.BlockSpec((1, N_HEADS, DK, DV), cmap(0, 0, 0, 0)),
        pl.BlockSpec((1, N_HEADS, DK, DV), cmap(0, 0, 0, 0)),
        pl.BlockSpec((1, N_HEADS, DK), cmap(0, 0, 0)),
        pl.BlockSpec((1, 1, N_HEADS), cmap(0, 0, 0)),
        pl.BlockSpec((1, CONV_W - 1, QK_CH), cmap(0, 0, 0)),
    ] + [pl.BlockSpec(w.shape, cmap(0, 0)) for w in wts] + [pl.BlockSpec(memory_space=pl.ANY)] * N_CAST
    out_shape = [
        jax.ShapeDtypeStruct((t_len, D_MODEL), BF16),
        jax.ShapeDtypeStruct((1, N_HEADS, DK, DV), F32),
        jax.ShapeDtypeStruct((1, N_HEADS, DK, DV), F32),
        jax.ShapeDtypeStruct((1, N_HEADS, DK), F32),
        jax.ShapeDtypeStruct((1, 1, LANES), F32),
        jax.ShapeDtypeStruct((1, CONV_W - 1, QK_CH), F32),
    ] + [jax.ShapeDtypeStruct(a.shape, BF16) for a in cast_in]
    out_specs = [
        pl.BlockSpec((rows, D_MODEL), lambda s, b: (b, 0)),
        pl.BlockSpec((1, N_HEADS, DK, DV), cmap(0, 0, 0, 0)),
        pl.BlockSpec((1, N_HEADS, DK, DV), cmap(0, 0, 0, 0)),
        pl.BlockSpec((1, N_HEADS, DK), cmap(0, 0, 0)),
        pl.BlockSpec((1, 1, LANES), cmap(0, 0, 0)),
        pl.BlockSpec((1, CONV_W - 1, QK_CH), cmap(0, 0, 0)),
    ] + [pl.BlockSpec(memory_space=pl.ANY)] * N_CAST
    chunk = [(a.shape[0] // n_blk, a.shape[1]) for a in cast_in]
    scratch = [
        pltpu.VMEM((N_HEADS, DV, DK), F32),
        pltpu.VMEM((N_HEADS, DK, DV), F32),
        pltpu.VMEM((N_HEADS, 1, DK), F32),
        pltpu.VMEM((N_HEADS, 1, LANES), F32),
        pltpu.VMEM((rows + 8, QK_CH), F32),
        pltpu.VMEM((rows, N_HEADS * DK), F32),
        pltpu.VMEM((2, rows, PROJ_BIG), BF16),
        pltpu.VMEM((2, rows, LANES), F32),
        pltpu.VMEM((rows, D_MODEL), BF16),
    ] + [pltpu.VMEM((2,) + c, F32) for c in chunk] + [pltpu.VMEM(c, BF16) for c in chunk] + [
        pltpu.SemaphoreType.DMA((N_CAST, 2)), pltpu.SemaphoreType.DMA((N_CAST,))]
    return pl.pallas_call(
        functools.partial(_prompt_kernel, rows, n_blk),
        grid=(1, n_blk),
        in_specs=in_specs, out_specs=out_specs, out_shape=out_shape,
        scratch_shapes=scratch,
        compiler_params=pltpu.CompilerParams(
            dimension_semantics=("arbitrary", "arbitrary"), vmem_limit_bytes=VMEM_LIMIT_FUSED),
        name="prompt_inproj_mixers",
    )(xp, xp, g, w_big, ws, s0, c0, n0, m0, cv0, *wts, *cast_in)


def _pack_rows(ref, val):
    word = pltpu.pack_elementwise([val[:, :HALF], val[:, HALF:]], packed_dtype=BF16)
    rows = val.shape[0]
    for j in range(REC_ROWS):
        ref[pl.ds(j, rows, stride=REC_ROWS), :] = word[:, j * LANES:(j + 1) * LANES]


def _unpack_rows(ref, rows):
    word = jnp.concatenate(
        [ref[pl.ds(j, rows, stride=REC_ROWS), :] for j in range(REC_ROWS)], axis=1)
    lo = pltpu.unpack_elementwise(word, index=0, packed_dtype=BF16, unpacked_dtype=F32)
    hi = pltpu.unpack_elementwise(word, index=1, packed_dtype=BF16, unpacked_dtype=F32)
    return lo, hi


def _outproj_kernel(n_prompt_tiles, op_ref, os_ref, xp_ref, xs_ref, w_ref, g_ref, wr_ref,
                    x1_ref, xn_ref, route_ref, code_ref, cnt_ref, cnt_s):
    i = pl.program_id(0)

    @pl.when(i == 0)
    def _():
        cnt_s[...] = jnp.zeros_like(cnt_s)

    def body(o_ref, x_ref):
        x1 = x_ref[...] + _dot(o_ref[...], w_ref[...])
        x1_ref[...] = x1
        xn = _rmsnorm(x1, g_ref[...])
        _pack_rows(xn_ref, xn)
        xn_hi, xn_lo = _split2(xn)
        lg = _dot_hilo_cols(xn_hi, xn_lo, wr_ref)
        lane = lax.broadcasted_iota(jnp.int32, lg.shape, 1).astype(F32)
        neg = -jnp.inf
        lgm = jnp.where(lane < N_GROUPS, lg, neg)
        mg = jnp.max(lgm, axis=1, keepdims=True)
        g_idx = jnp.min(jnp.where(lgm == mg, lane, float(LANES)), axis=1, keepdims=True)
        g_val = 1.0 / jnp.sum(jnp.where(lane < N_GROUPS, jnp.exp(lg - mg), 0.0), axis=1, keepdims=True)
        e0 = ROUTER_E0 + EXPERTS_PER_GROUP * g_idx
        le = jnp.where((lane >= e0) & (lane < e0 + EXPERTS_PER_GROUP), lg, neg)
        v1 = jnp.max(le, axis=1, keepdims=True)
        i1 = jnp.min(jnp.where(le == v1, lane, float(LANES)), axis=1, keepdims=True)
        le2 = jnp.where(lane == i1, neg, le)
        v2 = jnp.max(le2, axis=1, keepdims=True)
        i2 = jnp.min(jnp.where(le2 == v2, lane, float(LANES)), axis=1, keepdims=True)
        t = jnp.exp(v2 - v1)
        w1 = g_val / (1.0 + t)
        w2 = g_val * t / (1.0 + t)
        oh1 = lane == i1
        oh2 = lane == i2
        hot = jnp.where(oh1 | oh2, 1.0, 0.0)
        r_i = lax.broadcasted_iota(jnp.int32, (ROW_TILE, ROW_TILE), 0)
        c_i = lax.broadcasted_iota(jnp.int32, (ROW_TILE, ROW_TILE), 1)
        before = jnp.where(c_i < r_i, 1.0, 0.0).astype(BF16)
        seen = _dot(before, hot.astype(BF16)) + cnt_s[...]
        rank1 = jnp.sum(jnp.where(oh1, seen, 0.0), axis=1, keepdims=True)
        rank2 = jnp.sum(jnp.where(oh2, seen, 0.0), axis=1, keepdims=True)
        cnt_s[...] += jnp.sum(hot, axis=0, keepdims=True)
        code1 = (i1 - ROUTER_E0) * float(SLOT_CODE) + rank1
        code2 = (i2 - ROUTER_E0) * float(SLOT_CODE) + rank2
        route = jnp.zeros_like(lg)
        for k, col in ((0, code1), (1, code2), (4, w1), (5, w2)):
            route = jnp.where(lane == k, col, route)
        route_ref[...] = route
        code_ref[0] = route.T[0:8, :].astype(jnp.int32)

    @pl.when(i < n_prompt_tiles)
    def _():
        body(op_ref, xp_ref)

    @pl.when(i >= n_prompt_tiles)
    def _():
        body(os_ref, xs_ref)

    cnt_ref[...] = cnt_s[...]


def _outproj(o_p, o_s, xp, xs, w_out, g, wr):
    n_p, n_s = xp.shape[0], xs.shape[0]
    npt = n_p // ROW_TILE
    n = n_p + n_s

    def pmap(i):
        return (jnp.minimum(i, npt - 1), 0)

    def cmap(i):
        return (0, 0)

    def omap(i):
        return (i, 0)

    return pl.pallas_call(
        functools.partial(_outproj_kernel, npt),
        grid=(npt + 1,),
        in_specs=[
            pl.BlockSpec((ROW_TILE, D_MODEL), pmap), pl.BlockSpec((ROW_TILE, D_MODEL), cmap),
            pl.BlockSpec((ROW_TILE, D_MODEL), pmap), pl.BlockSpec((ROW_TILE, D_MODEL), cmap),
            pl.BlockSpec((D_MODEL, D_MODEL), cmap), pl.BlockSpec((1, D_MODEL), cmap),
            pl.BlockSpec((D_MODEL, 2 * LANES), cmap),
        ],
        out_specs=[
            pl.BlockSpec((ROW_TILE, D_MODEL), omap), pl.BlockSpec((ROW_TILE * REC_ROWS, LANES), omap),
            pl.BlockSpec((ROW_TILE, LANES), omap), pl.BlockSpec((1, 8, ROW_TILE), lambda i: (i, 0, 0)),
            pl.BlockSpec((1, LANES), cmap),
        ],
        out_shape=[
            jax.ShapeDtypeStruct((n, D_MODEL), F32),
            jax.ShapeDtypeStruct((n * REC_ROWS, LANES), jnp.uint32),
            jax.ShapeDtypeStruct((n, LANES), F32),
            jax.ShapeDtypeStruct((npt + 1, 8, ROW_TILE), jnp.int32),
            jax.ShapeDtypeStruct((1, LANES), F32),
        ],
        scratch_shapes=[pltpu.VMEM((1, LANES), F32)],
        compiler_params=pltpu.CompilerParams(
            dimension_semantics=("arbitrary",), vmem_limit_bytes=VMEM_LIMIT),
        name="outproj_router",
    )(o_p, o_s, xp, xs, w_out, g, wr)


def _rec(ref, idx):
    return ref.at[pl.ds(pl.multiple_of(idx * REC_ROWS, REC_ROWS), REC_ROWS)]


def _zero_records_kernel(o_ref):
    z = jnp.zeros(o_ref.shape, F32)
    o_ref[...] = pltpu.pack_elementwise([z, z], packed_dtype=BF16)


def _slot_rows_kernel(row0_ref, code_ref, pos_ref):
    code = code_ref[...]
    expert = lax.shift_right_logical(code, SLOT_CODE_BITS)
    first = jnp.zeros_like(code)
    for e in range(N_EXPERTS):
        first = jnp.where(expert == e, row0_ref[e], first)
    pos_ref[...] = first + (code & (SLOT_CODE - 1))


def _slot_rows(row0, codes):
    return pl.pallas_call(
        _slot_rows_kernel,
        grid_spec=pltpu.PrefetchScalarGridSpec(
            num_scalar_prefetch=1,
            grid=(1,),
            in_specs=[pl.BlockSpec(codes.shape, lambda i, r: (0, 0, 0))],
            out_specs=pl.BlockSpec(codes.shape, lambda i, r: (0, 0, 0)),
        ),
        out_shape=jax.ShapeDtypeStruct(codes.shape, codes.dtype),
        name="moe_slot_rows",
    )(row0, codes)


def _dispatch_kernel(t_max, p1_ref, p2_ref, padlo_ref, padn_ref, nt_ref, xn_ref, xs_ref,
                     zero_s, sem, zsem):
    step = pl.program_id(0)
    base = step * ROW_TILE

    def pad_copy(row):
        return pltpu.make_async_copy(_rec(zero_s, 0), _rec(xs_ref, row), zsem)

    def tile_copy(t):
        rows = EXP_TILE * REC_ROWS
        return pltpu.make_async_copy(zero_s, xs_ref.at[pl.ds(pl.multiple_of(t * rows, rows), rows)], zsem)

    def for_pad_rows(fn):
        def per_expert(e, c):
            lo = padlo_ref[e]
            lax.fori_loop(0, padn_ref[e], lambda r, cc: (fn(pad_copy(lo + r)), cc)[1], 0)
            return c
        lax.fori_loop(0, N_EXPERTS, per_expert, 0)
        lax.fori_loop(nt_ref[0], t_max, lambda t, c: (fn(tile_copy(t)), c)[1], 0)

    @pl.when(step == 0)
    def _():
        _zero_records_kernel(zero_s)
        for_pad_rows(lambda cp: cp.start(priority=1))

    def copies(r):
        src = _rec(xn_ref, r)
        return (pltpu.make_async_copy(src, _rec(xs_ref, p1_ref[base + r]), sem),
                pltpu.make_async_copy(src, _rec(xs_ref, p2_ref[base + r]), sem))

    def start(g, c):
        for u in range(DMA_UNROLL):
            for prio, cp in enumerate(copies(g * DMA_UNROLL + u)):
                cp.start(priority=prio)
        return c

    def wait(g, c):
        for u in range(DMA_UNROLL):
            for cp in copies(g * DMA_UNROLL + u):
                cp.wait()
        return c

    lax.fori_loop(0, ROW_TILE // DMA_UNROLL, start, 0)
    lax.fori_loop(0, ROW_TILE // DMA_UNROLL, wait, 0)

    @pl.when(step == 0)
    def _():
        for_pad_rows(lambda cp: cp.wait())


def _dispatch(pos1, pos2, pad_lo, pad_n, n_tiles, xn_rec, t_max):
    n = pos1.shape[0]
    return pl.pallas_call(
        functools.partial(_dispatch_kernel, t_max),
        grid_spec=pltpu.PrefetchScalarGridSpec(
            num_scalar_prefetch=5,
            grid=(n // ROW_TILE,),
            in_specs=[pl.BlockSpec((ROW_TILE * REC_ROWS, LANES), lambda i, *_: (i, 0))],
            out_specs=pl.BlockSpec(memory_space=pl.ANY),
            scratch_shapes=[pltpu.VMEM((EXP_TILE * REC_ROWS, LANES), jnp.uint32),
                            pltpu.SemaphoreType.DMA, pltpu.SemaphoreType.DMA],
        ),
        out_shape=jax.ShapeDtypeStruct((t_max * EXP_TILE * REC_ROWS, LANES), jnp.uint32),
        compiler_params=pltpu.CompilerParams(
            dimension_semantics=("arbitrary",), vmem_limit_bytes=VMEM_LIMIT),
        name="moe_dispatch",
    )(pos1, pos2, pad_lo, pad_n, n_tiles, xn_rec)


def _experts_kernel(te_ref, nt_ref, xs_ref, wg_ref, wu_ref, wd_ref, ys_ref):
    del te_ref
    t = pl.program_id(0)

    @pl.when(t < nt_ref[0])
    def _():
        lo, hi = _unpack_rows(xs_ref, EXP_TILE)
        x = jnp.concatenate([lo.astype(BF16), hi.astype(BF16)], axis=1)
        hg = _dot(x, wg_ref[0])
        hu = _dot(x, wu_ref[0])
        act = hg * _sigmoid(hg) * hu
        _pack_rows(ys_ref, _dot(act.astype(BF16), wd_ref[0]))

    @pl.when(t >= nt_ref[0])
    def _():
        _zero_records_kernel(ys_ref)


def _experts(tile_expert, n_tiles, xs, wg, wu, wd):
    t_max = tile_expert.shape[0]

    def tmap(t, te, nt):
        return (jnp.minimum(t, nt[0] - 1), 0)

    def wmap(t, te, nt):
        return (te[t], 0, 0)

    return pl.pallas_call(
        _experts_kernel,
        grid_spec=pltpu.PrefetchScalarGridSpec(
            num_scalar_prefetch=2,
            grid=(t_max,),
            in_specs=[pl.BlockSpec((EXP_TILE * REC_ROWS, LANES), tmap),
                      pl.BlockSpec((1, D_MODEL, D_EXPERT), wmap),
                      pl.BlockSpec((1, D_MODEL, D_EXPERT), wmap),
                      pl.BlockSpec((1, D_EXPERT, D_MODEL), wmap)],
            out_specs=pl.BlockSpec((EXP_TILE * REC_ROWS, LANES), lambda t, te, nt: (t, 0)),
        ),
        out_shape=jax.ShapeDtypeStruct(xs.shape, xs.dtype),
        compiler_params=pltpu.CompilerParams(
            dimension_semantics=("arbitrary",), vmem_limit_bytes=VMEM_LIMIT),
        name="moe_experts",
    )(tile_expert, n_tiles, xs, wg, wu, wd)


def _combine_kernel(n_prompt_tiles, p1_ref, p2_ref, x1_ref, route_ref, g_ref, ys_ref,
                    yp_ref, ysm_ref, a_s, b_s, sem):
    i = pl.program_id(0)
    n_steps = pl.num_programs(0)

    def copies(step, slot, r):
        tok = step * ROW_TILE + r
        return (pltpu.make_async_copy(_rec(ys_ref, p1_ref[tok]), _rec(a_s.at[slot], r), sem.at[slot]),
                pltpu.make_async_copy(_rec(ys_ref, p2_ref[tok]), _rec(b_s.at[slot], r), sem.at[slot]))

    def start_all(step, slot):
        def start(g, c):
            for u in range(DMA_UNROLL):
                for prio, cp in enumerate(copies(step, slot, g * DMA_UNROLL + u)):
                    cp.start(priority=prio)
            return c
        lax.fori_loop(0, ROW_TILE // DMA_UNROLL, start, 0)

    def wait_all(step, slot):
        def wait(g, c):
            for u in range(DMA_UNROLL):
                for cp in copies(step, slot, g * DMA_UNROLL + u):
                    cp.wait()
            return c
        lax.fori_loop(0, ROW_TILE // DMA_UNROLL, wait, 0)

    slot = i % 2

    @pl.when(i == 0)
    def _():
        start_all(0, 0)

    @pl.when(i + 1 < n_steps)
    def _():
        start_all(i + 1, 1 - slot)

    wait_all(i, slot)
    a_lo, a_hi = _unpack_rows(a_s.at[slot], ROW_TILE)
    b_lo, b_hi = _unpack_rows(b_s.at[slot], ROW_TILE)
    route = route_ref[...]
    w1 = route[:, 4:5]
    w2 = route[:, 5:6]
    moe = jnp.concatenate([w1 * a_lo + w2 * b_lo, w1 * a_hi + w2 * b_hi], axis=1)
    y = _rmsnorm(x1_ref[...] + moe, g_ref[...])

    @pl.when(i < n_prompt_tiles)
    def _():
        yp_ref[...] = y

    @pl.when(i >= n_prompt_tiles)
    def _():
        ysm_ref[...] = y


def _combine(pos1, pos2, x1, route, g, ys, n_p):
    n = x1.shape[0]
    npt = n_p // ROW_TILE

    def omap(i, *_):
        return (i, 0)

    return pl.pallas_call(
        functools.partial(_combine_kernel, npt),
        grid_spec=pltpu.PrefetchScalarGridSpec(
            num_scalar_prefetch=2,
            grid=(n // ROW_TILE,),
            in_specs=[pl.BlockSpec((ROW_TILE, D_MODEL), omap),
                      pl.BlockSpec((ROW_TILE, LANES), omap),
                      pl.BlockSpec((1, D_MODEL), lambda i, *_: (0, 0)),
                      pl.BlockSpec(memory_space=pl.ANY)],
            out_specs=[pl.BlockSpec((ROW_TILE, D_MODEL), lambda i, *_: (jnp.minimum(i, npt - 1), 0)),
                       pl.BlockSpec((ROW_TILE, D_MODEL), lambda i, *_: (jnp.maximum(i - npt, 0), 0))],
            scratch_shapes=[pltpu.VMEM((2, ROW_TILE * REC_ROWS, LANES), jnp.uint32),
                            pltpu.VMEM((2, ROW_TILE * REC_ROWS, LANES), jnp.uint32),
                            pltpu.SemaphoreType.DMA((2,))],
        ),
        out_shape=[jax.ShapeDtypeStruct((n_p, D_MODEL), F32),
                   jax.ShapeDtypeStruct((n - n_p, D_MODEL), F32)],
        compiler_params=pltpu.CompilerParams(
            dimension_semantics=("arbitrary",), vmem_limit_bytes=VMEM_LIMIT),
        name="moe_combine",
    )(pos1, pos2, x1, route, g, ys)


def _pad_lanes(w):
    return jnp.pad(w, ((0, 0), (0, LANES - w.shape[1])))


def _moe_plan(counts, n_tokens):
    cnt = counts[0, ROUTER_E0:ROUTER_E0 + N_EXPERTS].astype(jnp.int32)
    tiles = (cnt + EXP_TILE - 1) // EXP_TILE
    tile_end = jnp.cumsum(tiles)
    row0 = (tile_end - tiles) * EXP_TILE
    t_max = 2 * n_tokens // EXP_TILE + N_EXPERTS
    tile_ids = jnp.arange(t_max, dtype=jnp.int32)
    tile_expert = jnp.minimum(
        jnp.sum((tile_ids[:, None] >= tile_end[None, :]).astype(jnp.int32), axis=1), N_EXPERTS - 1)
    return row0, row0 + cnt, tiles * EXP_TILE - cnt, tile_expert, tile_end[-1:].astype(jnp.int32), t_max


def kernel(x_prompt, x_sample, state_gla_S, state_mlstm_C, state_mlstm_n, state_mlstm_m, cache_mlstm_conv, g_mix_norm, w_in, w_gla_gate_up, b_gla_gate_up, g_gla_out, w_mlstm_conv, b_mlstm_conv, b_mlstm_i, b_mlstm_f, g_mlstm_out, w_out, g_ffn_norm, w_router_group, w_router_expert, w_exp_gate, w_exp_up, w_exp_down, g_final):
    depth = w_in.shape[0]
    assert depth == 1
    bp, t_p, _ = x_prompt.shape
    bs, t_s, _ = x_sample.shape
    assert bp == 1
    xp = x_prompt.reshape(bp * t_p, D_MODEL)
    xs = x_sample.reshape(bs * t_s, D_MODEL)
    n_p = xp.shape[0]

    wt = jnp.transpose(w_in[0])
    c_gz = 3072
    c_mqk = c_gz + GLA_GATE_RANK
    c_mi = c_mqk + 3072
    w_big = _pack_inproj_weight(wt, c_gz, c_mqk)
    ws = _hilo_cols(_pad_lanes(jnp.concatenate([wt[c_gz:c_mqk], wt[c_mi:]], axis=0).T))
    wz = jnp.pad(w_gla_gate_up[0], ((0, LANES - GLA_GATE_RANK), (0, 0)))
    wz_hi, wz_lo = _split2(wz)
    bsm = _pad_lanes(jnp.concatenate(
        [jnp.zeros((1, GLA_GATE_RANK), F32), b_mlstm_i[0][None], b_mlstm_f[0][None]], axis=1))
    mix_w = (wz_hi, wz_lo, b_gla_gate_up[0][None], g_gla_out[0][None], w_mlstm_conv[0],
             b_mlstm_conv[0][None], bsm, g_mlstm_out[0][None])
    wr = _hilo_cols(_pad_lanes(jnp.concatenate([w_router_group[0], w_router_expert[0]], axis=1)))

    g_mix = g_mix_norm[0][None]
    p_big, p_small = _inproj(xs, g_mix, w_big, ws)

    dt = x_prompt.dtype
    z_s = jnp.zeros((bp, N_HEADS, DK, DV), dt)
    z_n = jnp.zeros((bp, N_HEADS, DK), dt)
    z_m = jnp.zeros((bp, 1, N_HEADS), dt)
    z_cv = jnp.zeros((bp, CONV_W - 1, QK_CH), dt)
    expert_w = (w_exp_gate[0], w_exp_up[0], w_exp_down[0])
    o_p, p_S, p_C, p_n, p_m, p_cv, *expert_w_bf16 = _prompt_mixers(
        xp, g_mix, w_big, ws, MIX_ROWS, z_s, z_s, z_n, z_m, z_cv, mix_w,
        [a.reshape(-1, a.shape[-1]) for a in expert_w])
    wg_b, wu_b, wd_b = (b.reshape(a.shape) for a, b in zip(expert_w, expert_w_bf16))
    o_s, s_S, s_C, s_n, s_m, s_cv = _mixers(
        p_big, p_small, 0, bs, t_s, t_s, state_gla_S[0], state_mlstm_C[0], state_mlstm_n[0],
        state_mlstm_m[0][:, None, :], cache_mlstm_conv[0], mix_w)

    x1, xn_rec, route, codes, counts = _outproj(
        o_p, o_s, xp, xs, w_out[0].astype(BF16), g_ffn_norm[0][None], wr)
    row0, pad_lo, pad_n, tile_expert, n_tiles, t_max = _moe_plan(counts, x1.shape[0])
    pos = _slot_rows(row0, codes)
    pos1 = pos[:, 0, :].reshape(-1)
    pos2 = pos[:, 1, :].reshape(-1)
    xs_rec = _dispatch(pos1, pos2, pad_lo, pad_n, n_tiles, xn_rec, t_max)
    ys_rec = _experts(tile_expert, n_tiles, xs_rec, wg_b, wu_b, wd_b)
    y_p, y_s = _combine(pos1, pos2, x1, route, g_final[None], ys_rec, n_p)

    return (y_p.reshape(x_prompt.shape), y_s.reshape(x_sample.shape),
            p_S[None], p_C[None], p_n[None], p_m[:, 0, :N_HEADS][None], p_cv[None],
            s_S[None], s_C[None], s_n[None], s_m[:, 0, :N_HEADS][None], s_cv[None])
```

```python
import functools

import jax
import jax.numpy as jnp
from jax import lax
from jax.experimental import pallas as pl
from jax.experimental.pallas import tpu as pltpu

F32 = jnp.float32
BF16 = jnp.bfloat16

D_MODEL = 2048
N_HEADS = 4
DK = 128
DV = 256
GLA_GATE_RANK = 16
GLA_GATE_NORM = 16.0
CONV_W = 4
QK_CH = 2 * N_HEADS * DK
N_GROUPS = 4
EXPERTS_PER_GROUP = 8
N_EXPERTS = N_GROUPS * EXPERTS_PER_GROUP
D_EXPERT = 256
EPS = 1e-6

LANES = 128
GLA_SUB = 16
GLA_SAFE_DECAY = 60.0
MIX_ROWS = 256
SAMPLE_PAR = 4
N_MIX_OUT = 6
ROW_TILE = 512
PROJ_BIG = 6 * 1024
PROJ_CHUNK = 1024
PACK_COLS = 256
LANE_I = GLA_GATE_RANK
LANE_F = GLA_GATE_RANK + N_HEADS
ROUTER_E0 = N_GROUPS
HALF = D_MODEL // 2
REC_ROWS = HALF // LANES
EXP_TILE = 256
DMA_UNROLL = 8
SLOT_CODE_BITS = 16
SLOT_CODE = 1 << SLOT_CODE_BITS

VMEM_LIMIT = 56 * 1024 * 1024
VMEM_LIMIT_FUSED = 62 * 1024 * 1024


def _dot(a, b):
    return jnp.dot(a, b, preferred_element_type=F32)


def _dot_nt(a, b):
    return lax.dot_general(a, b, (((1,), (1,)), ((), ())), preferred_element_type=F32)


def _dot_tn(a, b):
    return lax.dot_general(a, b, (((0,), (0,)), ((), ())), preferred_element_type=F32)


def _split2(x):
    hi = x.astype(BF16)
    lo = (x - hi.astype(F32)).astype(BF16)
    return hi, lo


def _split3(x):
    hi = x.astype(BF16)
    r = x - hi.astype(F32)
    mid = r.astype(BF16)
    lo = (r - mid.astype(F32)).astype(BF16)
    return hi, mid, lo


def _dot_exact_lhs(m_bf16, x):
    hi, mid, lo = _split3(x)
    return _dot(m_bf16, hi) + _dot(m_bf16, mid) + _dot(m_bf16, lo)


def _dot_hilo(a_hi, a_lo, b_hi, b_lo):
    return _dot(a_hi, b_hi) + _dot(a_lo, b_hi) + _dot(a_hi, b_lo)


def _hilo_cols(w):
    return jnp.concatenate(_split2(w), axis=1)


def _dot_hilo_cols(a_hi, a_lo, b_ref):
    p = _dot(a_hi, b_ref[...])
    return (p[:, LANES:] + _dot(a_lo, b_ref[:, :LANES])) + p[:, :LANES]


def _log_sigmoid(z):
    return jnp.minimum(z, 0.0) - jnp.log1p(jnp.exp(-jnp.abs(z)))


def _sigmoid(z):
    return 0.5 * jnp.tanh(0.5 * z) + 0.5


def _rmsnorm(x, g):
    return x * lax.rsqrt(jnp.mean(x * x, axis=-1, keepdims=True) + EPS) * g


def _pack_inproj_weight_kernel(c_gz, c_mqk, wt_ref, o_ref, buf, sem):
    j = pl.program_id(0)
    n_j = pl.num_programs(0)

    def copy(jj):
        col0 = jj * PACK_COLS
        row = pl.multiple_of(col0 + jnp.where(col0 >= c_gz, c_mqk - c_gz, 0), 16)
        return pltpu.make_async_copy(wt_ref.at[pl.ds(row, PACK_COLS)], buf.at[jj % 2], sem.at[jj % 2])

    @pl.when(j == 0)
    def _():
        copy(0).start()

    @pl.when(j + 1 < n_j)
    def _():
        copy(j + 1).start()

    copy(j).wait()
    r_i = lax.broadcasted_iota(jnp.int32, (PACK_COLS, PACK_COLS), 0)
    c_i = lax.broadcasted_iota(jnp.int32, (PACK_COLS, PACK_COLS), 1)
    eye = jnp.where(r_i == c_i, 1.0, 0.0).astype(BF16)
    o_ref[...] = _dot_tn(buf[j % 2].astype(BF16), eye).astype(o_ref.dtype)


def _pack_inproj_weight(wt, c_gz, c_mqk):
    assert c_gz % PACK_COLS == 0 and (c_mqk - c_gz) % 16 == 0
    return pl.pallas_call(
        functools.partial(_pack_inproj_weight_kernel, c_gz, c_mqk),
        grid=(PROJ_BIG // PACK_COLS,),
        in_specs=[pl.BlockSpec(memory_space=pl.ANY)],
        out_specs=pl.BlockSpec((wt.shape[1], PACK_COLS), lambda j: (0, j)),
        out_shape=jax.ShapeDtypeStruct((wt.shape[1], PROJ_BIG), BF16),
        scratch_shapes=[pltpu.VMEM((2, PACK_COLS, wt.shape[1]), F32), pltpu.SemaphoreType.DMA((2,))],
        compiler_params=pltpu.CompilerParams(dimension_semantics=("arbitrary",), vmem_limit_bytes=VMEM_LIMIT),
        name="pack_inproj_weight",
    )(wt)


def _project(x_ref, g_ref, w_ref, ws_ref, p_ref, ps_ref, h_ref, chunks, with_norm):
    if with_norm:
        y_hi, y_lo = _split2(_rmsnorm(x_ref[...], g_ref[...]))
        h_ref[...] = y_hi
        ps_ref[...] = _dot_hilo_cols(y_hi, y_lo, ws_ref)
    else:
        y_hi = h_ref[...]
    for c in chunks:
        cols = slice(c * PROJ_CHUNK, (c + 1) * PROJ_CHUNK)
        p_ref[:, cols] = _dot(y_hi, w_ref[:, cols]).astype(p_ref.dtype)


PROJ_CHUNKS = tuple(range(PROJ_BIG // PROJ_CHUNK))


def _inproj_kernel(x_ref, g_ref, w_ref, ws_ref, p_ref, ps_ref, h_s):
    _project(x_ref, g_ref, w_ref, ws_ref, p_ref, ps_ref, h_s, PROJ_CHUNKS, True)


def _inproj(x, g, w_big, ws):
    n = x.shape[0]
    assert n % ROW_TILE == 0
    once = pl.Buffered(1)
    return pl.pallas_call(
        _inproj_kernel,
        grid=(n // ROW_TILE,),
        in_specs=[
            pl.BlockSpec((ROW_TILE, D_MODEL), lambda i: (i, 0)),
            pl.BlockSpec((1, D_MODEL), lambda i: (0, 0)),
            pl.BlockSpec((D_MODEL, PROJ_BIG), lambda i: (0, 0), pipeline_mode=once),
            pl.BlockSpec((D_MODEL, 2 * LANES), lambda i: (0, 0), pipeline_mode=once),
        ],
        out_specs=[
            pl.BlockSpec((ROW_TILE, PROJ_BIG), lambda i: (i, 0)),
            pl.BlockSpec((ROW_TILE, LANES), lambda i: (i, 0)),
        ],
        out_shape=[
            jax.ShapeDtypeStruct((n, PROJ_BIG), BF16),
            jax.ShapeDtypeStruct((n, LANES), F32),
        ],
        scratch_shapes=[pltpu.VMEM((ROW_TILE, D_MODEL), BF16)],
        compiler_params=pltpu.CompilerParams(
            dimension_semantics=("arbitrary",), vmem_limit_bytes=VMEM_LIMIT),
        name="inproj",
    )(x, g, w_big, ws)


def _run_streams(streams):
    stages = [next(s) for s in streams]
    ok = stages[0][0]
    for st in stages[1:]:
        ok = jnp.logical_and(ok, st[0])

    @pl.when(ok)
    def _():
        for st in stages:
            st[1]()

    @pl.when(jnp.logical_not(ok))
    def _():
        for st in stages:
            st[2]()

    for s in streams:
        for _ in s:
            pass


def _mixer_kernel(rows, side_work, *refs):
    _run_streams([_mixer_stream(rows, False, side_work, *refs)])


def _mixer_stream(rows, single, side_work, gq_ref, gk_ref, gv_ref, gg_ref, mqk_ref, mv_ref, mo_ref, sm_ref,
                  s0_ref, c0_ref, n0_ref, m0_ref, cv0_ref,
                  wz_hi_ref, wz_lo_ref, bz_ref, g_gla_ref, wc_ref, bc_ref, bsm_ref, g_ml_ref,
                  o_ref, s_out_ref, c_out_ref, n_out_ref, m_out_ref, cv_out_ref,
                  st_s, c_s, n_s, m_s, cb_s, b_s):
    blk = pl.program_id(1)
    n_blk = pl.num_programs(1)

    def maybe(cond, fn):
        if single:
            fn()
        else:
            pl.when(cond)(fn)

    def init():
        for h in range(N_HEADS):
            st_s[h] = s0_ref[0, h].T
            c_s[h] = c0_ref[0, h]
            n_s[h] = n0_ref[0, h:h + 1, :]
            m_s[h] = jnp.broadcast_to(m0_ref[0, :, h:h + 1], (1, LANES))
        cb_s[0:8, :] = jnp.zeros((8, QK_CH), F32)
        cb_s[8 - (CONV_W - 1):8, :] = cv0_ref[0]

    maybe(blk == 0, init)

    small = sm_ref[...]
    row_i = lax.broadcasted_iota(jnp.int32, (rows, rows), 0)
    col_i = lax.broadcasted_iota(jnp.int32, (rows, rows), 1)
    causal = col_i <= row_i

    sm_hi, sm_lo = _split2(small)
    z = _dot_hilo(sm_hi, sm_lo, wz_hi_ref[...], wz_lo_ref[...]) + bz_ref[...]
    log_a = _log_sigmoid(z) * (1.0 / GLA_GATE_NORM)
    tri = jnp.where(causal, 1.0, 0.0).astype(BF16)
    b_blk = _dot_exact_lhs(tri, log_a)
    b_last = b_blk[rows - 1:rows, :]
    factorable = jnp.min(b_last) >= -GLA_SAFE_DECAY
    side_work[0]()

    def gla_finish(o, vc, gate):
        y = o * lax.rsqrt(jnp.mean(o * o, axis=-1, keepdims=True) + EPS) * g_gla_ref[:, vc]
        return (y * (gate * _sigmoid(gate))).astype(o_ref.dtype)

    def gla_fast():
        for h in range(N_HEADS):
            kc = slice(h * DK, (h + 1) * DK)
            vc = slice(h * DV, (h + 1) * DV)
            bh = b_blk[:, kc]
            qh = gq_ref[:, kc].astype(F32) * (DK ** -0.5)
            kh = gk_ref[:, kc].astype(F32)
            vh = gv_ref[:, vc]
            st = st_s[h]
            q_dec = (qh * jnp.exp(bh)).astype(BF16)
            k_inv = (kh * jnp.exp(-bh)).astype(BF16)
            a = jnp.where(causal, _dot_nt(q_dec, k_inv), 0.0)
            o = _dot(a.astype(BF16), vh) + _dot_nt(q_dec, st.astype(BF16))
            bh_end = b_last[:, kc]
            k_dec = (kh * jnp.exp(bh_end - bh)).astype(BF16)
            st_s[h] = st * jnp.exp(bh_end) + _dot_tn(vh, k_dec)
            o_ref[:, vc] = gla_finish(o, vc, gg_ref[:, vc].astype(F32))

    sub_r = lax.broadcasted_iota(jnp.int32, (GLA_SUB, LANES), 0)
    sub_l = lax.broadcasted_iota(jnp.int32, (GLA_SUB, LANES), 1)

    def gla_sub(c, carry):
        r0 = pl.multiple_of(c * GLA_SUB, GLA_SUB)
        rs = pl.ds(r0, GLA_SUB)
        for h in range(N_HEADS):
            kc = slice(h * DK, (h + 1) * DK)
            vc = slice(h * DV, (h + 1) * DV)
            bh = b_s[rs, kc]
            qh = gq_ref[rs, kc].astype(F32) * (DK ** -0.5)
            kh = gk_ref[rs, kc].astype(F32)
            vh = gv_ref[rs, vc]
            st = st_s[h]
            o = _dot_nt((qh * jnp.exp(bh)).astype(BF16), st.astype(BF16))
            a = jnp.zeros((GLA_SUB, LANES), F32)
            for s in range(GLA_SUB):
                e = jnp.exp(jnp.minimum(bh - bh[s:s + 1, :], 0.0))
                col = jnp.sum(qh * (kh[s:s + 1, :] * e), axis=1, keepdims=True)
                a = jnp.where((sub_l == s) & (sub_r >= s), col, a)
            o = o + _dot(a[:, :GLA_SUB].astype(BF16), vh)
            b_end = bh[GLA_SUB - 1:GLA_SUB, :]
            k_dec = kh * jnp.exp(b_end - bh)
            st_s[h] = st * jnp.exp(b_end) + _dot_tn(vh, k_dec.astype(BF16))
            o_ref[rs, vc] = gla_finish(o, vc, gg_ref[rs, vc].astype(F32))
        return carry

    def gla_slow():
        same_sub = (row_i // GLA_SUB) == (col_i // GLA_SUB)
        blk_tri = jnp.where(causal & same_sub, 1.0, 0.0).astype(BF16)
        b_s[...] = _dot_exact_lhs(blk_tri, log_a)
        lax.fori_loop(0, rows // GLA_SUB, gla_sub, 0)

    yield factorable, gla_fast, gla_slow

    cb_s[8:8 + rows, :] = mqk_ref[...].astype(F32)
    conv = bc_ref[...]
    for j in range(CONV_W):
        conv = conv + cb_s[8 - (CONV_W - 1) + j:8 - (CONV_W - 1) + j + rows, :] * wc_ref[j:j + 1, :]
    cb_s[0:8, :] = cb_s[rows:rows + 8, :]
    qk = conv * _sigmoid(conv)

    pre = small + bsm_ref[...]
    log_f = _log_sigmoid(pre)
    f_cum =_dot_exact_lhs(tri, log_f)
    eye = jnp.where(row_i == col_i, 1.0, 0.0).astype(BF16)
    f_cum_t = sum(_dot_tn(p, eye) for p in _split3(f_cum))
    pre_t = sum(_dot_tn(p, eye) for p in _split3(pre))
    side_work[1]()
    lane_1 = lax.broadcasted_iota(jnp.int32, (1, LANES), 1)
    m_new = jnp.zeros((1, LANES), F32)

    for h in range(N_HEADS):
        kc = slice(h * DK, (h + 1) * DK)
        vc = slice(h * DV, (h + 1) * DV)
        f_col = f_cum[:, LANE_F + h:LANE_F + h + 1]
        i_col = pre[:, LANE_I + h:LANE_I + h + 1]
        f_row = f_cum_t[LANE_F + h:LANE_F + h + 1, :]
        i_row = pre_t[LANE_I + h:LANE_I + h + 1, :]
        dm = jnp.where(causal, f_col - f_row + i_row, -jnp.inf)
        m_prev = m_s[h][:, 0:1]
        inter = f_col + m_prev
        m_t = jnp.maximum(inter, jnp.max(dm, axis=1, keepdims=True))
        w = jnp.exp(dm - m_t)
        w_inter = jnp.exp(inter - m_t)
        q = qk[:, kc]
        k = qk[:, N_HEADS * DK + h * DK:N_HEADS * DK + (h + 1) * DK] * (DK ** -0.5)
        v = mv_ref[:, vc]
        q_b = q.astype(BF16)
        c_prev = c_s[h]
        n_prev = n_s[h]
        s_qk = _dot_nt(q_b, k.astype(BF16)) * w
        num = _dot(s_qk.astype(BF16), v) + w_inter * _dot(q_b, c_prev.astype(BF16))
        den = (jnp.sum(s_qk, axis=1, keepdims=True)
               + w_inter * jnp.sum(q * n_prev, axis=1, keepdims=True))
        hh = num / jnp.maximum(jnp.abs(den), jnp.exp(-m_t))
        m_end = m_t[rows - 1:rows, :]
        w_s = jnp.exp(f_col[rows - 1:rows, :] - f_col + i_col - m_end)
        dec = w_inter[rows - 1:rows, :]
        k_w = k * w_s
        c_s[h] = dec * c_prev + _dot_tn(k_w.astype(BF16), v)
        side_work[2 + h]()
        n_s[h] = dec * n_prev + jnp.sum(k_w, axis=0, keepdims=True)
        m_s[h] = jnp.broadcast_to(m_end, (1, LANES))
        m_new = jnp.where(lane_1 == h, m_end, m_new)
        y = hh * lax.rsqrt(jnp.mean(hh * hh, axis=-1, keepdims=True) + EPS) * g_ml_ref[:, vc]
        o_ref[:, N_HEADS * DV + h * DV:N_HEADS * DV + (h + 1) * DV] = (
            y * _sigmoid(mo_ref[:, vc].astype(F32))).astype(o_ref.dtype)

    def final():
        for h in range(N_HEADS):
            s_out_ref[0, h] = st_s[h].T
            c_out_ref[0, h] = c_s[h]
            n_out_ref[0, h:h + 1, :] = n_s[h]
        m_out_ref[0] = m_new
        cv_out_ref[0] = cb_s[8 - (CONV_W - 1):8, :]

    maybe(blk == n_blk - 1, final)


def _sample_kernel(rows, n_par, *refs):
    n_row_in, n_state_in, n_w = 8, 5, 8
    row_in, state_in = refs[:n_row_in], refs[n_row_in:n_row_in + n_state_in]
    weights = refs[n_row_in + n_state_in:n_row_in + n_state_in + n_w]
    outs = refs[n_row_in + n_state_in + n_w:]
    o_ref, state_out, scratch = outs[0], outs[1:N_MIX_OUT], outs[N_MIX_OUT:]
    no_side = (lambda: None,) * (2 + N_HEADS)
    streams = []
    for g in range(n_par):
        rs = pl.ds(g * rows, rows)
        streams.append(_mixer_stream(
            rows, True, no_side,
            *[r.at[rs] for r in row_in], *[r.at[pl.ds(g, 1)] for r in state_in], *weights,
            o_ref.at[rs], *[r.at[pl.ds(g, 1)] for r in state_out], *[s.at[g] for s in scratch]))
    _run_streams(streams)


def _sample_mixers(p_big, p_small, n_streams, rows, n_par, s0, c0, n0, m0, cv0, wts):
    assert n_streams % n_par == 0 and rows % GLA_SUB == 0
    blk_rows = n_par * rows

    def rmap(col):
        return lambda s, b: (s, col)

    def smap(*zeros):
        return lambda s, b: (s,) + zeros

    def wmap(s, b):
        return (0, 0)

    in_specs = [
        pl.BlockSpec((blk_rows, N_HEADS * DK), rmap(0)),
        pl.BlockSpec((blk_rows, N_HEADS * DK), rmap(1)),
        pl.BlockSpec((blk_rows, N_HEADS * DV), rmap(1)),
        pl.BlockSpec((blk_rows, N_HEADS * DV), rmap(2)),
        pl.BlockSpec((blk_rows, QK_CH), rmap(3)),
        pl.BlockSpec((blk_rows, N_HEADS * DV), rmap(4)),
        pl.BlockSpec((blk_rows, N_HEADS * DV), rmap(5)),
        pl.BlockSpec((blk_rows, LANES), rmap(0)),
        pl.BlockSpec((n_par, N_HEADS, DK, DV), smap(0, 0, 0)),
        pl.BlockSpec((n_par, N_HEADS, DK, DV), smap(0, 0, 0)),
        pl.BlockSpec((n_par, N_HEADS, DK), smap(0, 0)),
        pl.BlockSpec((n_par, 1, N_HEADS), smap(0, 0)),
        pl.BlockSpec((n_par, CONV_W - 1, QK_CH), smap(0, 0)),
    ] + [pl.BlockSpec(w.shape, wmap) for w in wts]
    out_shape = [
        jax.ShapeDtypeStruct((n_streams * rows, D_MODEL), BF16),
        jax.ShapeDtypeStruct((n_streams, N_HEADS, DK, DV), F32),
        jax.ShapeDtypeStruct((n_streams, N_HEADS, DK, DV), F32),
        jax.ShapeDtypeStruct((n_streams, N_HEADS, DK), F32),
        jax.ShapeDtypeStruct((n_streams, 1, LANES), F32),
        jax.ShapeDtypeStruct((n_streams, CONV_W - 1, QK_CH), F32),
    ]
    out_specs = [
        pl.BlockSpec((blk_rows, D_MODEL), smap(0)),
        pl.BlockSpec((n_par, N_HEADS, DK, DV), smap(0, 0, 0)),
        pl.BlockSpec((n_par, N_HEADS, DK, DV), smap(0, 0, 0)),
        pl.BlockSpec((n_par, N_HEADS, DK), smap(0, 0)),
        pl.BlockSpec((n_par, 1, LANES), smap(0, 0)),
        pl.BlockSpec((n_par, CONV_W - 1, QK_CH), smap(0, 0)),
    ]
    scratch = [
        pltpu.VMEM((n_par, N_HEADS, DV, DK), F32),
        pltpu.VMEM((n_par, N_HEADS, DK, DV), F32),
        pltpu.VMEM((n_par, N_HEADS, 1, DK), F32),
        pltpu.VMEM((n_par, N_HEADS, 1, LANES), F32),
        pltpu.VMEM((n_par, rows + 8, QK_CH), F32),
        pltpu.VMEM((n_par, rows, N_HEADS * DK), F32),
    ]
    return pl.pallas_call(
        functools.partial(_sample_kernel, rows, n_par),
        grid=(n_streams // n_par, 1),
        in_specs=in_specs, out_specs=out_specs, out_shape=out_shape,
        scratch_shapes=scratch,
        compiler_params=pltpu.CompilerParams(
            dimension_semantics=("arbitrary", "arbitrary"), vmem_limit_bytes=VMEM_LIMIT),
        name=f"mixers_r{rows}x{n_par}",
    )(p_big, p_big, p_big, p_big, p_big, p_big, p_big, p_small, s0, c0, n0, m0, cv0, *wts)


def _prompt_kernel(rows, x0_ref, xn_ref, g_ref, w_ref, ws_ref, *refs):
    mixer_refs, (p_buf, ps_buf, h_s) = refs[:-3], refs[-3:]
    blk = pl.program_id(1)
    cur = blk % 2

    def project(x_ref, slot, chunks, with_norm):
        _project(x_ref, g_ref, w_ref, ws_ref, p_buf.at[slot], ps_buf.at[slot], h_s, chunks, with_norm)

    @pl.when(blk == 0)
    def _():
        project(x0_ref, 0, PROJ_CHUNKS, True)

    p_cur = p_buf.at[cur]
    c0 = N_HEADS * DK
    views = [p_cur.at[:, lo:hi] for lo, hi in
             ((0, c0), (c0, 2 * c0), (1024, 2048), (2048, 3072), (3072, 4096), (4096, 5120), (5120, 6144))]
    side = ((lambda: project(xn_ref, 1 - cur, PROJ_CHUNKS[:2], True),)
            + tuple(functools.partial(project, xn_ref, 1 - cur, (c,), False) for c in PROJ_CHUNKS[2:])
            + (lambda: None,))
    assert len(side) == 2 + N_HEADS
    _mixer_kernel(rows, side, *views, ps_buf.at[cur], *mixer_refs)


def _prompt_mixers(xp, g, w_big, ws, rows, s0, c0, n0, m0, cv0, wts):
    t_len = xp.shape[0]
    assert t_len % rows == 0
    n_blk = t_len // rows
    once = pl.Buffered(1)

    def cmap(*zeros):
        return lambda s, b: zeros

    in_specs = [
        pl.BlockSpec((rows, D_MODEL), cmap(0, 0), pipeline_mode=once),
        pl.BlockSpec((rows, D_MODEL), lambda s, b: (jnp.minimum(b + 1, n_blk - 1), 0)),
        pl.BlockSpec((1, D_MODEL), cmap(0, 0)),
        pl.BlockSpec((D_MODEL, PROJ_BIG), cmap(0, 0), pipeline_mode=once),
        pl.BlockSpec((D_MODEL, 2 * LANES), cmap(0, 0), pipeline_mode=once),
        pl.BlockSpec((1, N_HEADS, DK, DV), cmap(0, 0, 0, 0)),
        pl.BlockSpec((1, N_HEADS, DK, DV), cmap(0, 0, 0, 0)),
        pl.BlockSpec((1, N_HEADS, DK), cmap(0, 0, 0)),
        pl.BlockSpec((1, 1, N_HEADS), cmap(0, 0, 0)),
        pl.BlockSpec((1, CONV_W - 1, QK_CH), cmap(0, 0, 0)),
    ] + [pl.BlockSpec(w.shape, cmap(0, 0)) for w in wts]
    out_shape = [
        jax.ShapeDtypeStruct((t_len, D_MODEL), BF16),
        jax.ShapeDtypeStruct((1, N_HEADS, DK, DV), F32),
        jax.ShapeDtypeStruct((1, N_HEADS, DK, DV), F32),
        jax.ShapeDtypeStruct((1, N_HEADS, DK), F32),
        jax.ShapeDtypeStruct((1, 1, LANES), F32),
        jax.ShapeDtypeStruct((1, CONV_W - 1, QK_CH), F32),
    ]
    out_specs = [
        pl.BlockSpec((rows, D_MODEL), lambda s, b: (b, 0)),
        pl.BlockSpec((1, N_HEADS, DK, DV), cmap(0, 0, 0, 0)),
        pl.BlockSpec((1, N_HEADS, DK, DV), cmap(0, 0, 0, 0)),
        pl.BlockSpec((1, N_HEADS, DK), cmap(0, 0, 0)),
        pl.BlockSpec((1, 1, LANES), cmap(0, 0, 0)),
        pl.BlockSpec((1, CONV_W - 1, QK_CH), cmap(0, 0, 0)),
    ]
    scratch = [
        pltpu.VMEM((N_HEADS, DV, DK), F32),
        pltpu.VMEM((N_HEADS, DK, DV), F32),
        pltpu.VMEM((N_HEADS, 1, DK), F32),
        pltpu.VMEM((N_HEADS, 1, LANES), F32),
        pltpu.VMEM((rows + 8, QK_CH), F32),
        pltpu.VMEM((rows, N_HEADS * DK), F32),
        pltpu.VMEM((2, rows, PROJ_BIG), BF16),
        pltpu.VMEM((2, rows, LANES), F32),
        pltpu.VMEM((rows, D_MODEL), BF16),
    ]
    return pl.pallas_call(
        functools.partial(_prompt_kernel, rows),
        grid=(1, n_blk),
        in_specs=in_specs, out_specs=out_specs, out_shape=out_shape,
        scratch_shapes=scratch,
        compiler_params=pltpu.CompilerParams(
            dimension_semantics=("arbitrary", "arbitrary"), vmem_limit_bytes=VMEM_LIMIT_FUSED),
        name="prompt_inproj_mixers",
    )(xp, xp, g, w_big, ws, s0, c0, n0, m0, cv0, *wts)


def _pack_rows(ref, val):
    word = pltpu.pack_elementwise([val[:, :HALF], val[:, HALF:]], packed_dtype=BF16)
    rows = val.shape[0]
    for j in range(REC_ROWS):
        ref[pl.ds(j, rows, stride=REC_ROWS), :] = word[:, j * LANES:(j + 1) * LANES]


def _unpack_rows(ref, rows):
    word = jnp.concatenate(
        [ref[pl.ds(j, rows, stride=REC_ROWS), :] for j in range(REC_ROWS)], axis=1)
    lo = pltpu.unpack_elementwise(word, index=0, packed_dtype=BF16, unpacked_dtype=F32)
    hi = pltpu.unpack_elementwise(word, index=1, packed_dtype=BF16, unpacked_dtype=F32)
    return lo, hi


def _outproj_kernel(n_prompt_tiles, op_ref, os_ref, xp_ref, xs_ref, w_ref, g_ref, wr_ref,
                    x1_ref, xn_ref, route_ref, code_ref, cnt_ref, cnt_s):
    i = pl.program_id(0)

    @pl.when(i == 0)
    def _():
        cnt_s[...] = jnp.zeros_like(cnt_s)

    def body(o_ref, x_ref):
        x1 = x_ref[...] + _dot(o_ref[...], w_ref[...])
        x1_ref[...] = x1
        xn = _rmsnorm(x1, g_ref[...])
        _pack_rows(xn_ref, xn)
        xn_hi, xn_lo = _split2(xn)
        lg = _dot_hilo_cols(xn_hi, xn_lo, wr_ref)
        lane = lax.broadcasted_iota(jnp.int32, lg.shape, 1).astype(F32)
        neg = -jnp.inf
        lgm = jnp.where(lane < N_GROUPS, lg, neg)
        mg = jnp.max(lgm, axis=1, keepdims=True)
        g_idx = jnp.min(jnp.where(lgm == mg, lane, float(LANES)), axis=1, keepdims=True)
        g_val = 1.0 / jnp.sum(jnp.where(lane < N_GROUPS, jnp.exp(lg - mg), 0.0), axis=1, keepdims=True)
        e0 = ROUTER_E0 + EXPERTS_PER_GROUP * g_idx
        le = jnp.where((lane >= e0) & (lane < e0 + EXPERTS_PER_GROUP), lg, neg)
        v1 = jnp.max(le, axis=1, keepdims=True)
        i1 = jnp.min(jnp.where(le == v1, lane, float(LANES)), axis=1, keepdims=True)
        le2 = jnp.where(lane == i1, neg, le)
        v2 = jnp.max(le2, axis=1, keepdims=True)
        i2 = jnp.min(jnp.where(le2 == v2, lane, float(LANES)), axis=1, keepdims=True)
        t = jnp.exp(v2 - v1)
        w1 = g_val / (1.0 + t)
        w2 = g_val * t / (1.0 + t)
        oh1 = lane == i1
        oh2 = lane == i2
        hot = jnp.where(oh1 | oh2, 1.0, 0.0)
        r_i = lax.broadcasted_iota(jnp.int32, (ROW_TILE, ROW_TILE), 0)
        c_i = lax.broadcasted_iota(jnp.int32, (ROW_TILE, ROW_TILE), 1)
        before = jnp.where(c_i < r_i, 1.0, 0.0).astype(BF16)
        seen = _dot(before, hot.astype(BF16)) + cnt_s[...]
        rank1 = jnp.sum(jnp.where(oh1, seen, 0.0), axis=1, keepdims=True)
        rank2 = jnp.sum(jnp.where(oh2, seen, 0.0), axis=1, keepdims=True)
        cnt_s[...] += jnp.sum(hot, axis=0, keepdims=True)
        code1 = (i1 - ROUTER_E0) * float(SLOT_CODE) + rank1
        code2 = (i2 - ROUTER_E0) * float(SLOT_CODE) + rank2
        route = jnp.zeros_like(lg)
        for k, col in ((0, code1), (1, code2), (4, w1), (5, w2)):
            route = jnp.where(lane == k, col, route)
        route_ref[...] = route
        code_ref[0] = route.T[0:8, :].astype(jnp.int32)

    @pl.when(i < n_prompt_tiles)
    def _():
        body(op_ref, xp_ref)

    @pl.when(i >= n_prompt_tiles)
    def _():
        body(os_ref, xs_ref)

    cnt_ref[...] = cnt_s[...]


def _outproj(o_p, o_s, xp, xs, w_out, g, wr):
    n_p, n_s = xp.shape[0], xs.shape[0]
    npt = n_p // ROW_TILE
    n = n_p + n_s

    def pmap(i):
        return (jnp.minimum(i, npt - 1), 0)

    def cmap(i):
        return (0, 0)

    def omap(i):
        return (i, 0)

    return pl.pallas_call(
        functools.partial(_outproj_kernel, npt),
        grid=(npt + 1,),
        in_specs=[
            pl.BlockSpec((ROW_TILE, D_MODEL), pmap), pl.BlockSpec((ROW_TILE, D_MODEL), cmap),
            pl.BlockSpec((ROW_TILE, D_MODEL), pmap), pl.BlockSpec((ROW_TILE, D_MODEL), cmap),
            pl.BlockSpec((D_MODEL, D_MODEL), cmap), pl.BlockSpec((1, D_MODEL), cmap),
            pl.BlockSpec((D_MODEL, 2 * LANES), cmap),
        ],
        out_specs=[
            pl.BlockSpec((ROW_TILE, D_MODEL), omap), pl.BlockSpec((ROW_TILE * REC_ROWS, LANES), omap),
            pl.BlockSpec((ROW_TILE, LANES), omap), pl.BlockSpec((1, 8, ROW_TILE), lambda i: (i, 0, 0)),
            pl.BlockSpec((1, LANES), cmap),
        ],
        out_shape=[
            jax.ShapeDtypeStruct((n, D_MODEL), F32),
            jax.ShapeDtypeStruct((n * REC_ROWS, LANES), jnp.uint32),
            jax.ShapeDtypeStruct((n, LANES), F32),
            jax.ShapeDtypeStruct((npt + 1, 8, ROW_TILE), jnp.int32),
            jax.ShapeDtypeStruct((1, LANES), F32),
        ],
        scratch_shapes=[pltpu.VMEM((1, LANES), F32)],
        compiler_params=pltpu.CompilerParams(
            dimension_semantics=("arbitrary",), vmem_limit_bytes=VMEM_LIMIT),
        name="outproj_router",
    )(o_p, o_s, xp, xs, w_out, g, wr)


def _rec(ref, idx):
    return ref.at[pl.ds(pl.multiple_of(idx * REC_ROWS, REC_ROWS), REC_ROWS)]


def _zero_records_kernel(o_ref):
    z = jnp.zeros(o_ref.shape, F32)
    o_ref[...] = pltpu.pack_elementwise([z, z], packed_dtype=BF16)


def _slot_rows_kernel(row0_ref, code_ref, pos_ref):
    code = code_ref[...]
    expert = lax.shift_right_logical(code, SLOT_CODE_BITS)
    first = jnp.zeros_like(code)
    for e in range(N_EXPERTS):
        first = jnp.where(expert == e, row0_ref[e], first)
    pos_ref[...] = first + (code & (SLOT_CODE - 1))


def _slot_rows(row0, codes):
    return pl.pallas_call(
        _slot_rows_kernel,
        grid_spec=pltpu.PrefetchScalarGridSpec(
            num_scalar_prefetch=1,
            grid=(1,),
            in_specs=[pl.BlockSpec(codes.shape, lambda i, r: (0, 0, 0))],
            out_specs=pl.BlockSpec(codes.shape, lambda i, r: (0, 0, 0)),
        ),
        out_shape=jax.ShapeDtypeStruct(codes.shape, codes.dtype),
        name="moe_slot_rows",
    )(row0, codes)


def _dispatch_kernel(t_max, p1_ref, p2_ref, padlo_ref, padn_ref, nt_ref, xn_ref, xs_ref,
                     zero_s, sem, zsem):
    step = pl.program_id(0)
    base = step * ROW_TILE

    def pad_copy(row):
        return pltpu.make_async_copy(_rec(zero_s, 0), _rec(xs_ref, row), zsem)

    def tile_copy(t):
        rows = EXP_TILE * REC_ROWS
        return pltpu.make_async_copy(zero_s, xs_ref.at[pl.ds(pl.multiple_of(t * rows, rows), rows)], zsem)

    def for_pad_rows(fn):
        def per_expert(e, c):
            lo = padlo_ref[e]
            lax.fori_loop(0, padn_ref[e], lambda r, cc: (fn(pad_copy(lo + r)), cc)[1], 0)
            return c
        lax.fori_loop(0, N_EXPERTS, per_expert, 0)
        lax.fori_loop(nt_ref[0], t_max, lambda t, c: (fn(tile_copy(t)), c)[1], 0)

    @pl.when(step == 0)
    def _():
        _zero_records_kernel(zero_s)
        for_pad_rows(lambda cp: cp.start(priority=1))

    def copies(r):
        src = _rec(xn_ref, r)
        return (pltpu.make_async_copy(src, _rec(xs_ref, p1_ref[base + r]), sem),
                pltpu.make_async_copy(src, _rec(xs_ref, p2_ref[base + r]), sem))

    def start(g, c):
        for u in range(DMA_UNROLL):
            for prio, cp in enumerate(copies(g * DMA_UNROLL + u)):
                cp.start(priority=prio)
        return c

    def wait(g, c):
        for u in range(DMA_UNROLL):
            for cp in copies(g * DMA_UNROLL + u):
                cp.wait()
        return c

    lax.fori_loop(0, ROW_TILE // DMA_UNROLL, start, 0)
    lax.fori_loop(0, ROW_TILE // DMA_UNROLL, wait, 0)

    @pl.when(step == 0)
    def _():
        for_pad_rows(lambda cp: cp.wait())


def _dispatch(pos1, pos2, pad_lo, pad_n, n_tiles, xn_rec, t_max):
    n = pos1.shape[0]
    return pl.pallas_call(
        functools.partial(_dispatch_kernel, t_max),
        grid_spec=pltpu.PrefetchScalarGridSpec(
            num_scalar_prefetch=5,
            grid=(n // ROW_TILE,),
            in_specs=[pl.BlockSpec((ROW_TILE * REC_ROWS, LANES), lambda i, *_: (i, 0))],
            out_specs=pl.BlockSpec(memory_space=pl.ANY),
            scratch_shapes=[pltpu.VMEM((EXP_TILE * REC_ROWS, LANES), jnp.uint32),
                            pltpu.SemaphoreType.DMA, pltpu.SemaphoreType.DMA],
        ),
        out_shape=jax.ShapeDtypeStruct((t_max * EXP_TILE * REC_ROWS, LANES), jnp.uint32),
        compiler_params=pltpu.CompilerParams(
            dimension_semantics=("arbitrary",), vmem_limit_bytes=VMEM_LIMIT),
        name="moe_dispatch",
    )(pos1, pos2, pad_lo, pad_n, n_tiles, xn_rec)


def _experts_kernel(te_ref, nt_ref, xs_ref, wg_ref, wu_ref, wd_ref, ys_ref, wg_s, wu_s, wd_s):
    t = pl.program_id(0)
    prev = te_ref[jnp.maximum(t - 1, 0)]

    @pl.when((t == 0) | (te_ref[t] != prev))
    def _():
        wg_s[...] = wg_ref[0].astype(BF16)
        wu_s[...] = wu_ref[0].astype(BF16)
        wd_s[...] = wd_ref[0].astype(BF16)

    @pl.when(t < nt_ref[0])
    def _():
        lo, hi = _unpack_rows(xs_ref, EXP_TILE)
        x = jnp.concatenate([lo.astype(BF16), hi.astype(BF16)], axis=1)
        hg = _dot(x, wg_s[...])
        hu = _dot(x, wu_s[...])
        act = hg * _sigmoid(hg) * hu
        _pack_rows(ys_ref, _dot(act.astype(BF16), wd_s[...]))

    @pl.when(t >= nt_ref[0])
    def _():
        _zero_records_kernel(ys_ref)


def _experts(tile_expert, n_tiles, xs, wg, wu, wd):
    t_max = tile_expert.shape[0]

    def tmap(t, te, nt):
        return (jnp.minimum(t, nt[0] - 1), 0)

    def wmap(t, te, nt):
        return (te[t], 0, 0)

    return pl.pallas_call(
        _experts_kernel,
        grid_spec=pltpu.PrefetchScalarGridSpec(
            num_scalar_prefetch=2,
            grid=(t_max,),
            in_specs=[pl.BlockSpec((EXP_TILE * REC_ROWS, LANES), tmap),
                      pl.BlockSpec((1, D_MODEL, D_EXPERT), wmap),
                      pl.BlockSpec((1, D_MODEL, D_EXPERT), wmap),
                      pl.BlockSpec((1, D_EXPERT, D_MODEL), wmap)],
            out_specs=pl.BlockSpec((EXP_TILE * REC_ROWS, LANES), lambda t, te, nt: (t, 0)),
            scratch_shapes=[pltpu.VMEM((D_MODEL, D_EXPERT), BF16), pltpu.VMEM((D_MODEL, D_EXPERT), BF16),
                            pltpu.VMEM((D_EXPERT, D_MODEL), BF16)],
        ),
        out_shape=jax.ShapeDtypeStruct(xs.shape, xs.dtype),
        compiler_params=pltpu.CompilerParams(
            dimension_semantics=("arbitrary",), vmem_limit_bytes=VMEM_LIMIT),
        name="moe_experts",
    )(tile_expert, n_tiles, xs, wg, wu, wd)


def _combine_kernel(n_prompt_tiles, p1_ref, p2_ref, x1_ref, route_ref, g_ref, ys_ref,
                    yp_ref, ysm_ref, a_s, b_s, sem):
    i = pl.program_id(0)
    n_steps = pl.num_programs(0)

    def copies(step, slot, r):
        tok = step * ROW_TILE + r
        return (pltpu.make_async_copy(_rec(ys_ref, p1_ref[tok]), _rec(a_s.at[slot], r), sem.at[slot]),
                pltpu.make_async_copy(_rec(ys_ref, p2_ref[tok]), _rec(b_s.at[slot], r), sem.at[slot]))

    def start_all(step, slot):
        def start(g, c):
            for u in range(DMA_UNROLL):
                for prio, cp in enumerate(copies(step, slot, g * DMA_UNROLL + u)):
                    cp.start(priority=prio)
            return c
        lax.fori_loop(0, ROW_TILE // DMA_UNROLL, start, 0)

    def wait_all(step, slot):
        def wait(g, c):
            for u in range(DMA_UNROLL):
                for cp in copies(step, slot, g * DMA_UNROLL + u):
                    cp.wait()
            return c
        lax.fori_loop(0, ROW_TILE // DMA_UNROLL, wait, 0)

    slot = i % 2

    @pl.when(i == 0)
    def _():
        start_all(0, 0)

    @pl.when(i + 1 < n_steps)
    def _():
        start_all(i + 1, 1 - slot)

    wait_all(i, slot)
    a_lo, a_hi = _unpack_rows(a_s.at[slot], ROW_TILE)
    b_lo, b_hi = _unpack_rows(b_s.at[slot], ROW_TILE)
    route = route_ref[...]
    w1 = route[:, 4:5]
    w2 = route[:, 5:6]
    moe = jnp.concatenate([w1 * a_lo + w2 * b_lo, w1 * a_hi + w2 * b_hi], axis=1)
    y = _rmsnorm(x1_ref[...] + moe, g_ref[...])

    @pl.when(i < n_prompt_tiles)
    def _():
        yp_ref[...] = y

    @pl.when(i >= n_prompt_tiles)
    def _():
        ysm_ref[...] = y


def _combine(pos1, pos2, x1, route, g, ys, n_p):
    n = x1.shape[0]
    npt = n_p // ROW_TILE

    def omap(i, *_):
        return (i, 0)

    return pl.pallas_call(
        functools.partial(_combine_kernel, npt),
        grid_spec=pltpu.PrefetchScalarGridSpec(
            num_scalar_prefetch=2,
            grid=(n // ROW_TILE,),
            in_specs=[pl.BlockSpec((ROW_TILE, D_MODEL), omap),
                      pl.BlockSpec((ROW_TILE, LANES), omap),
                      pl.BlockSpec((1, D_MODEL), lambda i, *_: (0, 0)),
                      pl.BlockSpec(memory_space=pl.ANY)],
            out_specs=[pl.BlockSpec((ROW_TILE, D_MODEL), lambda i, *_: (jnp.minimum(i, npt - 1), 0)),
                       pl.BlockSpec((ROW_TILE, D_MODEL), lambda i, *_: (jnp.maximum(i - npt, 0), 0))],
            scratch_shapes=[pltpu.VMEM((2, ROW_TILE * REC_ROWS, LANES), jnp.uint32),
                            pltpu.VMEM((2, ROW_TILE * REC_ROWS, LANES), jnp.uint32),
                            pltpu.SemaphoreType.DMA((2,))],
        ),
        out_shape=[jax.ShapeDtypeStruct((n_p, D_MODEL), F32),
                   jax.ShapeDtypeStruct((n - n_p, D_MODEL), F32)],
        compiler_params=pltpu.CompilerParams(
            dimension_semantics=("arbitrary",), vmem_limit_bytes=VMEM_LIMIT),
        name="moe_combine",
    )(pos1, pos2, x1, route, g, ys)


def _pad_lanes(w):
    return jnp.pad(w, ((0, 0), (0, LANES - w.shape[1])))


def _moe_plan(counts, n_tokens):
    cnt = counts[0, ROUTER_E0:ROUTER_E0 + N_EXPERTS].astype(jnp.int32)
    tiles = (cnt + EXP_TILE - 1) // EXP_TILE
    tile_end = jnp.cumsum(tiles)
    row0 = (tile_end - tiles) * EXP_TILE
    t_max = 2 * n_tokens // EXP_TILE + N_EXPERTS
    tile_ids = jnp.arange(t_max, dtype=jnp.int32)
    tile_expert = jnp.minimum(
        jnp.sum((tile_ids[:, None] >= tile_end[None, :]).astype(jnp.int32), axis=1), N_EXPERTS - 1)
    return row0, row0 + cnt, tiles * EXP_TILE - cnt, tile_expert, tile_end[-1:].astype(jnp.int32), t_max


def kernel(x_prompt, x_sample, state_gla_S, state_mlstm_C, state_mlstm_n, state_mlstm_m, cache_mlstm_conv, g_mix_norm, w_in, w_gla_gate_up, b_gla_gate_up, g_gla_out, w_mlstm_conv, b_mlstm_conv, b_mlstm_i, b_mlstm_f, g_mlstm_out, w_out, g_ffn_norm, w_router_group, w_router_expert, w_exp_gate, w_exp_up, w_exp_down, g_final):
    depth = w_in.shape[0]
    assert depth == 1
    bp, t_p, _ = x_prompt.shape
    bs, t_s, _ = x_sample.shape
    assert bp == 1
    xp = x_prompt.reshape(bp * t_p, D_MODEL)
    xs = x_sample.reshape(bs * t_s, D_MODEL)
    n_p = xp.shape[0]

    wt = jnp.transpose(w_in[0])
    c_gz = 3072
    c_mqk = c_gz + GLA_GATE_RANK
    c_mi = c_mqk + 3072
    w_big = _pack_inproj_weight(wt, c_gz, c_mqk)
    ws = _hilo_cols(_pad_lanes(jnp.concatenate([wt[c_gz:c_mqk], wt[c_mi:]], axis=0).T))
    wz = jnp.pad(w_gla_gate_up[0], ((0, LANES - GLA_GATE_RANK), (0, 0)))
    wz_hi, wz_lo = _split2(wz)
    bsm = _pad_lanes(jnp.concatenate(
        [jnp.zeros((1, GLA_GATE_RANK), F32), b_mlstm_i[0][None], b_mlstm_f[0][None]], axis=1))
    mix_w = (wz_hi, wz_lo, b_gla_gate_up[0][None], g_gla_out[0][None], w_mlstm_conv[0],
             b_mlstm_conv[0][None], bsm, g_mlstm_out[0][None])
    wr = _hilo_cols(_pad_lanes(jnp.concatenate([w_router_group[0], w_router_expert[0]], axis=1)))

    g_mix = g_mix_norm[0][None]
    p_big, p_small = _inproj(xs, g_mix, w_big, ws)

    dt = x_prompt.dtype
    z_s = jnp.zeros((bp, N_HEADS, DK, DV), dt)
    z_n = jnp.zeros((bp, N_HEADS, DK), dt)
    z_m = jnp.zeros((bp, 1, N_HEADS), dt)
    z_cv = jnp.zeros((bp, CONV_W - 1, QK_CH), dt)
    o_p, p_S, p_C, p_n, p_m, p_cv = _prompt_mixers(
        xp, g_mix, w_big, ws, MIX_ROWS, z_s, z_s, z_n, z_m, z_cv, mix_w)
    o_s, s_S, s_C, s_n, s_m, s_cv = _sample_mixers(
        p_big, p_small, bs, t_s, SAMPLE_PAR, state_gla_S[0], state_mlstm_C[0], state_mlstm_n[0],
        state_mlstm_m[0][:, None, :], cache_mlstm_conv[0], mix_w)

    x1, xn_rec, route, codes, counts = _outproj(
        o_p, o_s, xp, xs, w_out[0].astype(BF16), g_ffn_norm[0][None], wr)
    row0, pad_lo, pad_n, tile_expert, n_tiles, t_max = _moe_plan(counts, x1.shape[0])
    pos = _slot_rows(row0, codes)
    pos1 = pos[:, 0, :].reshape(-1)
    pos2 = pos[:, 1, :].reshape(-1)
    xs_rec = _dispatch(pos1, pos2, pad_lo, pad_n, n_tiles, xn_rec, t_max)
    ys_rec = _experts(tile_expert, n_tiles, xs_rec, w_exp_gate[0], w_exp_up[0], w_exp_down[0])
    y_p, y_s = _combine(pos1, pos2, x1, route, g_final[None], ys_rec, n_p)

    return (y_p.reshape(x_prompt.shape), y_s.reshape(x_sample.shape),
            p_S[None], p_C[None], p_n[None], p_m[:, 0, :N_HEADS][None], p_cv[None],
            s_S[None], s_C[None], s_n[None], s_m[:, 0, :N_HEADS][None], s_cv[None])
```

```python
import functools

import jax
import jax.numpy as jnp
from jax import lax
from jax.experimental import pallas as pl
from jax.experimental.pallas import tpu as pltpu

F32 = jnp.float32
BF16 = jnp.bfloat16

D_MODEL = 2048
N_HEADS = 4
DK = 128
DV = 256
GLA_GATE_RANK = 16
GLA_GATE_NORM = 16.0
CONV_W = 4
QK_CH = 2 * N_HEADS * DK
N_GROUPS = 4
EXPERTS_PER_GROUP = 8
N_EXPERTS = N_GROUPS * EXPERTS_PER_GROUP
D_EXPERT = 256
EPS = 1e-6

LANES = 128
GLA_SUB = 16
GLA_SAFE_DECAY = 60.0
MIX_ROWS = 256
SAMPLE_PAR = 8
N_MIX_OUT = 6
ROW_TILE = 512
PROJ_BIG = 6 * 1024
PROJ_CHUNK = 1024
PACK_COLS = 256
LANE_I = GLA_GATE_RANK
LANE_F = GLA_GATE_RANK + N_HEADS
ROUTER_E0 = N_GROUPS
HALF = D_MODEL // 2
REC_ROWS = HALF // LANES
EXP_TILE = 512
DMA_UNROLL = 8
SLOT_CODE_BITS = 16
SLOT_CODE = 1 << SLOT_CODE_BITS

VMEM_LIMIT = 56 * 1024 * 1024
VMEM_LIMIT_FUSED = 62 * 1024 * 1024


def _dot(a, b):
    return jnp.dot(a, b, preferred_element_type=F32)


def _dot_nt(a, b):
    return lax.dot_general(a, b, (((1,), (1,)), ((), ())), preferred_element_type=F32)


def _dot_tn(a, b):
    return lax.dot_general(a, b, (((0,), (0,)), ((), ())), preferred_element_type=F32)


def _split2(x):
    hi = x.astype(BF16)
    lo = (x - hi.astype(F32)).astype(BF16)
    return hi, lo


def _split3(x):
    hi = x.astype(BF16)
    r = x - hi.astype(F32)
    mid = r.astype(BF16)
    lo = (r - mid.astype(F32)).astype(BF16)
    return hi, mid, lo


def _dot_exact_lhs(m_bf16, x):
    hi, mid, lo = _split3(x)
    return _dot(m_bf16, hi) + _dot(m_bf16, mid) + _dot(m_bf16, lo)


def _dot_hilo(a_hi, a_lo, b_hi, b_lo):
    return _dot(a_hi, b_hi) + _dot(a_lo, b_hi) + _dot(a_hi, b_lo)


def _hilo_cols(w):
    return jnp.concatenate(_split2(w), axis=1)


def _dot_hilo_cols(a_hi, a_lo, b_ref):
    p = _dot(a_hi, b_ref[...])
    return (p[:, LANES:] + _dot(a_lo, b_ref[:, :LANES])) + p[:, :LANES]


def _log_sigmoid(z):
    return jnp.minimum(z, 0.0) - jnp.log1p(jnp.exp(-jnp.abs(z)))


def _sigmoid(z):
    return 0.5 * jnp.tanh(0.5 * z) + 0.5


def _rmsnorm(x, g):
    return x * lax.rsqrt(jnp.mean(x * x, axis=-1, keepdims=True) + EPS) * g


def _pack_inproj_weight_kernel(c_gz, c_mqk, wt_ref, o_ref, buf, sem):
    j = pl.program_id(0)
    n_j = pl.num_programs(0)

    def copy(jj):
        col0 = jj * PACK_COLS
        row = pl.multiple_of(col0 + jnp.where(col0 >= c_gz, c_mqk - c_gz, 0), 16)
        return pltpu.make_async_copy(wt_ref.at[pl.ds(row, PACK_COLS)], buf.at[jj % 2], sem.at[jj % 2])

    @pl.when(j == 0)
    def _():
        copy(0).start()

    @pl.when(j + 1 < n_j)
    def _():
        copy(j + 1).start()

    copy(j).wait()
    r_i = lax.broadcasted_iota(jnp.int32, (PACK_COLS, PACK_COLS), 0)
    c_i = lax.broadcasted_iota(jnp.int32, (PACK_COLS, PACK_COLS), 1)
    eye = jnp.where(r_i == c_i, 1.0, 0.0).astype(BF16)
    o_ref[...] = _dot_tn(buf[j % 2].astype(BF16), eye).astype(o_ref.dtype)


def _pack_inproj_weight(wt, c_gz, c_mqk):
    assert c_gz % PACK_COLS == 0 and (c_mqk - c_gz) % 16 == 0
    return pl.pallas_call(
        functools.partial(_pack_inproj_weight_kernel, c_gz, c_mqk),
        grid=(PROJ_BIG // PACK_COLS,),
        in_specs=[pl.BlockSpec(memory_space=pl.ANY)],
        out_specs=pl.BlockSpec((wt.shape[1], PACK_COLS), lambda j: (0, j)),
        out_shape=jax.ShapeDtypeStruct((wt.shape[1], PROJ_BIG), BF16),
        scratch_shapes=[pltpu.VMEM((2, PACK_COLS, wt.shape[1]), F32), pltpu.SemaphoreType.DMA((2,))],
        compiler_params=pltpu.CompilerParams(dimension_semantics=("arbitrary",), vmem_limit_bytes=VMEM_LIMIT),
        name="pack_inproj_weight",
    )(wt)


def _project(x_ref, g_ref, w_ref, ws_ref, p_ref, ps_ref, h_ref, chunks, with_norm):
    if with_norm:
        y_hi, y_lo = _split2(_rmsnorm(x_ref[...], g_ref[...]))
        h_ref[...] = y_hi
        ps_ref[...] = _dot_hilo_cols(y_hi, y_lo, ws_ref)
    else:
        y_hi = h_ref[...]
    for c in chunks:
        cols = slice(c * PROJ_CHUNK, (c + 1) * PROJ_CHUNK)
        p_ref[:, cols] = _dot(y_hi, w_ref[:, cols]).astype(p_ref.dtype)


PROJ_CHUNKS = tuple(range(PROJ_BIG // PROJ_CHUNK))


def _inproj_kernel(x_ref, g_ref, w_ref, ws_ref, p_ref, ps_ref, h_s):
    _project(x_ref, g_ref, w_ref, ws_ref, p_ref, ps_ref, h_s, PROJ_CHUNKS, True)


def _inproj(x, g, w_big, ws):
    n = x.shape[0]
    assert n % ROW_TILE == 0
    once = pl.Buffered(1)
    return pl.pallas_call(
        _inproj_kernel,
        grid=(n // ROW_TILE,),
        in_specs=[
            pl.BlockSpec((ROW_TILE, D_MODEL), lambda i: (i, 0)),
            pl.BlockSpec((1, D_MODEL), lambda i: (0, 0)),
            pl.BlockSpec((D_MODEL, PROJ_BIG), lambda i: (0, 0), pipeline_mode=once),
            pl.BlockSpec((D_MODEL, 2 * LANES), lambda i: (0, 0), pipeline_mode=once),
        ],
        out_specs=[
            pl.BlockSpec((ROW_TILE, PROJ_BIG), lambda i: (i, 0)),
            pl.BlockSpec((ROW_TILE, LANES), lambda i: (i, 0)),
        ],
        out_shape=[
            jax.ShapeDtypeStruct((n, PROJ_BIG), BF16),
            jax.ShapeDtypeStruct((n, LANES), F32),
        ],
        scratch_shapes=[pltpu.VMEM((ROW_TILE, D_MODEL), BF16)],
        compiler_params=pltpu.CompilerParams(
            dimension_semantics=("arbitrary",), vmem_limit_bytes=VMEM_LIMIT),
        name="inproj",
    )(x, g, w_big, ws)


def _run_streams(streams):
    stages = [next(s) for s in streams]
    ok = stages[0][0]
    for st in stages[1:]:
        ok = jnp.logical_and(ok, st[0])

    @pl.when(ok)
    def _():
        for st in stages:
            st[1]()

    @pl.when(jnp.logical_not(ok))
    def _():
        for st in stages:
            st[2]()

    for s in streams:
        for _ in s:
            pass


def _mixer_kernel(rows, side_work, *refs):
    _run_streams([_mixer_stream(rows, False, side_work, *refs)])


def _mixer_stream(rows, single, side_work, gq_ref, gk_ref, gv_ref, gg_ref, mqk_ref, mv_ref, mo_ref, sm_ref,
                  s0_ref, c0_ref, n0_ref, m0_ref, cv0_ref,
                  wz_hi_ref, wz_lo_ref, bz_ref, g_gla_ref, wc_ref, bc_ref, bsm_ref, g_ml_ref,
                  o_ref, s_out_ref, c_out_ref, n_out_ref, m_out_ref, cv_out_ref,
                  st_s, c_s, n_s, m_s, cb_s, b_s):
    blk = pl.program_id(1)
    n_blk = pl.num_programs(1)

    def maybe(cond, fn):
        if single:
            fn()
        else:
            pl.when(cond)(fn)

    def init():
        for h in range(N_HEADS):
            st_s[h] = s0_ref[0, h].T
            c_s[h] = c0_ref[0, h]
            n_s[h] = n0_ref[0, h:h + 1, :]
            m_s[h] = jnp.broadcast_to(m0_ref[0, :, h:h + 1], (1, LANES))
        cb_s[0:8, :] = jnp.zeros((8, QK_CH), F32)
        cb_s[8 - (CONV_W - 1):8, :] = cv0_ref[0]

    maybe(blk == 0, init)

    small = sm_ref[...]
    row_i = lax.broadcasted_iota(jnp.int32, (rows, rows), 0)
    col_i = lax.broadcasted_iota(jnp.int32, (rows, rows), 1)
    causal = col_i <= row_i

    sm_hi, sm_lo = _split2(small)
    z = _dot_hilo(sm_hi, sm_lo, wz_hi_ref[...], wz_lo_ref[...]) + bz_ref[...]
    log_a = _log_sigmoid(z) * (1.0 / GLA_GATE_NORM)
    tri = jnp.where(causal, 1.0, 0.0).astype(BF16)
    b_blk = _dot_exact_lhs(tri, log_a)
    b_last = b_blk[rows - 1:rows, :]
    factorable = jnp.min(b_last) >= -GLA_SAFE_DECAY
    side_work[0]()

    def gla_finish(o, vc, gate):
        y = o * lax.rsqrt(jnp.mean(o * o, axis=-1, keepdims=True) + EPS) * g_gla_ref[:, vc]
        return (y * (gate * _sigmoid(gate))).astype(o_ref.dtype)

    def gla_fast():
        for h in range(N_HEADS):
            kc = slice(h * DK, (h + 1) * DK)
            vc = slice(h * DV, (h + 1) * DV)
            bh = b_blk[:, kc]
            qh = gq_ref[:, kc].astype(F32) * (DK ** -0.5)
            kh = gk_ref[:, kc].astype(F32)
            vh = gv_ref[:, vc]
            st = st_s[h]
            q_dec = (qh * jnp.exp(bh)).astype(BF16)
            k_inv = (kh * jnp.exp(-bh)).astype(BF16)
            a = jnp.where(causal, _dot_nt(q_dec, k_inv), 0.0)
            o = _dot(a.astype(BF16), vh) + _dot_nt(q_dec, st.astype(BF16))
            bh_end = b_last[:, kc]
            k_dec = (kh * jnp.exp(bh_end - bh)).astype(BF16)
            st_s[h] = st * jnp.exp(bh_end) + _dot_tn(vh, k_dec)
            o_ref[:, vc] = gla_finish(o, vc, gg_ref[:, vc].astype(F32))

    sub_r = lax.broadcasted_iota(jnp.int32, (GLA_SUB, LANES), 0)
    sub_l = lax.broadcasted_iota(jnp.int32, (GLA_SUB, LANES), 1)

    def gla_sub(c, carry):
        r0 = pl.multiple_of(c * GLA_SUB, GLA_SUB)
        rs = pl.ds(r0, GLA_SUB)
        for h in range(N_HEADS):
            kc = slice(h * DK, (h + 1) * DK)
            vc = slice(h * DV, (h + 1) * DV)
            bh = b_s[rs, kc]
            qh = gq_ref[rs, kc].astype(F32) * (DK ** -0.5)
            kh = gk_ref[rs, kc].astype(F32)
            vh = gv_ref[rs, vc]
            st = st_s[h]
            o = _dot_nt((qh * jnp.exp(bh)).astype(BF16), st.astype(BF16))
            a = jnp.zeros((GLA_SUB, LANES), F32)
            for s in range(GLA_SUB):
                e = jnp.exp(jnp.minimum(bh - bh[s:s + 1, :], 0.0))
                col = jnp.sum(qh * (kh[s:s + 1, :] * e), axis=1, keepdims=True)
                a = jnp.where((sub_l == s) & (sub_r >= s), col, a)
            o = o + _dot(a[:, :GLA_SUB].astype(BF16), vh)
            b_end = bh[GLA_SUB - 1:GLA_SUB, :]
            k_dec = kh * jnp.exp(b_end - bh)
            st_s[h] = st * jnp.exp(b_end) + _dot_tn(vh, k_dec.astype(BF16))
            o_ref[rs, vc] = gla_finish(o, vc, gg_ref[rs, vc].astype(F32))
        return carry

    def gla_slow():
        same_sub = (row_i // GLA_SUB) == (col_i // GLA_SUB)
        blk_tri = jnp.where(causal & same_sub, 1.0, 0.0).astype(BF16)
        b_s[...] = _dot_exact_lhs(blk_tri, log_a)
        lax.fori_loop(0, rows // GLA_SUB, gla_sub, 0)

    yield factorable, gla_fast, gla_slow

    cb_s[8:8 + rows, :] = mqk_ref[...].astype(F32)
    conv = bc_ref[...]
    for j in range(CONV_W):
        conv = conv + cb_s[8 - (CONV_W - 1) + j:8 - (CONV_W - 1) + j + rows, :] * wc_ref[j:j + 1, :]
    cb_s[0:8, :] = cb_s[rows:rows + 8, :]
    qk = conv * _sigmoid(conv)

    pre = small + bsm_ref[...]
    log_f = _log_sigmoid(pre)
    f_cum =_dot_exact_lhs(tri, log_f)
    eye = jnp.where(row_i == col_i, 1.0, 0.0).astype(BF16)
    f_cum_t = sum(_dot_tn(p, eye) for p in _split3(f_cum))
    pre_t = sum(_dot_tn(p, eye) for p in _split3(pre))
    side_work[1]()
    lane_1 = lax.broadcasted_iota(jnp.int32, (1, LANES), 1)
    m_new = jnp.zeros((1, LANES), F32)

    for h in range(N_HEADS):
        kc = slice(h * DK, (h + 1) * DK)
        vc = slice(h * DV, (h + 1) * DV)
        f_col = f_cum[:, LANE_F + h:LANE_F + h + 1]
        i_col = pre[:, LANE_I + h:LANE_I + h + 1]
        f_row = f_cum_t[LANE_F + h:LANE_F + h + 1, :]
        i_row = pre_t[LANE_I + h:LANE_I + h + 1, :]
        dm = jnp.where(causal, f_col - f_row + i_row, -jnp.inf)
        m_prev = m_s[h][:, 0:1]
        inter = f_col + m_prev
        m_t = jnp.maximum(inter, jnp.max(dm, axis=1, keepdims=True))
        w = jnp.exp(dm - m_t)
        w_inter = jnp.exp(inter - m_t)
        q = qk[:, kc]
        k = qk[:, N_HEADS * DK + h * DK:N_HEADS * DK + (h + 1) * DK] * (DK ** -0.5)
        v = mv_ref[:, vc]
        q_b = q.astype(BF16)
        c_prev = c_s[h]
        n_prev = n_s[h]
        s_qk = _dot_nt(q_b, k.astype(BF16)) * w
        num = _dot(s_qk.astype(BF16), v) + w_inter * _dot(q_b, c_prev.astype(BF16))
        den = (jnp.sum(s_qk, axis=1, keepdims=True)
               + w_inter * jnp.sum(q * n_prev, axis=1, keepdims=True))
        hh = num / jnp.maximum(jnp.abs(den), jnp.exp(-m_t))
        m_end = m_t[rows - 1:rows, :]
        w_s = jnp.exp(f_col[rows - 1:rows, :] - f_col + i_col - m_end)
        dec = w_inter[rows - 1:rows, :]
        k_w = k * w_s
        c_s[h] = dec * c_prev + _dot_tn(k_w.astype(BF16), v)
        side_work[2 + h]()
        n_s[h] = dec * n_prev + jnp.sum(k_w, axis=0, keepdims=True)
        m_s[h] = jnp.broadcast_to(m_end, (1, LANES))
        m_new = jnp.where(lane_1 == h, m_end, m_new)
        y = hh * lax.rsqrt(jnp.mean(hh * hh, axis=-1, keepdims=True) + EPS) * g_ml_ref[:, vc]
        o_ref[:, N_HEADS * DV + h * DV:N_HEADS * DV + (h + 1) * DV] = (
            y * _sigmoid(mo_ref[:, vc].astype(F32))).astype(o_ref.dtype)

    def final():
        for h in range(N_HEADS):
            s_out_ref[0, h] = st_s[h].T
            c_out_ref[0, h] = c_s[h]
            n_out_ref[0, h:h + 1, :] = n_s[h]
        m_out_ref[0] = m_new
        cv_out_ref[0] = cb_s[8 - (CONV_W - 1):8, :]

    maybe(blk == n_blk - 1, final)


def _sample_kernel(rows, n_par, *refs):
    n_row_in, n_state_in, n_w = 8, 5, 8
    row_in, state_in = refs[:n_row_in], refs[n_row_in:n_row_in + n_state_in]
    weights = refs[n_row_in + n_state_in:n_row_in + n_state_in + n_w]
    outs = refs[n_row_in + n_state_in + n_w:]
    o_ref, state_out, scratch = outs[0], outs[1:N_MIX_OUT], outs[N_MIX_OUT:]
    no_side = (lambda: None,) * (2 + N_HEADS)
    streams = []
    for g in range(n_par):
        rs = pl.ds(g * rows, rows)
        streams.append(_mixer_stream(
            rows, True, no_side,
            *[r.at[rs] for r in row_in], *[r.at[pl.ds(g, 1)] for r in state_in], *weights,
            o_ref.at[rs], *[r.at[pl.ds(g, 1)] for r in state_out], *[s.at[g] for s in scratch]))
    _run_streams(streams)


def _sample_mixers(p_big, p_small, n_streams, rows, n_par, s0, c0, n0, m0, cv0, wts):
    assert n_streams % n_par == 0 and rows % GLA_SUB == 0
    blk_rows = n_par * rows

    def rmap(col):
        return lambda s, b: (s, col)

    def smap(*zeros):
        return lambda s, b: (s,) + zeros

    def wmap(s, b):
        return (0, 0)

    in_specs = [
        pl.BlockSpec((blk_rows, N_HEADS * DK), rmap(0)),
        pl.BlockSpec((blk_rows, N_HEADS * DK), rmap(1)),
        pl.BlockSpec((blk_rows, N_HEADS * DV), rmap(1)),
        pl.BlockSpec((blk_rows, N_HEADS * DV), rmap(2)),
        pl.BlockSpec((blk_rows, QK_CH), rmap(3)),
        pl.BlockSpec((blk_rows, N_HEADS * DV), rmap(4)),
        pl.BlockSpec((blk_rows, N_HEADS * DV), rmap(5)),
        pl.BlockSpec((blk_rows, LANES), rmap(0)),
        pl.BlockSpec((n_par, N_HEADS, DK, DV), smap(0, 0, 0)),
        pl.BlockSpec((n_par, N_HEADS, DK, DV), smap(0, 0, 0)),
        pl.BlockSpec((n_par, N_HEADS, DK), smap(0, 0)),
        pl.BlockSpec((n_par, 1, N_HEADS), smap(0, 0)),
        pl.BlockSpec((n_par, CONV_W - 1, QK_CH), smap(0, 0)),
    ] + [pl.BlockSpec(w.shape, wmap) for w in wts]
    out_shape = [
        jax.ShapeDtypeStruct((n_streams * rows, D_MODEL), BF16),
        jax.ShapeDtypeStruct((n_streams, N_HEADS, DK, DV), F32),
        jax.ShapeDtypeStruct((n_streams, N_HEADS, DK, DV), F32),
        jax.ShapeDtypeStruct((n_streams, N_HEADS, DK), F32),
        jax.ShapeDtypeStruct((n_streams, 1, LANES), F32),
        jax.ShapeDtypeStruct((n_streams, CONV_W - 1, QK_CH), F32),
    ]
    out_specs = [
        pl.BlockSpec((blk_rows, D_MODEL), smap(0)),
        pl.BlockSpec((n_par, N_HEADS, DK, DV), smap(0, 0, 0)),
        pl.BlockSpec((n_par, N_HEADS, DK, DV), smap(0, 0, 0)),
        pl.BlockSpec((n_par, N_HEADS, DK), smap(0, 0)),
        pl.BlockSpec((n_par, 1, LANES), smap(0, 0)),
        pl.BlockSpec((n_par, CONV_W - 1, QK_CH), smap(0, 0)),
    ]
    scratch = [
        pltpu.VMEM((n_par, N_HEADS, DV, DK), F32),
        pltpu.VMEM((n_par, N_HEADS, DK, DV), F32),
        pltpu.VMEM((n_par, N_HEADS, 1, DK), F32),
        pltpu.VMEM((n_par, N_HEADS, 1, LANES), F32),
        pltpu.VMEM((n_par, rows + 8, QK_CH), F32),
        pltpu.VMEM((n_par, rows, N_HEADS * DK), F32),
    ]
    return pl.pallas_call(
        functools.partial(_sample_kernel, rows, n_par),
        grid=(n_streams // n_par, 1),
        in_specs=in_specs, out_specs=out_specs, out_shape=out_shape,
        scratch_shapes=scratch,
        compiler_params=pltpu.CompilerParams(
            dimension_semantics=("arbitrary", "arbitrary"), vmem_limit_bytes=VMEM_LIMIT),
        name=f"mixers_r{rows}x{n_par}",
    )(p_big, p_big, p_big, p_big, p_big, p_big, p_big, p_small, s0, c0, n0, m0, cv0, *wts)


def _prompt_kernel(rows, x0_ref, xn_ref, g_ref, w_ref, ws_ref, *refs):
    mixer_refs, (p_buf, ps_buf, h_s) = refs[:-3], refs[-3:]
    blk = pl.program_id(1)
    cur = blk % 2

    def project(x_ref, slot, chunks, with_norm):
        _project(x_ref, g_ref, w_ref, ws_ref, p_buf.at[slot], ps_buf.at[slot], h_s, chunks, with_norm)

    @pl.when(blk == 0)
    def _():
        project(x0_ref, 0, PROJ_CHUNKS, True)

    p_cur = p_buf.at[cur]
    c0 = N_HEADS * DK
    views = [p_cur.at[:, lo:hi] for lo, hi in
             ((0, c0), (c0, 2 * c0), (1024, 2048), (2048, 3072), (3072, 4096), (4096, 5120), (5120, 6144))]
    side = ((lambda: project(xn_ref, 1 - cur, PROJ_CHUNKS[:2], True),)
            + tuple(functools.partial(project, xn_ref, 1 - cur, (c,), False) for c in PROJ_CHUNKS[2:])
            + (lambda: None,))
    assert len(side) == 2 + N_HEADS
    _mixer_kernel(rows, side, *views, ps_buf.at[cur], *mixer_refs)


def _prompt_mixers(xp, g, w_big, ws, rows, s0, c0, n0, m0, cv0, wts):
    t_len = xp.shape[0]
    assert t_len % rows == 0
    n_blk = t_len // rows
    once = pl.Buffered(1)

    def cmap(*zeros):
        return lambda s, b: zeros

    in_specs = [
        pl.BlockSpec((rows, D_MODEL), cmap(0, 0), pipeline_mode=once),
        pl.BlockSpec((rows, D_MODEL), lambda s, b: (jnp.minimum(b + 1, n_blk - 1), 0)),
        pl.BlockSpec((1, D_MODEL), cmap(0, 0)),
        pl.BlockSpec((D_MODEL, PROJ_BIG), cmap(0, 0), pipeline_mode=once),
        pl.BlockSpec((D_MODEL, 2 * LANES), cmap(0, 0), pipeline_mode=once),
        pl.BlockSpec((1, N_HEADS, DK, DV), cmap(0, 0, 0, 0)),
        pl.BlockSpec((1, N_HEADS, DK, DV), cmap(0, 0, 0, 0)),
        pl.BlockSpec((1, N_HEADS, DK), cmap(0, 0, 0)),
        pl.BlockSpec((1, 1, N_HEADS), cmap(0, 0, 0)),
        pl.BlockSpec((1, CONV_W - 1, QK_CH), cmap(0, 0, 0)),
    ] + [pl.BlockSpec(w.shape, cmap(0, 0)) for w in wts]
    out_shape = [
        jax.ShapeDtypeStruct((t_len, D_MODEL), BF16),
        jax.ShapeDtypeStruct((1, N_HEADS, DK, DV), F32),
        jax.ShapeDtypeStruct((1, N_HEADS, DK, DV), F32),
        jax.ShapeDtypeStruct((1, N_HEADS, DK), F32),
        jax.ShapeDtypeStruct((1, 1, LANES), F32),
        jax.ShapeDtypeStruct((1, CONV_W - 1, QK_CH), F32),
    ]
    out_specs = [
        pl.BlockSpec((rows, D_MODEL), lambda s, b: (b, 0)),
        pl.BlockSpec((1, N_HEADS, DK, DV), cmap(0, 0, 0, 0)),
        pl.BlockSpec((1, N_HEADS, DK, DV), cmap(0, 0, 0, 0)),
        pl.BlockSpec((1, N_HEADS, DK), cmap(0, 0, 0)),
        pl.BlockSpec((1, 1, LANES), cmap(0, 0, 0)),
        pl.BlockSpec((1, CONV_W - 1, QK_CH), cmap(0, 0, 0)),
    ]
    scratch = [
        pltpu.VMEM((N_HEADS, DV, DK), F32),
        pltpu.VMEM((N_HEADS, DK, DV), F32),
        pltpu.VMEM((N_HEADS, 1, DK), F32),
        pltpu.VMEM((N_HEADS, 1, LANES), F32),
        pltpu.VMEM((rows + 8, QK_CH), F32),
        pltpu.VMEM((rows, N_HEADS * DK), F32),
        pltpu.VMEM((2, rows, PROJ_BIG), BF16),
        pltpu.VMEM((2, rows, LANES), F32),
        pltpu.VMEM((rows, D_MODEL), BF16),
    ]
    return pl.pallas_call(
        functools.partial(_prompt_kernel, rows),
        grid=(1, n_blk),
        in_specs=in_specs, out_specs=out_specs, out_shape=out_shape,
        scratch_shapes=scratch,
        compiler_params=pltpu.CompilerParams(
            dimension_semantics=("arbitrary", "arbitrary"), vmem_limit_bytes=VMEM_LIMIT_FUSED),
        name="prompt_inproj_mixers",
    )(xp, xp, g, w_big, ws, s0, c0, n0, m0, cv0, *wts)


def _pack_rows(ref, val):
    word = pltpu.pack_elementwise([val[:, :HALF], val[:, HALF:]], packed_dtype=BF16)
    rows = val.shape[0]
    for j in range(REC_ROWS):
        ref[pl.ds(j, rows, stride=REC_ROWS), :] = word[:, j * LANES:(j + 1) * LANES]


def _unpack_rows(ref, rows):
    word = jnp.concatenate(
        [ref[pl.ds(j, rows, stride=REC_ROWS), :] for j in range(REC_ROWS)], axis=1)
    lo = pltpu.unpack_elementwise(word, index=0, packed_dtype=BF16, unpacked_dtype=F32)
    hi = pltpu.unpack_elementwise(word, index=1, packed_dtype=BF16, unpacked_dtype=F32)
    return lo, hi


def _outproj_kernel(n_prompt_tiles, op_ref, os_ref, xp_ref, xs_ref, w_ref, g_ref, wr_ref,
                    x1_ref, xn_ref, route_ref, code_ref, cnt_ref, cnt_s):
    i = pl.program_id(0)

    @pl.when(i == 0)
    def _():
        cnt_s[...] = jnp.zeros_like(cnt_s)

    def body(o_ref, x_ref):
        x1 = x_ref[...] + _dot(o_ref[...], w_ref[...])
        x1_ref[...] = x1
        xn = _rmsnorm(x1, g_ref[...])
        _pack_rows(xn_ref, xn)
        xn_hi, xn_lo = _split2(xn)
        lg = _dot_hilo_cols(xn_hi, xn_lo, wr_ref)
        lane = lax.broadcasted_iota(jnp.int32, lg.shape, 1).astype(F32)
        neg = -jnp.inf
        lgm = jnp.where(lane < N_GROUPS, lg, neg)
        mg = jnp.max(lgm, axis=1, keepdims=True)
        g_idx = jnp.min(jnp.where(lgm == mg, lane, float(LANES)), axis=1, keepdims=True)
        g_val = 1.0 / jnp.sum(jnp.where(lane < N_GROUPS, jnp.exp(lg - mg), 0.0), axis=1, keepdims=True)
        e0 = ROUTER_E0 + EXPERTS_PER_GROUP * g_idx
        le = jnp.where((lane >= e0) & (lane < e0 + EXPERTS_PER_GROUP), lg, neg)
        v1 = jnp.max(le, axis=1, keepdims=True)
        i1 = jnp.min(jnp.where(le == v1, lane, float(LANES)), axis=1, keepdims=True)
        le2 = jnp.where(lane == i1, neg, le)
        v2 = jnp.max(le2, axis=1, keepdims=True)
        i2 = jnp.min(jnp.where(le2 == v2, lane, float(LANES)), axis=1, keepdims=True)
        t = jnp.exp(v2 - v1)
        w1 = g_val / (1.0 + t)
        w2 = g_val * t / (1.0 + t)
        oh1 = lane == i1
        oh2 = lane == i2
        hot = jnp.where(oh1 | oh2, 1.0, 0.0)
        r_i = lax.broadcasted_iota(jnp.int32, (ROW_TILE, ROW_TILE), 0)
        c_i = lax.broadcasted_iota(jnp.int32, (ROW_TILE, ROW_TILE), 1)
        before = jnp.where(c_i < r_i, 1.0, 0.0).astype(BF16)
        seen = _dot(before, hot.astype(BF16)) + cnt_s[...]
        rank1 = jnp.sum(jnp.where(oh1, seen, 0.0), axis=1, keepdims=True)
        rank2 = jnp.sum(jnp.where(oh2, seen, 0.0), axis=1, keepdims=True)
        cnt_s[...] += jnp.sum(hot, axis=0, keepdims=True)
        code1 = (i1 - ROUTER_E0) * float(SLOT_CODE) + rank1
        code2 = (i2 - ROUTER_E0) * float(SLOT_CODE) + rank2
        route = jnp.zeros_like(lg)
        for k, col in ((0, code1), (1, code2), (4, w1), (5, w2)):
            route = jnp.where(lane == k, col, route)
        route_ref[...] = route
        code_ref[0] = route.T[0:8, :].astype(jnp.int32)

    @pl.when(i < n_prompt_tiles)
    def _():
        body(op_ref, xp_ref)

    @pl.when(i >= n_prompt_tiles)
    def _():
        body(os_ref, xs_ref)

    cnt_ref[...] = cnt_s[...]


def _outproj(o_p, o_s, xp, xs, w_out, g, wr):
    n_p, n_s = xp.shape[0], xs.shape[0]
    npt = n_p // ROW_TILE
    n = n_p + n_s

    def pmap(i):
        return (jnp.minimum(i, npt - 1), 0)

    def cmap(i):
        return (0, 0)

    def omap(i):
        return (i, 0)

    return pl.pallas_call(
        functools.partial(_outproj_kernel, npt),
        grid=(npt + 1,),
        in_specs=[
            pl.BlockSpec((ROW_TILE, D_MODEL), pmap), pl.BlockSpec((ROW_TILE, D_MODEL), cmap),
            pl.BlockSpec((ROW_TILE, D_MODEL), pmap), pl.BlockSpec((ROW_TILE, D_MODEL), cmap),
            pl.BlockSpec((D_MODEL, D_MODEL), cmap), pl.BlockSpec((1, D_MODEL), cmap),
            pl.BlockSpec((D_MODEL, 2 * LANES), cmap),
        ],
        out_specs=[
            pl.BlockSpec((ROW_TILE, D_MODEL), omap), pl.BlockSpec((ROW_TILE * REC_ROWS, LANES), omap),
            pl.BlockSpec((ROW_TILE, LANES), omap), pl.BlockSpec((1, 8, ROW_TILE), lambda i: (i, 0, 0)),
            pl.BlockSpec((1, LANES), cmap),
        ],
        out_shape=[
            jax.ShapeDtypeStruct((n, D_MODEL), F32),
            jax.ShapeDtypeStruct((n * REC_ROWS, LANES), jnp.uint32),
            jax.ShapeDtypeStruct((n, LANES), F32),
            jax.ShapeDtypeStruct((npt + 1, 8, ROW_TILE), jnp.int32),
            jax.ShapeDtypeStruct((1, LANES), F32),
        ],
        scratch_shapes=[pltpu.VMEM((1, LANES), F32)],
        compiler_params=pltpu.CompilerParams(
            dimension_semantics=("arbitrary",), vmem_limit_bytes=VMEM_LIMIT),
        name="outproj_router",
    )(o_p, o_s, xp, xs, w_out, g, wr)


def _rec(ref, idx):
    return ref.at[pl.ds(pl.multiple_of(idx * REC_ROWS, REC_ROWS), REC_ROWS)]


def _zero_records_kernel(o_ref):
    z = jnp.zeros(o_ref.shape, F32)
    o_ref[...] = pltpu.pack_elementwise([z, z], packed_dtype=BF16)


def _slot_rows_kernel(row0_ref, code_ref, pos_ref):
    code = code_ref[...]
    expert = lax.shift_right_logical(code, SLOT_CODE_BITS)
    first = jnp.zeros_like(code)
    for e in range(N_EXPERTS):
        first = jnp.where(expert == e, row0_ref[e], first)
    pos_ref[...] = first + (code & (SLOT_CODE - 1))


def _slot_rows(row0, codes):
    return pl.pallas_call(
        _slot_rows_kernel,
        grid_spec=pltpu.PrefetchScalarGridSpec(
            num_scalar_prefetch=1,
            grid=(1,),
            in_specs=[pl.BlockSpec(codes.shape, lambda i, r: (0, 0, 0))],
            out_specs=pl.BlockSpec(codes.shape, lambda i, r: (0, 0, 0)),
        ),
        out_shape=jax.ShapeDtypeStruct(codes.shape, codes.dtype),
        name="moe_slot_rows",
    )(row0, codes)


def _dispatch_kernel(t_max, p1_ref, p2_ref, padlo_ref, padn_ref, nt_ref, xn_ref, xs_ref,
                     zero_s, sem, zsem):
    step = pl.program_id(0)
    base = step * ROW_TILE

    def pad_copy(row):
        return pltpu.make_async_copy(_rec(zero_s, 0), _rec(xs_ref, row), zsem)

    def tile_copy(t):
        rows = EXP_TILE * REC_ROWS
        return pltpu.make_async_copy(zero_s, xs_ref.at[pl.ds(pl.multiple_of(t * rows, rows), rows)], zsem)

    def for_pad_rows(fn):
        def per_expert(e, c):
            lo = padlo_ref[e]
            lax.fori_loop(0, padn_ref[e], lambda r, cc: (fn(pad_copy(lo + r)), cc)[1], 0)
            return c
        lax.fori_loop(0, N_EXPERTS, per_expert, 0)
        lax.fori_loop(nt_ref[0], t_max, lambda t, c: (fn(tile_copy(t)), c)[1], 0)

    @pl.when(step == 0)
    def _():
        _zero_records_kernel(zero_s)
        for_pad_rows(lambda cp: cp.start(priority=1))

    def copies(r):
        src = _rec(xn_ref, r)
        return (pltpu.make_async_copy(src, _rec(xs_ref, p1_ref[base + r]), sem),
                pltpu.make_async_copy(src, _rec(xs_ref, p2_ref[base + r]), sem))

    def start(g, c):
        for u in range(DMA_UNROLL):
            for prio, cp in enumerate(copies(g * DMA_UNROLL + u)):
                cp.start(priority=prio)
        return c

    def wait(g, c):
        for u in range(DMA_UNROLL):
            for cp in copies(g * DMA_UNROLL + u):
                cp.wait()
        return c

    lax.fori_loop(0, ROW_TILE // DMA_UNROLL, start, 0)
    lax.fori_loop(0, ROW_TILE // DMA_UNROLL, wait, 0)

    @pl.when(step == 0)
    def _():
        for_pad_rows(lambda cp: cp.wait())


def _dispatch(pos1, pos2, pad_lo, pad_n, n_tiles, xn_rec, t_max):
    n = pos1.shape[0]
    return pl.pallas_call(
        functools.partial(_dispatch_kernel, t_max),
        grid_spec=pltpu.PrefetchScalarGridSpec(
            num_scalar_prefetch=5,
            grid=(n // ROW_TILE,),
            in_specs=[pl.BlockSpec((ROW_TILE * REC_ROWS, LANES), lambda i, *_: (i, 0))],
            out_specs=pl.BlockSpec(memory_space=pl.ANY),
            scratch_shapes=[pltpu.VMEM((EXP_TILE * REC_ROWS, LANES), jnp.uint32),
                            pltpu.SemaphoreType.DMA, pltpu.SemaphoreType.DMA],
        ),
        out_shape=jax.ShapeDtypeStruct((t_max * EXP_TILE * REC_ROWS, LANES), jnp.uint32),
        compiler_params=pltpu.CompilerParams(
            dimension_semantics=("arbitrary",), vmem_limit_bytes=VMEM_LIMIT),
        name="moe_dispatch",
    )(pos1, pos2, pad_lo, pad_n, n_tiles, xn_rec)


def _experts_kernel(te_ref, nt_ref, xs_ref, wg_ref, wu_ref, wd_ref, ys_ref, wg_s, wu_s, wd_s):
    t = pl.program_id(0)
    prev = te_ref[jnp.maximum(t - 1, 0)]

    @pl.when((t == 0) | (te_ref[t] != prev))
    def _():
        wg_s[...] = wg_ref[0].astype(BF16)
        wu_s[...] = wu_ref[0].astype(BF16)
        wd_s[...] = wd_ref[0].astype(BF16)

    @pl.when(t < nt_ref[0])
    def _():
        lo, hi = _unpack_rows(xs_ref, EXP_TILE)
        x = jnp.concatenate([lo.astype(BF16), hi.astype(BF16)], axis=1)
        hg = _dot(x, wg_s[...])
        hu = _dot(x, wu_s[...])
        act = hg * _sigmoid(hg) * hu
        _pack_rows(ys_ref, _dot(act.astype(BF16), wd_s[...]))

    @pl.when(t >= nt_ref[0])
    def _():
        _zero_records_kernel(ys_ref)


def _experts(tile_expert, n_tiles, xs, wg, wu, wd):
    t_max = tile_expert.shape[0]

    def tmap(t, te, nt):
        return (jnp.minimum(t, nt[0] - 1), 0)

    def wmap(t, te, nt):
        return (te[t], 0, 0)

    return pl.pallas_call(
        _experts_kernel,
        grid_spec=pltpu.PrefetchScalarGridSpec(
            num_scalar_prefetch=2,
            grid=(t_max,),
            in_specs=[pl.BlockSpec((EXP_TILE * REC_ROWS, LANES), tmap),
                      pl.BlockSpec((1, D_MODEL, D_EXPERT), wmap),
                      pl.BlockSpec((1, D_MODEL, D_EXPERT), wmap),
                      pl.BlockSpec((1, D_EXPERT, D_MODEL), wmap)],
            out_specs=pl.BlockSpec((EXP_TILE * REC_ROWS, LANES), lambda t, te, nt: (t, 0)),
            scratch_shapes=[pltpu.VMEM((D_MODEL, D_EXPERT), BF16), pltpu.VMEM((D_MODEL, D_EXPERT), BF16),
                            pltpu.VMEM((D_EXPERT, D_MODEL), BF16)],
        ),
        out_shape=jax.ShapeDtypeStruct(xs.shape, xs.dtype),
        compiler_params=pltpu.CompilerParams(
            dimension_semantics=("arbitrary",), vmem_limit_bytes=VMEM_LIMIT),
        name="moe_experts",
    )(tile_expert, n_tiles, xs, wg, wu, wd)


def _combine_kernel(n_prompt_tiles, p1_ref, p2_ref, x1_ref, route_ref, g_ref, ys_ref,
                    yp_ref, ysm_ref, a_s, b_s, sem):
    i = pl.program_id(0)
    n_steps = pl.num_programs(0)

    def copies(step, slot, r):
        tok = step * ROW_TILE + r
        return (pltpu.make_async_copy(_rec(ys_ref, p1_ref[tok]), _rec(a_s.at[slot], r), sem.at[slot]),
                pltpu.make_async_copy(_rec(ys_ref, p2_ref[tok]), _rec(b_s.at[slot], r), sem.at[slot]))

    def start_all(step, slot):
        def start(g, c):
            for u in range(DMA_UNROLL):
                for prio, cp in enumerate(copies(step, slot, g * DMA_UNROLL + u)):
                    cp.start(priority=prio)
            return c
        lax.fori_loop(0, ROW_TILE // DMA_UNROLL, start, 0)

    def wait_all(step, slot):
        def wait(g, c):
            for u in range(DMA_UNROLL):
                for cp in copies(step, slot, g * DMA_UNROLL + u):
                    cp.wait()
            return c
        lax.fori_loop(0, ROW_TILE // DMA_UNROLL, wait, 0)

    slot = i % 2

    @pl.when(i == 0)
    def _():
        start_all(0, 0)

    @pl.when(i + 1 < n_steps)
    def _():
        start_all(i + 1, 1 - slot)

    wait_all(i, slot)
    a_lo, a_hi = _unpack_rows(a_s.at[slot], ROW_TILE)
    b_lo, b_hi = _unpack_rows(b_s.at[slot], ROW_TILE)
    route = route_ref[...]
    w1 = route[:, 4:5]
    w2 = route[:, 5:6]
    moe = jnp.concatenate([w1 * a_lo + w2 * b_lo, w1 * a_hi + w2 * b_hi], axis=1)
    y = _rmsnorm(x1_ref[...] + moe, g_ref[...])

    @pl.when(i < n_prompt_tiles)
    def _():
        yp_ref[...] = y

    @pl.when(i >= n_prompt_tiles)
    def _():
        ysm_ref[...] = y


def _combine(pos1, pos2, x1, route, g, ys, n_p):
    n = x1.shape[0]
    npt = n_p // ROW_TILE

    def omap(i, *_):
        return (i, 0)

    return pl.pallas_call(
        functools.partial(_combine_kernel, npt),
        grid_spec=pltpu.PrefetchScalarGridSpec(
            num_scalar_prefetch=2,
            grid=(n // ROW_TILE,),
            in_specs=[pl.BlockSpec((ROW_TILE, D_MODEL), omap),
                      pl.BlockSpec((ROW_TILE, LANES), omap),
                      pl.BlockSpec((1, D_MODEL), lambda i, *_: (0, 0)),
                      pl.BlockSpec(memory_space=pl.ANY)],
            out_specs=[pl.BlockSpec((ROW_TILE, D_MODEL), lambda i, *_: (jnp.minimum(i, npt - 1), 0)),
                       pl.BlockSpec((ROW_TILE, D_MODEL), lambda i, *_: (jnp.maximum(i - npt, 0), 0))],
            scratch_shapes=[pltpu.VMEM((2, ROW_TILE * REC_ROWS, LANES), jnp.uint32),
                            pltpu.VMEM((2, ROW_TILE * REC_ROWS, LANES), jnp.uint32),
                            pltpu.SemaphoreType.DMA((2,))],
        ),
        out_shape=[jax.ShapeDtypeStruct((n_p, D_MODEL), F32),
                   jax.ShapeDtypeStruct((n - n_p, D_MODEL), F32)],
        compiler_params=pltpu.CompilerParams(
            dimension_semantics=("arbitrary",), vmem_limit_bytes=VMEM_LIMIT),
        name="moe_combine",
    )(pos1, pos2, x1, route, g, ys)


def _pad_lanes(w):
    return jnp.pad(w, ((0, 0), (0, LANES - w.shape[1])))


def _moe_plan(counts, n_tokens):
    cnt = counts[0, ROUTER_E0:ROUTER_E0 + N_EXPERTS].astype(jnp.int32)
    tiles = (cnt + EXP_TILE - 1) // EXP_TILE
    tile_end = jnp.cumsum(tiles)
    row0 = (tile_end - tiles) * EXP_TILE
    t_max = 2 * n_tokens // EXP_TILE + N_EXPERTS
    tile_ids = jnp.arange(t_max, dtype=jnp.int32)
    tile_expert = jnp.minimum(
        jnp.sum((tile_ids[:, None] >= tile_end[None, :]).astype(jnp.int32), axis=1), N_EXPERTS - 1)
    return row0, row0 + cnt, tiles * EXP_TILE - cnt, tile_expert, tile_end[-1:].astype(jnp.int32), t_max


def kernel(x_prompt, x_sample, state_gla_S, state_mlstm_C, state_mlstm_n, state_mlstm_m, cache_mlstm_conv, g_mix_norm, w_in, w_gla_gate_up, b_gla_gate_up, g_gla_out, w_mlstm_conv, b_mlstm_conv, b_mlstm_i, b_mlstm_f, g_mlstm_out, w_out, g_ffn_norm, w_router_group, w_router_expert, w_exp_gate, w_exp_up, w_exp_down, g_final):
    depth = w_in.shape[0]
    assert depth == 1
    bp, t_p, _ = x_prompt.shape
    bs, t_s, _ = x_sample.shape
    assert bp == 1
    xp = x_prompt.reshape(bp * t_p, D_MODEL)
    xs = x_sample.reshape(bs * t_s, D_MODEL)
    n_p = xp.shape[0]

    wt = jnp.transpose(w_in[0])
    c_gz = 3072
    c_mqk = c_gz + GLA_GATE_RANK
    c_mi = c_mqk + 3072
    w_big = _pack_inproj_weight(wt, c_gz, c_mqk)
    ws = _hilo_cols(_pad_lanes(jnp.concatenate([wt[c_gz:c_mqk], wt[c_mi:]], axis=0).T))
    wz = jnp.pad(w_gla_gate_up[0], ((0, LANES - GLA_GATE_RANK), (0, 0)))
    wz_hi, wz_lo = _split2(wz)
    bsm = _pad_lanes(jnp.concatenate(
        [jnp.zeros((1, GLA_GATE_RANK), F32), b_mlstm_i[0][None], b_mlstm_f[0][None]], axis=1))
    mix_w = (wz_hi, wz_lo, b_gla_gate_up[0][None], g_gla_out[0][None], w_mlstm_conv[0],
             b_mlstm_conv[0][None], bsm, g_mlstm_out[0][None])
    wr = _hilo_cols(_pad_lanes(jnp.concatenate([w_router_group[0], w_router_expert[0]], axis=1)))

    g_mix = g_mix_norm[0][None]
    p_big, p_small = _inproj(xs, g_mix, w_big, ws)

    dt = x_prompt.dtype
    z_s = jnp.zeros((bp, N_HEADS, DK, DV), dt)
    z_n = jnp.zeros((bp, N_HEADS, DK), dt)
    z_m = jnp.zeros((bp, 1, N_HEADS), dt)
    z_cv = jnp.zeros((bp, CONV_W - 1, QK_CH), dt)
    o_p, p_S, p_C, p_n, p_m, p_cv = _prompt_mixers(
        xp, g_mix, w_big, ws, MIX_ROWS, z_s, z_s, z_n, z_m, z_cv, mix_w)
    o_s, s_S, s_C, s_n, s_m, s_cv = _sample_mixers(
        p_big, p_small, bs, t_s, SAMPLE_PAR, state_gla_S[0], state_mlstm_C[0], state_mlstm_n[0],
        state_mlstm_m[0][:, None, :], cache_mlstm_conv[0], mix_w)

    x1, xn_rec, route, codes, counts = _outproj(
        o_p, o_s, xp, xs, w_out[0].astype(BF16), g_ffn_norm[0][None], wr)
    row0, pad_lo, pad_n, tile_expert, n_tiles, t_max = _moe_plan(counts, x1.shape[0])
    pos = _slot_rows(row0, codes)
    pos1 = pos[:, 0, :].reshape(-1)
    pos2 = pos[:, 1, :].reshape(-1)
    xs_rec = _dispatch(pos1, pos2, pad_lo, pad_n, n_tiles, xn_rec, t_max)
    ys_rec = _experts(tile_expert, n_tiles, xs_rec, w_exp_gate[0], w_exp_up[0], w_exp_down[0])
    y_p, y_s = _combine(pos1, pos2, x1, route, g_final[None], ys_rec, n_p)

    return (y_p.reshape(x_prompt.shape), y_s.reshape(x_sample.shape),
            p_S[None], p_C[None], p_n[None], p_m[:, 0, :N_HEADS][None], p_cv[None],
            s_S[None], s_C[None], s_n[None], s_m[:, 0, :N_HEADS][None], s_cv[None])
```

```python
import functools

import jax
import jax.numpy as jnp
from jax import lax
from jax.experimental import pallas as pl
from jax.experimental.pallas import tpu as pltpu

F32 = jnp.float32
BF16 = jnp.bfloat16

D_MODEL = 2048
N_HEADS = 4
DK = 128
DV = 256
GLA_GATE_RANK = 16
GLA_GATE_NORM = 16.0
CONV_W = 4
QK_CH = 2 * N_HEADS * DK
N_GROUPS = 4
EXPERTS_PER_GROUP = 8
N_EXPERTS = N_GROUPS * EXPERTS_PER_GROUP
D_EXPERT = 256
EPS = 1e-6

LANES = 128
GLA_SUB = 16
GLA_SAFE_DECAY = 60.0
MIX_ROWS = 256
SAMPLE_PAR = 4
N_MIX_OUT = 6
ROW_TILE = 512
PROJ_BIG = 6 * 1024
PROJ_CHUNK = 1024
PACK_COLS = 256
LANE_I = GLA_GATE_RANK
LANE_F = GLA_GATE_RANK + N_HEADS
ROUTER_E0 = N_GROUPS
HALF = D_MODEL // 2
REC_ROWS = HALF // LANES
EXP_TILE = 512
DMA_UNROLL = 8
SLOT_CODE_BITS = 16
SLOT_CODE = 1 << SLOT_CODE_BITS

VMEM_LIMIT = 56 * 1024 * 1024
VMEM_LIMIT_FUSED = 62 * 1024 * 1024


def _dot(a, b):
    return jnp.dot(a, b, preferred_element_type=F32)


def _dot_nt(a, b):
    return lax.dot_general(a, b, (((1,), (1,)), ((), ())), preferred_element_type=F32)


def _dot_tn(a, b):
    return lax.dot_general(a, b, (((0,), (0,)), ((), ())), preferred_element_type=F32)


def _split2(x):
    hi = x.astype(BF16)
    lo = (x - hi.astype(F32)).astype(BF16)
    return hi, lo


def _split3(x):
    hi = x.astype(BF16)
    r = x - hi.astype(F32)
    mid = r.astype(BF16)
    lo = (r - mid.astype(F32)).astype(BF16)
    return hi, mid, lo


def _dot_exact_lhs(m_bf16, x):
    hi, mid, lo = _split3(x)
    return _dot(m_bf16, hi) + _dot(m_bf16, mid) + _dot(m_bf16, lo)


def _dot_hilo(a_hi, a_lo, b_hi, b_lo):
    return _dot(a_hi, b_hi) + _dot(a_lo, b_hi) + _dot(a_hi, b_lo)


def _hilo_cols(w):
    return jnp.concatenate(_split2(w), axis=1)


def _dot_hilo_cols(a_hi, a_lo, b_ref):
    p = _dot(a_hi, b_ref[...])
    return (p[:, LANES:] + _dot(a_lo, b_ref[:, :LANES])) + p[:, :LANES]


def _log_sigmoid(z):
    return jnp.minimum(z, 0.0) - jnp.log1p(jnp.exp(-jnp.abs(z)))


def _sigmoid(z):
    return 0.5 * jnp.tanh(0.5 * z) + 0.5


def _rmsnorm(x, g):
    return x * lax.rsqrt(jnp.mean(x * x, axis=-1, keepdims=True) + EPS) * g


def _pack_inproj_weight_kernel(c_gz, c_mqk, wt_ref, o_ref, buf, sem):
    j = pl.program_id(0)
    n_j = pl.num_programs(0)

    def copy(jj):
        col0 = jj * PACK_COLS
        row = pl.multiple_of(col0 + jnp.where(col0 >= c_gz, c_mqk - c_gz, 0), 16)
        return pltpu.make_async_copy(wt_ref.at[pl.ds(row, PACK_COLS)], buf.at[jj % 2], sem.at[jj % 2])

    @pl.when(j == 0)
    def _():
        copy(0).start()

    @pl.when(j + 1 < n_j)
    def _():
        copy(j + 1).start()

    copy(j).wait()
    r_i = lax.broadcasted_iota(jnp.int32, (PACK_COLS, PACK_COLS), 0)
    c_i = lax.broadcasted_iota(jnp.int32, (PACK_COLS, PACK_COLS), 1)
    eye = jnp.where(r_i == c_i, 1.0, 0.0).astype(BF16)
    o_ref[...] = _dot_tn(buf[j % 2].astype(BF16), eye).astype(o_ref.dtype)


def _pack_inproj_weight(wt, c_gz, c_mqk):
    assert c_gz % PACK_COLS == 0 and (c_mqk - c_gz) % 16 == 0
    return pl.pallas_call(
        functools.partial(_pack_inproj_weight_kernel, c_gz, c_mqk),
        grid=(PROJ_BIG // PACK_COLS,),
        in_specs=[pl.BlockSpec(memory_space=pl.ANY)],
        out_specs=pl.BlockSpec((wt.shape[1], PACK_COLS), lambda j: (0, j)),
        out_shape=jax.ShapeDtypeStruct((wt.shape[1], PROJ_BIG), BF16),
        scratch_shapes=[pltpu.VMEM((2, PACK_COLS, wt.shape[1]), F32), pltpu.SemaphoreType.DMA((2,))],
        compiler_params=pltpu.CompilerParams(dimension_semantics=("arbitrary",), vmem_limit_bytes=VMEM_LIMIT),
        name="pack_inproj_weight",
    )(wt)


def _project(x_ref, g_ref, w_ref, ws_ref, p_ref, ps_ref, h_ref, chunks, with_norm):
    if with_norm:
        y_hi, y_lo = _split2(_rmsnorm(x_ref[...], g_ref[...]))
        h_ref[...] = y_hi
        ps_ref[...] = _dot_hilo_cols(y_hi, y_lo, ws_ref)
    else:
        y_hi = h_ref[...]
    for c in chunks:
        cols = slice(c * PROJ_CHUNK, (c + 1) * PROJ_CHUNK)
        p_ref[:, cols] = _dot(y_hi, w_ref[:, cols]).astype(p_ref.dtype)


PROJ_CHUNKS = tuple(range(PROJ_BIG // PROJ_CHUNK))


def _inproj_kernel(x_ref, g_ref, w_ref, ws_ref, p_ref, ps_ref, h_s):
    _project(x_ref, g_ref, w_ref, ws_ref, p_ref, ps_ref, h_s, PROJ_CHUNKS, True)


def _inproj(x, g, w_big, ws):
    n = x.shape[0]
    assert n % ROW_TILE == 0
    once = pl.Buffered(1)
    return pl.pallas_call(
        _inproj_kernel,
        grid=(n // ROW_TILE,),
        in_specs=[
            pl.BlockSpec((ROW_TILE, D_MODEL), lambda i: (i, 0)),
            pl.BlockSpec((1, D_MODEL), lambda i: (0, 0)),
            pl.BlockSpec((D_MODEL, PROJ_BIG), lambda i: (0, 0), pipeline_mode=once),
            pl.BlockSpec((D_MODEL, 2 * LANES), lambda i: (0, 0), pipeline_mode=once),
        ],
        out_specs=[
            pl.BlockSpec((ROW_TILE, PROJ_BIG), lambda i: (i, 0)),
            pl.BlockSpec((ROW_TILE, LANES), lambda i: (i, 0)),
        ],
        out_shape=[
            jax.ShapeDtypeStruct((n, PROJ_BIG), BF16),
            jax.ShapeDtypeStruct((n, LANES), F32),
        ],
        scratch_shapes=[pltpu.VMEM((ROW_TILE, D_MODEL), BF16)],
        compiler_params=pltpu.CompilerParams(
            dimension_semantics=("arbitrary",), vmem_limit_bytes=VMEM_LIMIT),
        name="inproj",
    )(x, g, w_big, ws)


def _run_streams(streams):
    stages = [next(s) for s in streams]
    ok = stages[0][0]
    for st in stages[1:]:
        ok = jnp.logical_and(ok, st[0])

    @pl.when(ok)
    def _():
        for st in stages:
            st[1]()

    @pl.when(jnp.logical_not(ok))
    def _():
        for st in stages:
            st[2]()

    for s in streams:
        for _ in s:
            pass


def _mixer_kernel(rows, side_work, *refs):
    _run_streams([_mixer_stream(rows, False, side_work, *refs)])


def _mixer_stream(rows, single, side_work, gq_ref, gk_ref, gv_ref, gg_ref, mqk_ref, mv_ref, mo_ref, sm_ref,
                  s0_ref, c0_ref, n0_ref, m0_ref, cv0_ref,
                  wz_hi_ref, wz_lo_ref, bz_ref, g_gla_ref, wc_ref, bc_ref, bsm_ref, g_ml_ref,
                  o_ref, s_out_ref, c_out_ref, n_out_ref, m_out_ref, cv_out_ref,
                  st_s, c_s, n_s, m_s, cb_s, b_s):
    blk = pl.program_id(1)
    n_blk = pl.num_programs(1)

    def maybe(cond, fn):
        if single:
            fn()
        else:
            pl.when(cond)(fn)

    def init():
        for h in range(N_HEADS):
            st_s[h] = s0_ref[0, h].T
            c_s[h] = c0_ref[0, h]
            n_s[h] = n0_ref[0, h:h + 1, :]
            m_s[h] = jnp.broadcast_to(m0_ref[0, :, h:h + 1], (1, LANES))
        cb_s[0:8, :] = jnp.zeros((8, QK_CH), F32)
        cb_s[8 - (CONV_W - 1):8, :] = cv0_ref[0]

    maybe(blk == 0, init)

    small = sm_ref[...]
    row_i = lax.broadcasted_iota(jnp.int32, (rows, rows), 0)
    col_i = lax.broadcasted_iota(jnp.int32, (rows, rows), 1)
    causal = col_i <= row_i

    sm_hi, sm_lo = _split2(small)
    z = _dot_hilo(sm_hi, sm_lo, wz_hi_ref[...], wz_lo_ref[...]) + bz_ref[...]
    log_a = _log_sigmoid(z) * (1.0 / GLA_GATE_NORM)
    tri = jnp.where(causal, 1.0, 0.0).astype(BF16)
    b_blk = _dot_exact_lhs(tri, log_a)
    b_last = b_blk[rows - 1:rows, :]
    factorable = jnp.min(b_last) >= -GLA_SAFE_DECAY
    side_work[0]()

    def gla_finish(o, vc, gate):
        y = o * lax.rsqrt(jnp.mean(o * o, axis=-1, keepdims=True) + EPS) * g_gla_ref[:, vc]
        return (y * (gate * _sigmoid(gate))).astype(o_ref.dtype)

    def gla_fast():
        for h in range(N_HEADS):
            kc = slice(h * DK, (h + 1) * DK)
            vc = slice(h * DV, (h + 1) * DV)
            bh = b_blk[:, kc]
            qh = gq_ref[:, kc].astype(F32) * (DK ** -0.5)
            kh = gk_ref[:, kc].astype(F32)
            vh = gv_ref[:, vc]
            st = st_s[h]
            q_dec = (qh * jnp.exp(bh)).astype(BF16)
            k_inv = (kh * jnp.exp(-bh)).astype(BF16)
            a = jnp.where(causal, _dot_nt(q_dec, k_inv), 0.0)
            o = _dot(a.astype(BF16), vh) + _dot_nt(q_dec, st.astype(BF16))
            bh_end = b_last[:, kc]
            k_dec = (kh * jnp.exp(bh_end - bh)).astype(BF16)
            st_s[h] = st * jnp.exp(bh_end) + _dot_tn(vh, k_dec)
            o_ref[:, vc] = gla_finish(o, vc, gg_ref[:, vc].astype(F32))

    sub_r = lax.broadcasted_iota(jnp.int32, (GLA_SUB, LANES), 0)
    sub_l = lax.broadcasted_iota(jnp.int32, (GLA_SUB, LANES), 1)

    def gla_sub(c, carry):
        r0 = pl.multiple_of(c * GLA_SUB, GLA_SUB)
        rs = pl.ds(r0, GLA_SUB)
        for h in range(N_HEADS):
            kc = slice(h * DK, (h + 1) * DK)
            vc = slice(h * DV, (h + 1) * DV)
            bh = b_s[rs, kc]
            qh = gq_ref[rs, kc].astype(F32) * (DK ** -0.5)
            kh = gk_ref[rs, kc].astype(F32)
            vh = gv_ref[rs, vc]
            st = st_s[h]
            o = _dot_nt((qh * jnp.exp(bh)).astype(BF16), st.astype(BF16))
            a = jnp.zeros((GLA_SUB, LANES), F32)
            for s in range(GLA_SUB):
                e = jnp.exp(jnp.minimum(bh - bh[s:s + 1, :], 0.0))
                col = jnp.sum(qh * (kh[s:s + 1, :] * e), axis=1, keepdims=True)
                a = jnp.where((sub_l == s) & (sub_r >= s), col, a)
            o = o + _dot(a[:, :GLA_SUB].astype(BF16), vh)
            b_end = bh[GLA_SUB - 1:GLA_SUB, :]
            k_dec = kh * jnp.exp(b_end - bh)
            st_s[h] = st * jnp.exp(b_end) + _dot_tn(vh, k_dec.astype(BF16))
            o_ref[rs, vc] = gla_finish(o, vc, gg_ref[rs, vc].astype(F32))
        return carry

    def gla_slow():
        same_sub = (row_i // GLA_SUB) == (col_i // GLA_SUB)
        blk_tri = jnp.where(causal & same_sub, 1.0, 0.0).astype(BF16)
        b_s[...] = _dot_exact_lhs(blk_tri, log_a)
        lax.fori_loop(0, rows // GLA_SUB, gla_sub, 0)

    yield factorable, gla_fast, gla_slow

    cb_s[8:8 + rows, :] = mqk_ref[...].astype(F32)
    conv = bc_ref[...]
    for j in range(CONV_W):
        conv = conv + cb_s[8 - (CONV_W - 1) + j:8 - (CONV_W - 1) + j + rows, :] * wc_ref[j:j + 1, :]
    cb_s[0:8, :] = cb_s[rows:rows + 8, :]
    qk = conv * _sigmoid(conv)

    pre = small + bsm_ref[...]
    log_f = _log_sigmoid(pre)
    f_cum =_dot_exact_lhs(tri, log_f)
    eye = jnp.where(row_i == col_i, 1.0, 0.0).astype(BF16)
    f_cum_t = sum(_dot_tn(p, eye) for p in _split3(f_cum))
    pre_t = sum(_dot_tn(p, eye) for p in _split3(pre))
    side_work[1]()
    lane_1 = lax.broadcasted_iota(jnp.int32, (1, LANES), 1)
    m_new = jnp.zeros((1, LANES), F32)

    for h in range(N_HEADS):
        kc = slice(h * DK, (h + 1) * DK)
        vc = slice(h * DV, (h + 1) * DV)
        f_col = f_cum[:, LANE_F + h:LANE_F + h + 1]
        i_col = pre[:, LANE_I + h:LANE_I + h + 1]
        f_row = f_cum_t[LANE_F + h:LANE_F + h + 1, :]
        i_row = pre_t[LANE_I + h:LANE_I + h + 1, :]
        dm = jnp.where(causal, f_col - f_row + i_row, -jnp.inf)
        m_prev = m_s[h][:, 0:1]
        inter = f_col + m_prev
        m_t = jnp.maximum(inter, jnp.max(dm, axis=1, keepdims=True))
        w = jnp.exp(dm - m_t)
        w_inter = jnp.exp(inter - m_t)
        q = qk[:, kc]
        k = qk[:, N_HEADS * DK + h * DK:N_HEADS * DK + (h + 1) * DK] * (DK ** -0.5)
        v = mv_ref[:, vc]
        q_b = q.astype(BF16)
        c_prev = c_s[h]
        n_prev = n_s[h]
        s_qk = _dot_nt(q_b, k.astype(BF16)) * w
        num = _dot(s_qk.astype(BF16), v) + w_inter * _dot(q_b, c_prev.astype(BF16))
        den = (jnp.sum(s_qk, axis=1, keepdims=True)
               + w_inter * jnp.sum(q * n_prev, axis=1, keepdims=True))
        hh = num / jnp.maximum(jnp.abs(den), jnp.exp(-m_t))
        m_end = m_t[rows - 1:rows, :]
        w_s = jnp.exp(f_col[rows - 1:rows, :] - f_col + i_col - m_end)
        dec = w_inter[rows - 1:rows, :]
        k_w = k * w_s
        c_s[h] = dec * c_prev + _dot_tn(k_w.astype(BF16), v)
        side_work[2 + h]()
        n_s[h] = dec * n_prev + jnp.sum(k_w, axis=0, keepdims=True)
        m_s[h] = jnp.broadcast_to(m_end, (1, LANES))
        m_new = jnp.where(lane_1 == h, m_end, m_new)
        y = hh * lax.rsqrt(jnp.mean(hh * hh, axis=-1, keepdims=True) + EPS) * g_ml_ref[:, vc]
        o_ref[:, N_HEADS * DV + h * DV:N_HEADS * DV + (h + 1) * DV] = (
            y * _sigmoid(mo_ref[:, vc].astype(F32))).astype(o_ref.dtype)

    def final():
        for h in range(N_HEADS):
            s_out_ref[0, h] = st_s[h].T
            c_out_ref[0, h] = c_s[h]
            n_out_ref[0, h:h + 1, :] = n_s[h]
        m_out_ref[0] = m_new
        cv_out_ref[0] = cb_s[8 - (CONV_W - 1):8, :]

    maybe(blk == n_blk - 1, final)


def _sample_kernel(rows, n_par, *refs):
    n_row_in, n_state_in, n_w = 8, 5, 8
    row_in, state_in = refs[:n_row_in], refs[n_row_in:n_row_in + n_state_in]
    weights = refs[n_row_in + n_state_in:n_row_in + n_state_in + n_w]
    outs = refs[n_row_in + n_state_in + n_w:]
    o_ref, state_out, scratch = outs[0], outs[1:N_MIX_OUT], outs[N_MIX_OUT:]
    no_side = (lambda: None,) * (2 + N_HEADS)
    streams = []
    for g in range(n_par):
        rs = pl.ds(g * rows, rows)
        streams.append(_mixer_stream(
            rows, True, no_side,
            *[r.at[rs] for r in row_in], *[r.at[pl.ds(g, 1)] for r in state_in], *weights,
            o_ref.at[rs], *[r.at[pl.ds(g, 1)] for r in state_out], *[s.at[g] for s in scratch]))
    _run_streams(streams)


def _sample_mixers(p_big, p_small, n_streams, rows, n_par, s0, c0, n0, m0, cv0, wts):
    assert n_streams % n_par == 0 and rows % GLA_SUB == 0
    blk_rows = n_par * rows

    def rmap(col):
        return lambda s, b: (s, col)

    def smap(*zeros):
        return lambda s, b: (s,) + zeros

    def wmap(s, b):
        return (0, 0)

    in_specs = [
        pl.BlockSpec((blk_rows, N_HEADS * DK), rmap(0)),
        pl.BlockSpec((blk_rows, N_HEADS * DK), rmap(1)),
        pl.BlockSpec((blk_rows, N_HEADS * DV), rmap(1)),
        pl.BlockSpec((blk_rows, N_HEADS * DV), rmap(2)),
        pl.BlockSpec((blk_rows, QK_CH), rmap(3)),
        pl.BlockSpec((blk_rows, N_HEADS * DV), rmap(4)),
        pl.BlockSpec((blk_rows, N_HEADS * DV), rmap(5)),
        pl.BlockSpec((blk_rows, LANES), rmap(0)),
        pl.BlockSpec((n_par, N_HEADS, DK, DV), smap(0, 0, 0)),
        pl.BlockSpec((n_par, N_HEADS, DK, DV), smap(0, 0, 0)),
        pl.BlockSpec((n_par, N_HEADS, DK), smap(0, 0)),
        pl.BlockSpec((n_par, 1, N_HEADS), smap(0, 0)),
        pl.BlockSpec((n_par, CONV_W - 1, QK_CH), smap(0, 0)),
    ] + [pl.BlockSpec(w.shape, wmap) for w in wts]
    out_shape = [
        jax.ShapeDtypeStruct((n_streams * rows, D_MODEL), BF16),
        jax.ShapeDtypeStruct((n_streams, N_HEADS, DK, DV), F32),
        jax.ShapeDtypeStruct((n_streams, N_HEADS, DK, DV), F32),
        jax.ShapeDtypeStruct((n_streams, N_HEADS, DK), F32),
        jax.ShapeDtypeStruct((n_streams, 1, LANES), F32),
        jax.ShapeDtypeStruct((n_streams, CONV_W - 1, QK_CH), F32),
    ]
    out_specs = [
        pl.BlockSpec((blk_rows, D_MODEL), smap(0)),
        pl.BlockSpec((n_par, N_HEADS, DK, DV), smap(0, 0, 0)),
        pl.BlockSpec((n_par, N_HEADS, DK, DV), smap(0, 0, 0)),
        pl.BlockSpec((n_par, N_HEADS, DK), smap(0, 0)),
        pl.BlockSpec((n_par, 1, LANES), smap(0, 0)),
        pl.BlockSpec((n_par, CONV_W - 1, QK_CH), smap(0, 0)),
    ]
    scratch = [
        pltpu.VMEM((n_par, N_HEADS, DV, DK), F32),
        pltpu.VMEM((n_par, N_HEADS, DK, DV), F32),
        pltpu.VMEM((n_par, N_HEADS, 1, DK), F32),
        pltpu.VMEM((n_par, N_HEADS, 1, LANES), F32),
        pltpu.VMEM((n_par, rows + 8, QK_CH), F32),
        pltpu.VMEM((n_par, rows, N_HEADS * DK), F32),
    ]
    return pl.pallas_call(
        functools.partial(_sample_kernel, rows, n_par),
        grid=(n_streams // n_par, 1),
        in_specs=in_specs, out_specs=out_specs, out_shape=out_shape,
        scratch_shapes=scratch,
        compiler_params=pltpu.CompilerParams(
            dimension_semantics=("arbitrary", "arbitrary"), vmem_limit_bytes=VMEM_LIMIT),
        name=f"mixers_r{rows}x{n_par}",
    )(p_big, p_big, p_big, p_big, p_big, p_big, p_big, p_small, s0, c0, n0, m0, cv0, *wts)


def _prompt_kernel(rows, x0_ref, xn_ref, g_ref, w_ref, ws_ref, *refs):
    mixer_refs, (p_buf, ps_buf, h_s) = refs[:-3], refs[-3:]
    blk = pl.program_id(1)
    cur = blk % 2

    def project(x_ref, slot, chunks, with_norm):
        _project(x_ref, g_ref, w_ref, ws_ref, p_buf.at[slot], ps_buf.at[slot], h_s, chunks, with_norm)

    @pl.when(blk == 0)
    def _():
        project(x0_ref, 0, PROJ_CHUNKS, True)

    p_cur = p_buf.at[cur]
    c0 = N_HEADS * DK
    views = [p_cur.at[:, lo:hi] for lo, hi in
             ((0, c0), (c0, 2 * c0), (1024, 2048), (2048, 3072), (3072, 4096), (4096, 5120), (5120, 6144))]
    side = ((lambda: project(xn_ref, 1 - cur, PROJ_CHUNKS[:2], True),)
            + tuple(functools.partial(project, xn_ref, 1 - cur, (c,), False) for c in PROJ_CHUNKS[2:])
            + (lambda: None,))
    assert len(side) == 2 + N_HEADS
    _mixer_kernel(rows, side, *views, ps_buf.at[cur], *mixer_refs)


def _prompt_mixers(xp, g, w_big, ws, rows, s0, c0, n0, m0, cv0, wts):
    t_len = xp.shape[0]
    assert t_len % rows == 0
    n_blk = t_len // rows
    once = pl.Buffered(1)

    def cmap(*zeros):
        return lambda s, b: zeros

    in_specs = [
        pl.BlockSpec((rows, D_MODEL), cmap(0, 0), pipeline_mode=once),
        pl.BlockSpec((rows, D_MODEL), lambda s, b: (jnp.minimum(b + 1, n_blk - 1), 0)),
        pl.BlockSpec((1, D_MODEL), cmap(0, 0)),
        pl.BlockSpec((D_MODEL, PROJ_BIG), cmap(0, 0), pipeline_mode=once),
        pl.BlockSpec((D_MODEL, 2 * LANES), cmap(0, 0), pipeline_mode=once),
        pl.BlockSpec((1, N_HEADS, DK, DV), cmap(0, 0, 0, 0)),
        pl.BlockSpec((1, N_HEADS, DK, DV), cmap(0, 0, 0, 0)),
        pl.BlockSpec((1, N_HEADS, DK), cmap(0, 0, 0)),
        pl.BlockSpec((1, 1, N_HEADS), cmap(0, 0, 0)),
        pl.BlockSpec((1, CONV_W - 1, QK_CH), cmap(0, 0, 0)),
    ] + [pl.BlockSpec(w.shape, cmap(0, 0)) for w in wts]
    out_shape = [
        jax.ShapeDtypeStruct((t_len, D_MODEL), BF16),
        jax.ShapeDtypeStruct((1, N_HEADS, DK, DV), F32),
        jax.ShapeDtypeStruct((1, N_HEADS, DK, DV), F32),
        jax.ShapeDtypeStruct((1, N_HEADS, DK), F32),
        jax.ShapeDtypeStruct((1, 1, LANES), F32),
        jax.ShapeDtypeStruct((1, CONV_W - 1, QK_CH), F32),
    ]
    out_specs = [
        pl.BlockSpec((rows, D_MODEL), lambda s, b: (b, 0)),
        pl.BlockSpec((1, N_HEADS, DK, DV), cmap(0, 0, 0, 0)),
        pl.BlockSpec((1, N_HEADS, DK, DV), cmap(0, 0, 0, 0)),
        pl.BlockSpec((1, N_HEADS, DK), cmap(0, 0, 0)),
        pl.BlockSpec((1, 1, LANES), cmap(0, 0, 0)),
        pl.BlockSpec((1, CONV_W - 1, QK_CH), cmap(0, 0, 0)),
    ]
    scratch = [
        pltpu.VMEM((N_HEADS, DV, DK), F32),
        pltpu.VMEM((N_HEADS, DK, DV), F32),
        pltpu.VMEM((N_HEADS, 1, DK), F32),
        pltpu.VMEM((N_HEADS, 1, LANES), F32),
        pltpu.VMEM((rows + 8, QK_CH), F32),
        pltpu.VMEM((rows, N_HEADS * DK), F32),
        pltpu.VMEM((2, rows, PROJ_BIG), BF16),
        pltpu.VMEM((2, rows, LANES), F32),
        pltpu.VMEM((rows, D_MODEL), BF16),
    ]
    return pl.pallas_call(
        functools.partial(_prompt_kernel, rows),
        grid=(1, n_blk),
        in_specs=in_specs, out_specs=out_specs, out_shape=out_shape,
        scratch_shapes=scratch,
        compiler_params=pltpu.CompilerParams(
            dimension_semantics=("arbitrary", "arbitrary"), vmem_limit_bytes=VMEM_LIMIT_FUSED),
        name="prompt_inproj_mixers",
    )(xp, xp, g, w_big, ws, s0, c0, n0, m0, cv0, *wts)


def _pack_rows(ref, val):
    word = pltpu.pack_elementwise([val[:, :HALF], val[:, HALF:]], packed_dtype=BF16)
    rows = val.shape[0]
    for j in range(REC_ROWS):
        ref[pl.ds(j, rows, stride=REC_ROWS), :] = word[:, j * LANES:(j + 1) * LANES]


def _unpack_rows(ref, rows):
    word = jnp.concatenate(
        [ref[pl.ds(j, rows, stride=REC_ROWS), :] for j in range(REC_ROWS)], axis=1)
    lo = pltpu.unpack_elementwise(word, index=0, packed_dtype=BF16, unpacked_dtype=F32)
    hi = pltpu.unpack_elementwise(word, index=1, packed_dtype=BF16, unpacked_dtype=F32)
    return lo, hi


def _outproj_kernel(n_prompt_tiles, op_ref, os_ref, xp_ref, xs_ref, w_ref, g_ref, wr_ref,
                    x1_ref, xn_ref, route_ref, code_ref, cnt_ref, cnt_s):
    i = pl.program_id(0)

    @pl.when(i == 0)
    def _():
        cnt_s[...] = jnp.zeros_like(cnt_s)

    def body(o_ref, x_ref):
        x1 = x_ref[...] + _dot(o_ref[...], w_ref[...])
        x1_ref[...] = x1
        xn = _rmsnorm(x1, g_ref[...])
        _pack_rows(xn_ref, xn)
        xn_hi, xn_lo = _split2(xn)
        lg = _dot_hilo_cols(xn_hi, xn_lo, wr_ref)
        lane = lax.broadcasted_iota(jnp.int32, lg.shape, 1).astype(F32)
        neg = -jnp.inf
        lgm = jnp.where(lane < N_GROUPS, lg, neg)
        mg = jnp.max(lgm, axis=1, keepdims=True)
        g_idx = jnp.min(jnp.where(lgm == mg, lane, float(LANES)), axis=1, keepdims=True)
        g_val = 1.0 / jnp.sum(jnp.where(lane < N_GROUPS, jnp.exp(lg - mg), 0.0), axis=1, keepdims=True)
        e0 = ROUTER_E0 + EXPERTS_PER_GROUP * g_idx
        le = jnp.where((lane >= e0) & (lane < e0 + EXPERTS_PER_GROUP), lg, neg)
        v1 = jnp.max(le, axis=1, keepdims=True)
        i1 = jnp.min(jnp.where(le == v1, lane, float(LANES)), axis=1, keepdims=True)
        le2 = jnp.where(lane == i1, neg, le)
        v2 = jnp.max(le2, axis=1, keepdims=True)
        i2 = jnp.min(jnp.where(le2 == v2, lane, float(LANES)), axis=1, keepdims=True)
        t = jnp.exp(v2 - v1)
        w1 = g_val / (1.0 + t)
        w2 = g_val * t / (1.0 + t)
        oh1 = lane == i1
        oh2 = lane == i2
        hot = jnp.where(oh1 | oh2, 1.0, 0.0)
        r_i = lax.broadcasted_iota(jnp.int32, (ROW_TILE, ROW_TILE), 0)
        c_i = lax.broadcasted_iota(jnp.int32, (ROW_TILE, ROW_TILE), 1)
        before = jnp.where(c_i < r_i, 1.0, 0.0).astype(BF16)
        seen = _dot(before, hot.astype(BF16)) + cnt_s[...]
        rank1 = jnp.sum(jnp.where(oh1, seen, 0.0), axis=1, keepdims=True)
        rank2 = jnp.sum(jnp.where(oh2, seen, 0.0), axis=1, keepdims=True)
        cnt_s[...] += jnp.sum(hot, axis=0, keepdims=True)
        code1 = (i1 - ROUTER_E0) * float(SLOT_CODE) + rank1
        code2 = (i2 - ROUTER_E0) * float(SLOT_CODE) + rank2
        route = jnp.zeros_like(lg)
        for k, col in ((0, code1), (1, code2), (4, w1), (5, w2)):
            route = jnp.where(lane == k, col, route)
        route_ref[...] = route
        code_ref[0] = route.T[0:8, :].astype(jnp.int32)

    @pl.when(i < n_prompt_tiles)
    def _():
        body(op_ref, xp_ref)

    @pl.when(i >= n_prompt_tiles)
    def _():
        body(os_ref, xs_ref)

    cnt_ref[...] = cnt_s[...]


def _outproj(o_p, o_s, xp, xs, w_out, g, wr):
    n_p, n_s = xp.shape[0], xs.shape[0]
    npt = n_p // ROW_TILE
    n = n_p + n_s

    def pmap(i):
        return (jnp.minimum(i, npt - 1), 0)

    def cmap(i):
        return (0, 0)

    def omap(i):
        return (i, 0)

    return pl.pallas_call(
        functools.partial(_outproj_kernel, npt),
        grid=(npt + 1,),
        in_specs=[
            pl.BlockSpec((ROW_TILE, D_MODEL), pmap), pl.BlockSpec((ROW_TILE, D_MODEL), cmap),
            pl.BlockSpec((ROW_TILE, D_MODEL), pmap), pl.BlockSpec((ROW_TILE, D_MODEL), cmap),
            pl.BlockSpec((D_MODEL, D_MODEL), cmap), pl.BlockSpec((1, D_MODEL), cmap),
            pl.BlockSpec((D_MODEL, 2 * LANES), cmap),
        ],
        out_specs=[
            pl.BlockSpec((ROW_TILE, D_MODEL), omap), pl.BlockSpec((ROW_TILE * REC_ROWS, LANES), omap),
            pl.BlockSpec((ROW_TILE, LANES), omap), pl.BlockSpec((1, 8, ROW_TILE), lambda i: (i, 0, 0)),
            pl.BlockSpec((1, LANES), cmap),
        ],
        out_shape=[
            jax.ShapeDtypeStruct((n, D_MODEL), F32),
            jax.ShapeDtypeStruct((n * REC_ROWS, LANES), jnp.uint32),
            jax.ShapeDtypeStruct((n, LANES), F32),
            jax.ShapeDtypeStruct((npt + 1, 8, ROW_TILE), jnp.int32),
            jax.ShapeDtypeStruct((1, LANES), F32),
        ],
        scratch_shapes=[pltpu.VMEM((1, LANES), F32)],
        compiler_params=pltpu.CompilerParams(
            dimension_semantics=("arbitrary",), vmem_limit_bytes=VMEM_LIMIT),
        name="outproj_router",
    )(o_p, o_s, xp, xs, w_out, g, wr)


def _rec(ref, idx):
    return ref.at[pl.ds(pl.multiple_of(idx * REC_ROWS, REC_ROWS), REC_ROWS)]


def _zero_records_kernel(o_ref):
    z = jnp.zeros(o_ref.shape, F32)
    o_ref[...] = pltpu.pack_elementwise([z, z], packed_dtype=BF16)


def _slot_rows_kernel(row0_ref, code_ref, pos_ref):
    code = code_ref[...]
    expert = lax.shift_right_logical(code, SLOT_CODE_BITS)
    first = jnp.zeros_like(code)
    for e in range(N_EXPERTS):
        first = jnp.where(expert == e, row0_ref[e], first)
    pos_ref[...] = first + (code & (SLOT_CODE - 1))


def _slot_rows(row0, codes):
    return pl.pallas_call(
        _slot_rows_kernel,
        grid_spec=pltpu.PrefetchScalarGridSpec(
            num_scalar_prefetch=1,
            grid=(1,),
            in_specs=[pl.BlockSpec(codes.shape, lambda i, r: (0, 0, 0))],
            out_specs=pl.BlockSpec(codes.shape, lambda i, r: (0, 0, 0)),
        ),
        out_shape=jax.ShapeDtypeStruct(codes.shape, codes.dtype),
        name="moe_slot_rows",
    )(row0, codes)


def _dispatch_kernel(t_max, p1_ref, p2_ref, padlo_ref, padn_ref, nt_ref, xn_ref, xs_ref,
                     zero_s, sem, zsem):
    step = pl.program_id(0)
    base = step * ROW_TILE

    def pad_copy(row, n_rec):
        src = zero_s.at[pl.ds(0, n_rec * REC_ROWS)]
        dst = xs_ref.at[pl.ds(pl.multiple_of(row * REC_ROWS, REC_ROWS), n_rec * REC_ROWS)]
        return pltpu.make_async_copy(src, dst, zsem)

    def tile_copy(t):
        rows = EXP_TILE * REC_ROWS
        return pltpu.make_async_copy(zero_s, xs_ref.at[pl.ds(pl.multiple_of(t * rows, rows), rows)], zsem)

    def for_pad_rows(fn):
        def per_expert(e, c):
            lo = padlo_ref[e]
            n = padn_ref[e]
            for bit in reversed(range(EXP_TILE.bit_length() - 1)):
                done = lax.shift_left(lax.shift_right_logical(n, bit + 1), bit + 1)
                pl.when((lax.shift_right_logical(n, bit) & 1) == 1)(
                    functools.partial(lambda b, d: fn(pad_copy(lo + d, 1 << b)), bit, done))
            return c
        lax.fori_loop(0, N_EXPERTS, per_expert, 0)
        lax.fori_loop(nt_ref[0], t_max, lambda t, c: (fn(tile_copy(t)), c)[1], 0)

    @pl.when(step == 0)
    def _():
        _zero_records_kernel(zero_s)
        for_pad_rows(lambda cp: cp.start(priority=1))

    def copies(r):
        src = _rec(xn_ref, r)
        return (pltpu.make_async_copy(src, _rec(xs_ref, p1_ref[base + r]), sem),
                pltpu.make_async_copy(src, _rec(xs_ref, p2_ref[base + r]), sem))

    def start(g, c):
        for u in range(DMA_UNROLL):
            for prio, cp in enumerate(copies(g * DMA_UNROLL + u)):
                cp.start(priority=prio)
        return c

    def wait(g, c):
        for u in range(DMA_UNROLL):
            for cp in copies(g * DMA_UNROLL + u):
                cp.wait()
        return c

    lax.fori_loop(0, ROW_TILE // DMA_UNROLL, start, 0)
    lax.fori_loop(0, ROW_TILE // DMA_UNROLL, wait, 0)

    @pl.when(step == 0)
    def _():
        for_pad_rows(lambda cp: cp.wait())


def _dispatch(pos1, pos2, pad_lo, pad_n, n_tiles, xn_rec, t_max):
    n = pos1.shape[0]
    return pl.pallas_call(
        functools.partial(_dispatch_kernel, t_max),
        grid_spec=pltpu.PrefetchScalarGridSpec(
            num_scalar_prefetch=5,
            grid=(n // ROW_TILE,),
            in_specs=[pl.BlockSpec((ROW_TILE * REC_ROWS, LANES), lambda i, *_: (i, 0))],
            out_specs=pl.BlockSpec(memory_space=pl.ANY),
            scratch_shapes=[pltpu.VMEM((EXP_TILE * REC_ROWS, LANES), jnp.uint32),
                            pltpu.SemaphoreType.DMA, pltpu.SemaphoreType.DMA],
        ),
        out_shape=jax.ShapeDtypeStruct((t_max * EXP_TILE * REC_ROWS, LANES), jnp.uint32),
        compiler_params=pltpu.CompilerParams(
            dimension_semantics=("arbitrary",), vmem_limit_bytes=VMEM_LIMIT),
        name="moe_dispatch",
    )(pos1, pos2, pad_lo, pad_n, n_tiles, xn_rec)


def _experts_kernel(te_ref, nt_ref, xs_ref, wg_ref, wu_ref, wd_ref, ys_ref, wg_s, wu_s, wd_s):
    t = pl.program_id(0)
    prev = te_ref[jnp.maximum(t - 1, 0)]

    @pl.when((t == 0) | (te_ref[t] != prev))
    def _():
        wg_s[...] = wg_ref[0].astype(BF16)
        wu_s[...] = wu_ref[0].astype(BF16)
        wd_s[...] = wd_ref[0].astype(BF16)

    @pl.when(t < nt_ref[0])
    def _():
        lo, hi = _unpack_rows(xs_ref, EXP_TILE)
        x = jnp.concatenate([lo.astype(BF16), hi.astype(BF16)], axis=1)
        hg = _dot(x, wg_s[...])
        hu = _dot(x, wu_s[...])
        act = hg * _sigmoid(hg) * hu
        _pack_rows(ys_ref, _dot(act.astype(BF16), wd_s[...]))

    @pl.when(t >= nt_ref[0])
    def _():
        _zero_records_kernel(ys_ref)


def _experts(tile_expert, n_tiles, xs, wg, wu, wd):
    t_max = tile_expert.shape[0]

    def tmap(t, te, nt):
        return (jnp.minimum(t, nt[0] - 1), 0)

    def wmap(t, te, nt):
        return (te[t], 0, 0)

    return pl.pallas_call(
        _experts_kernel,
        grid_spec=pltpu.PrefetchScalarGridSpec(
            num_scalar_prefetch=2,
            grid=(t_max,),
            in_specs=[pl.BlockSpec((EXP_TILE * REC_ROWS, LANES), tmap),
                      pl.BlockSpec((1, D_MODEL, D_EXPERT), wmap),
                      pl.BlockSpec((1, D_MODEL, D_EXPERT), wmap),
                      pl.BlockSpec((1, D_EXPERT, D_MODEL), wmap)],
            out_specs=pl.BlockSpec((EXP_TILE * REC_ROWS, LANES), lambda t, te, nt: (t, 0)),
            scratch_shapes=[pltpu.VMEM((D_MODEL, D_EXPERT), BF16), pltpu.VMEM((D_MODEL, D_EXPERT), BF16),
                            pltpu.VMEM((D_EXPERT, D_MODEL), BF16)],
        ),
        out_shape=jax.ShapeDtypeStruct(xs.shape, xs.dtype),
        compiler_params=pltpu.CompilerParams(
            dimension_semantics=("arbitrary",), vmem_limit_bytes=VMEM_LIMIT),
        name="moe_experts",
    )(tile_expert, n_tiles, xs, wg, wu, wd)


def _combine_kernel(n_prompt_tiles, p1_ref, p2_ref, x1_ref, route_ref, g_ref, ys_ref,
                    yp_ref, ysm_ref, a_s, b_s, sem):
    i = pl.program_id(0)
    n_steps = pl.num_programs(0)

    def copies(step, slot, r):
        tok = step * ROW_TILE + r
        return (pltpu.make_async_copy(_rec(ys_ref, p1_ref[tok]), _rec(a_s.at[slot], r), sem.at[slot]),
                pltpu.make_async_copy(_rec(ys_ref, p2_ref[tok]), _rec(b_s.at[slot], r), sem.at[slot]))

    def start_all(step, slot):
        def start(g, c):
            for u in range(DMA_UNROLL):
                for prio, cp in enumerate(copies(step, slot, g * DMA_UNROLL + u)):
                    cp.start(priority=prio)
            return c
        lax.fori_loop(0, ROW_TILE // DMA_UNROLL, start, 0)

    def wait_all(step, slot):
        def wait(g, c):
            for u in range(DMA_UNROLL):
                for cp in copies(step, slot, g * DMA_UNROLL + u):
                    cp.wait()
            return c
        lax.fori_loop(0, ROW_TILE // DMA_UNROLL, wait, 0)

    slot = i % 2

    @pl.when(i == 0)
    def _():
        start_all(0, 0)

    @pl.when(i + 1 < n_steps)
    def _():
        start_all(i + 1, 1 - slot)

    wait_all(i, slot)
    a_lo, a_hi = _unpack_rows(a_s.at[slot], ROW_TILE)
    b_lo, b_hi = _unpack_rows(b_s.at[slot], ROW_TILE)
    route = route_ref[...]
    w1 = route[:, 4:5]
    w2 = route[:, 5:6]
    moe = jnp.concatenate([w1 * a_lo + w2 * b_lo, w1 * a_hi + w2 * b_hi], axis=1)
    y = _rmsnorm(x1_ref[...] + moe, g_ref[...])

    @pl.when(i < n_prompt_tiles)
    def _():
        yp_ref[...] = y

    @pl.when(i >= n_prompt_tiles)
    def _():
        ysm_ref[...] = y


def _combine(pos1, pos2, x1, route, g, ys, n_p):
    n = x1.shape[0]
    npt = n_p // ROW_TILE

    def omap(i, *_):
        return (i, 0)

    return pl.pallas_call(
        functools.partial(_combine_kernel, npt),
        grid_spec=pltpu.PrefetchScalarGridSpec(
            num_scalar_prefetch=2,
            grid=(n // ROW_TILE,),
            in_specs=[pl.BlockSpec((ROW_TILE, D_MODEL), omap),
                      pl.BlockSpec((ROW_TILE, LANES), omap),
                      pl.BlockSpec((1, D_MODEL), lambda i, *_: (0, 0)),
                      pl.BlockSpec(memory_space=pl.ANY)],
            out_specs=[pl.BlockSpec((ROW_TILE, D_MODEL), lambda i, *_: (jnp.minimum(i, npt - 1), 0)),
                       pl.BlockSpec((ROW_TILE, D_MODEL), lambda i, *_: (jnp.maximum(i - npt, 0), 0))],
            scratch_shapes=[pltpu.VMEM((2, ROW_TILE * REC_ROWS, LANES), jnp.uint32),
                            pltpu.VMEM((2, ROW_TILE * REC_ROWS, LANES), jnp.uint32),
                            pltpu.SemaphoreType.DMA((2,))],
        ),
        out_shape=[jax.ShapeDtypeStruct((n_p, D_MODEL), F32),
                   jax.ShapeDtypeStruct((n - n_p, D_MODEL), F32)],
        compiler_params=pltpu.CompilerParams(
            dimension_semantics=("arbitrary",), vmem_limit_bytes=VMEM_LIMIT),
        name="moe_combine",
    )(pos1, pos2, x1, route, g, ys)


def _pad_lanes(w):
    return jnp.pad(w, ((0, 0), (0, LANES - w.shape[1])))


def _moe_plan(counts, n_tokens):
    cnt = counts[0, ROUTER_E0:ROUTER_E0 + N_EXPERTS].astype(jnp.int32)
    tiles = (cnt + EXP_TILE - 1) // EXP_TILE
    tile_end = jnp.cumsum(tiles)
    row0 = (tile_end - tiles) * EXP_TILE
    t_max = 2 * n_tokens // EXP_TILE + N_EXPERTS
    tile_ids = jnp.arange(t_max, dtype=jnp.int32)
    tile_expert = jnp.minimum(
        jnp.sum((tile_ids[:, None] >= tile_end[None, :]).astype(jnp.int32), axis=1), N_EXPERTS - 1)
    return row0, row0 + cnt, tiles * EXP_TILE - cnt, tile_expert, tile_end[-1:].astype(jnp.int32), t_max


def kernel(x_prompt, x_sample, state_gla_S, state_mlstm_C, state_mlstm_n, state_mlstm_m, cache_mlstm_conv, g_mix_norm, w_in, w_gla_gate_up, b_gla_gate_up, g_gla_out, w_mlstm_conv, b_mlstm_conv, b_mlstm_i, b_mlstm_f, g_mlstm_out, w_out, g_ffn_norm, w_router_group, w_router_expert, w_exp_gate, w_exp_up, w_exp_down, g_final):
    depth = w_in.shape[0]
    assert depth == 1
    bp, t_p, _ = x_prompt.shape
    bs, t_s, _ = x_sample.shape
    assert bp == 1
    xp = x_prompt.reshape(bp * t_p, D_MODEL)
    xs = x_sample.reshape(bs * t_s, D_MODEL)
    n_p = xp.shape[0]

    wt = jnp.transpose(w_in[0])
    c_gz = 3072
    c_mqk = c_gz + GLA_GATE_RANK
    c_mi = c_mqk + 3072
    w_big = _pack_inproj_weight(wt, c_gz, c_mqk)
    ws = _hilo_cols(_pad_lanes(jnp.concatenate([wt[c_gz:c_mqk], wt[c_mi:]], axis=0).T))
    wz = jnp.pad(w_gla_gate_up[0], ((0, LANES - GLA_GATE_RANK), (0, 0)))
    wz_hi, wz_lo = _split2(wz)
    bsm = _pad_lanes(jnp.concatenate(
        [jnp.zeros((1, GLA_GATE_RANK), F32), b_mlstm_i[0][None], b_mlstm_f[0][None]], axis=1))
    mix_w = (wz_hi, wz_lo, b_gla_gate_up[0][None], g_gla_out[0][None], w_mlstm_conv[0],
             b_mlstm_conv[0][None], bsm, g_mlstm_out[0][None])
    wr = _hilo_cols(_pad_lanes(jnp.concatenate([w_router_group[0], w_router_expert[0]], axis=1)))

    g_mix = g_mix_norm[0][None]
    p_big, p_small = _inproj(xs, g_mix, w_big, ws)

    dt = x_prompt.dtype
    z_s = jnp.zeros((bp, N_HEADS, DK, DV), dt)
    z_n = jnp.zeros((bp, N_HEADS, DK), dt)
    z_m = jnp.zeros((bp, 1, N_HEADS), dt)
    z_cv = jnp.zeros((bp, CONV_W - 1, QK_CH), dt)
    o_p, p_S, p_C, p_n, p_m, p_cv = _prompt_mixers(
        xp, g_mix, w_big, ws, MIX_ROWS, z_s, z_s, z_n, z_m, z_cv, mix_w)
    o_s, s_S, s_C, s_n, s_m, s_cv = _sample_mixers(
        p_big, p_small, bs, t_s, SAMPLE_PAR, state_gla_S[0], state_mlstm_C[0], state_mlstm_n[0],
        state_mlstm_m[0][:, None, :], cache_mlstm_conv[0], mix_w)

    x1, xn_rec, route, codes, counts = _outproj(
        o_p, o_s, xp, xs, w_out[0].astype(BF16), g_ffn_norm[0][None], wr)
    row0, pad_lo, pad_n, tile_expert, n_tiles, t_max = _moe_plan(counts, x1.shape[0])
    pos = _slot_rows(row0, codes)
    pos1 = pos[:, 0, :].reshape(-1)
    pos2 = pos[:, 1, :].reshape(-1)
    xs_rec = _dispatch(pos1, pos2, pad_lo, pad_n, n_tiles, xn_rec, t_max)
    ys_rec = _experts(tile_expert, n_tiles, xs_rec, w_exp_gate[0], w_exp_up[0], w_exp_down[0])
    y_p, y_s = _combine(pos1, pos2, x1, route, g_final[None], ys_rec, n_p)

    return (y_p.reshape(x_prompt.shape), y_s.reshape(x_sample.shape),
            p_S[None], p_C[None], p_n[None], p_m[:, 0, :N_HEADS][None], p_cv[None],
            s_S[None], s_C[None], s_n[None], s_m[:, 0, :N_HEADS][None], s_cv[None])
```
